```python
import jax, jax.numpy as jnp
from jax import lax
import numpy as np

D_MODEL = 1024
BATCH = 8
SEQ = 2048
DEPTH = 4
DEC_BATCH = 16
DEC_SEQ = 16
PAST_LEN = 1024

CHUNK = 64
SB_HEADS = 6
SB_DIM = 64
SB_WIDTH = SB_HEADS * SB_DIM
SB_BLOCK = 128
POOL_WINDOWS = (2, 4, 8, 16)
POOL_GROUPS = 4
POOL_GDIM = 64
POOL_WIDTH = POOL_GROUPS * POOL_GDIM
POOL_HIST = max(POOL_WINDOWS) - 1
GLA_HEADS = 4
GLA_DK = 96
GLA_DV = 96
GLA_KW = GLA_HEADS * GLA_DK
GLA_VW = GLA_HEADS * GLA_DV
GLA_RANK = 16
GLA_TAU = 16.0
N_BRANCH = 3
MEM_LEN = 256
X_HEADS = 4
X_DIM = D_MODEL // X_HEADS
FF = -(-8 * D_MODEL // (3 * 256)) * 256
EPS = 1e-6
IN_SPLITS = (SB_WIDTH, SB_WIDTH, SB_WIDTH, POOL_WIDTH, GLA_KW, GLA_KW, GLA_VW, GLA_VW, GLA_RANK, D_MODEL, D_MODEL, D_MODEL)
IN_WIDTH = 3 * SB_WIDTH + POOL_WIDTH + 2 * GLA_KW + 2 * GLA_VW + GLA_RANK + N_BRANCH * D_MODEL

kernel_name = 'stickbreak_pool_gla_hybrid_stream_step'


def rmsnorm(x, g):
    xf = x.astype(jnp.float32)
    y = xf * lax.rsqrt(jnp.mean(xf * xf, axis=-1, keepdims=True) + EPS)
    return (y * g.astype(jnp.float32)).astype(x.dtype)


def stick_breaking(q, k, v, q_pos, k_pos):
    z = jnp.einsum('bqhd,bkhd->bhqk', q, k).astype(jnp.float32) * (q.shape[-1] ** -0.5)
    mask = (k_pos[None, :] < q_pos[:, None])[None, None]
    log_beta = jax.nn.log_sigmoid(z)
    log_fail = jnp.where(mask, jax.nn.log_sigmoid(-z), 0.0)
    after = lax.cumsum(log_fail, axis=3, reverse=True) - log_fail
    w = jnp.where(mask, jnp.exp(log_beta + after), 0.0)
    return jnp.einsum('bhqk,bkhd->bqhd', w.astype(v.dtype), v)


def sb_prompt(q, k, v):
    B, T, H, D = q.shape
    nb = T // SB_BLOCK
    pos = jnp.arange(T)
    qb = q.reshape(B, nb, SB_BLOCK, H, D).swapaxes(0, 1)
    pb = pos.reshape(nb, SB_BLOCK)
    ob = lax.map(lambda a: stick_breaking(a[0], k, v, a[1], pos), (qb, pb))
    return ob.swapaxes(0, 1).reshape(B, T, H, D)


def pool_mix(u, hist, pos0):
    B, T, C = u.shape
    full = jnp.concatenate([hist.astype(u.dtype), u], axis=1)
    ff = full.astype(jnp.float32)
    cs = jnp.concatenate([jnp.zeros((B, 1, C), jnp.float32), jnp.cumsum(ff, axis=1)], axis=1)
    pos = pos0 + jnp.arange(T)
    P = POOL_HIST
    outs = []
    for g, w in enumerate(POOL_WINDOWS):
        sl = slice(g * POOL_GDIM, (g + 1) * POOL_GDIM)
        s = cs[:, P + 1:P + 1 + T, sl] - cs[:, P + 1 - w:P + 1 - w + T, sl]
        cnt = jnp.minimum(pos + 1, w).astype(jnp.float32)
        outs.append(s / cnt[None, :, None])
    pooled = jnp.concatenate(outs, axis=-1) - ff[:, P:]
    return pooled.astype(u.dtype), full[:, -POOL_HIST:]


def gla(q, k, v, log_a, S0, L):
    B, T, H, DK = q.shape
    DV = v.shape[-1]
    n = T // L

    def to_chunks(a):
        return a.reshape(B, n, L, H, a.shape[-1]).swapaxes(0, 1).astype(jnp.float32)

    causal = jnp.tril(jnp.ones((L, L), bool))[None, :, :, None, None]

    def step(S, inp):
        qc, kc, vc, ac = inp
        b = jnp.cumsum(ac, axis=1)
        o_inter = jnp.einsum('blhk,bhkv->blhv', qc * jnp.exp(b), S)
        rel = b[:, :, None] - b[:, None, :]
        decay = jnp.exp(jnp.where(causal, rel, -jnp.inf))
        att = jnp.einsum('bthk,bshk,btshk->bhts', qc, kc, decay)
        o_intra = jnp.einsum('bhts,bshv->bthv', att, vc)
        bL = b[:, -1]
        S_new = jnp.exp(bL)[..., None] * S + jnp.einsum('bshk,bshv->bhkv', kc * jnp.exp(bL[:, None] - b), vc)
        return S_new, o_inter + o_intra

    S, o = lax.scan(step, S0.astype(jnp.float32), (to_chunks(q), to_chunks(k), to_chunks(v), to_chunks(log_a)))
    return S, o.swapaxes(0, 1).reshape(B, T, H, DV)


def token_mixers(h, p, l, sb_past, pool_hist, S0, pos0, gla_chunk):
    B, T, _ = h.shape
    idx = np.cumsum(np.array(IN_SPLITS))[:-1].tolist()
    qa, ka, va, u, qc, kc, vc, gc, lr, ga, gb, gcg = jnp.split(h @ p['w_in'][l], idx, axis=-1)
    qa = qa.reshape(B, T, SB_HEADS, SB_DIM)
    ka = ka.reshape(B, T, SB_HEADS, SB_DIM)
    va = va.reshape(B, T, SB_HEADS, SB_DIM)
    if sb_past is None:
        oa = sb_prompt(qa, ka, va)
    else:
        K = jnp.concatenate([sb_past[0].astype(ka.dtype), ka], axis=1)
        V = jnp.concatenate([sb_past[1].astype(va.dtype), va], axis=1)
        oa = stick_breaking(qa, K, V, pos0 + jnp.arange(T), jnp.arange(K.shape[1]))
    oa = oa.reshape(B, T, SB_WIDTH)
    pooled, pool_new = pool_mix(u, pool_hist, pos0)
    ob = jnp.einsum('btgc,gcd->btgd', pooled.reshape(B, T, POOL_GROUPS, POOL_GDIM), p['w_pool'][l])
    ob = ob.reshape(B, T, POOL_WIDTH) * p['pool_scale'][l]
    log_a = jax.nn.log_sigmoid((lr @ p['w_gla_a2'][l] + p['b_gla_a'][l]).astype(jnp.float32)) / GLA_TAU
    S_new, oc = gla((qc * (GLA_DK ** -0.5)).reshape(B, T, GLA_HEADS, GLA_DK),
                    kc.reshape(B, T, GLA_HEADS, GLA_DK),
                    vc.reshape(B, T, GLA_HEADS, GLA_DV),
                    log_a.reshape(B, T, GLA_HEADS, GLA_DK), S0, gla_chunk)
    oc = rmsnorm(oc, p['gla_norm'][l].reshape(GLA_HEADS, GLA_DV)).reshape(B, T, GLA_VW)
    oc = (oc * jax.nn.silu(gc.astype(jnp.float32))).astype(h.dtype)
    merged = (jax.nn.sigmoid(ga) * (oa @ p['w_branch_a'][l])
              + jax.nn.sigmoid(gb) * (ob @ p['w_branch_b'][l])
              + jax.nn.sigmoid(gcg) * (oc @ p['w_branch_c'][l]))
    out = (merged @ p['w_mix_out'][l]).astype(h.dtype)
    return out, ka, va, pool_new, S_new


def mem_kv(mem, g, wk, wv):
    B, M, _ = mem.shape
    m = rmsnorm(mem, g)
    return (m @ wk).reshape(B, M, X_HEADS, X_DIM), (m @ wv).reshape(B, M, X_HEADS, X_DIM)


def cross_attn(h, mk, mv, wq, wo):
    B, T, _ = h.shape
    q = (h @ wq).reshape(B, T, X_HEADS, X_DIM)
    s = jnp.einsum('bthd,bmhd->bhtm', q, mk.astype(q.dtype)).astype(jnp.float32) * (X_DIM ** -0.5)
    pr = jax.nn.softmax(s, axis=-1).astype(h.dtype)
    o = jnp.einsum('bhtm,bmhd->bthd', pr, mv.astype(h.dtype)).reshape(B, T, D_MODEL)
    return o @ wo


def swiglu(h, w_in, w_out):
    g, u = jnp.split(h @ w_in, 2, axis=-1)
    return (jax.nn.silu(g) * u) @ w_out


def layer(x, p, l, sb_past, pool_hist, S0, mk, mv, pos0, gla_chunk):
    h = rmsnorm(x, p['norm_mix_pre'][l])
    mix, k_new, v_new, pool_new, S_new = token_mixers(h, p, l, sb_past, pool_hist, S0, pos0, gla_chunk)
    x = x + rmsnorm(mix, p['norm_mix_post'][l]).astype(x.dtype)
    h = rmsnorm(x, p['norm_x_pre'][l])
    x = x + rmsnorm(cross_attn(h, mk, mv, p['w_xq'][l], p['w_xo'][l]), p['norm_x_post'][l]).astype(x.dtype)
    h = rmsnorm(x, p['norm_ffn_pre'][l])
    x = x + rmsnorm(swiglu(h, p['w_ffn_in'][l], p['w_ffn_out'][l]), p['norm_ffn_post'][l]).astype(x.dtype)
    return x, k_new, v_new, pool_new, S_new


def setup_inputs(seed: int = 0) -> dict:
    key = jax.random.key(seed)
    ks = jax.random.split(key, 40)
    f32 = jnp.float32

    def nrm(k, shape, fan_in):
        return jax.random.normal(k, shape, f32) * (fan_in ** -0.5)

    def gain(k, shape):
        return 1.0 + 0.05 * jax.random.normal(k, shape, f32)

    L = DEPTH
    return {
        'x_prompt': jax.random.normal(ks[0], (BATCH, SEQ, D_MODEL), f32),
        'x_sample': jax.random.normal(ks[1], (DEC_BATCH, DEC_SEQ, D_MODEL), f32),
        'mem_prompt': jax.random.normal(ks[2], (BATCH, MEM_LEN, D_MODEL), f32),
        'cache_sb_k': jax.random.normal(ks[3], (L, DEC_BATCH, PAST_LEN, SB_HEADS, SB_DIM), f32),
        'cache_sb_v': jax.random.normal(ks[4], (L, DEC_BATCH, PAST_LEN, SB_HEADS, SB_DIM), f32),
        'state_pool': jax.random.normal(ks[5], (L, DEC_BATCH, POOL_HIST, POOL_WIDTH), f32),
        'state_gla': jax.random.normal(ks[6], (L, DEC_BATCH, GLA_HEADS, GLA_DK, GLA_DV), f32),
        'cache_mem_k': jax.random.normal(ks[7], (L, DEC_BATCH, MEM_LEN, X_HEADS, X_DIM), f32),
        'cache_mem_v': jax.random.normal(ks[8], (L, DEC_BATCH, MEM_LEN, X_HEADS, X_DIM), f32),
        'w_in': nrm(ks[9], (L, D_MODEL, IN_WIDTH), D_MODEL),
        'w_gla_a2': nrm(ks[10], (L, GLA_RANK, GLA_KW), GLA_RANK),
        'b_gla_a': 0.1 * jax.random.normal(ks[11], (L, GLA_KW), f32),
        'gla_norm': gain(ks[12], (L, GLA_VW)),
        'w_pool': nrm(ks[13], (L, POOL_GROUPS, POOL_GDIM, POOL_GDIM), POOL_GDIM),
        'pool_scale': 1.0 + 0.1 * jax.random.normal(ks[14], (L, POOL_WIDTH), f32),
        'w_branch_a': nrm(ks[15], (L, SB_WIDTH, D_MODEL), SB_WIDTH),
        'w_branch_b': nrm(ks[16], (L, POOL_WIDTH, D_MODEL), POOL_WIDTH),
        'w_branch_c': nrm(ks[17], (L, GLA_VW, D_MODEL), GLA_VW),
        'w_mix_out': nrm(ks[18], (L, D_MODEL, D_MODEL), D_MODEL),
        'mem_norm': gain(ks[19], (L, D_MODEL)),
        'w_xq': nrm(ks[20], (L, D_MODEL, D_MODEL), D_MODEL),
        'w_xk': nrm(ks[21], (L, D_MODEL, D_MODEL), D_MODEL),
        'w_xv': nrm(ks[22], (L, D_MODEL, D_MODEL), D_MODEL),
        'w_xo': nrm(ks[23], (L, D_MODEL, D_MODEL), D_MODEL),
        'w_ffn_in': nrm(ks[24], (L, D_MODEL, 2 * FF), D_MODEL),
        'w_ffn_out': nrm(ks[25], (L, FF, D_MODEL), FF),
        'norm_mix_pre': gain(ks[26], (L, D_MODEL)),
        'norm_mix_post': gain(ks[27], (L, D_MODEL)),
        'norm_x_pre': gain(ks[28], (L, D_MODEL)),
        'norm_x_post': gain(ks[29], (L, D_MODEL)),
        'norm_ffn_pre': gain(ks[30], (L, D_MODEL)),
        'norm_ffn_post': gain(ks[31], (L, D_MODEL)),
    }


def reference(x_prompt, x_sample, mem_prompt, cache_sb_k, cache_sb_v, state_pool, state_gla,
              cache_mem_k, cache_mem_v, w_in, w_gla_a2, b_gla_a, gla_norm, w_pool, pool_scale,
              w_branch_a, w_branch_b, w_branch_c, w_mix_out, mem_norm, w_xq, w_xk, w_xv, w_xo,
              w_ffn_in, w_ffn_out, norm_mix_pre, norm_mix_post, norm_x_pre, norm_x_post,
              norm_ffn_pre, norm_ffn_post):
    p = dict(w_in=w_in, w_gla_a2=w_gla_a2, b_gla_a=b_gla_a, gla_norm=gla_norm, w_pool=w_pool,
             pool_scale=pool_scale, w_branch_a=w_branch_a, w_branch_b=w_branch_b,
             w_branch_c=w_branch_c, w_mix_out=w_mix_out, w_xq=w_xq, w_xo=w_xo,
             w_ffn_in=w_ffn_in, w_ffn_out=w_ffn_out, norm_mix_pre=norm_mix_pre,
             norm_mix_post=norm_mix_post, norm_x_pre=norm_x_pre, norm_x_post=norm_x_post,
             norm_ffn_pre=norm_ffn_pre, norm_ffn_post=norm_ffn_post)
    B = x_prompt.shape[0]
    Ts = x_sample.shape[1]
    yp, ys = x_prompt, x_sample
    kp, vp, poolp, glap, mkp, mvp = [], [], [], [], [], []
    kss, vss, pools, glas = [], [], [], []
    pool_zero = jnp.zeros((B, POOL_HIST, POOL_WIDTH), x_prompt.dtype)
    gla_zero = jnp.zeros((B, GLA_HEADS, GLA_DK, GLA_DV), jnp.float32)
    for l in range(DEPTH):
        mk, mv = mem_kv(mem_prompt, mem_norm[l], w_xk[l], w_xv[l])
        yp, k_new, v_new, pool_new, S_new = layer(yp, p, l, None, pool_zero, gla_zero, mk, mv, 0, CHUNK)
        kp.append(k_new); vp.append(v_new); poolp.append(pool_new); glap.append(S_new)
        mkp.append(mk); mvp.append(mv)
        ys, k_new, v_new, pool_new, S_new = layer(ys, p, l, (cache_sb_k[l], cache_sb_v[l]), state_pool[l],
                                                 state_gla[l], cache_mem_k[l], cache_mem_v[l], PAST_LEN, Ts)
        kss.append(k_new); vss.append(v_new); pools.append(pool_new); glas.append(S_new)
    return (yp, ys,
            jnp.stack(kp), jnp.stack(vp), jnp.stack(poolp), jnp.stack(glap), jnp.stack(mkp), jnp.stack(mvp),
            jnp.stack(kss), jnp.stack(vss), jnp.stack(pools), jnp.stack(glas))
```

```python
import functools

import numpy as np
import jax
import jax.numpy as jnp
from jax import lax
from jax.experimental import pallas as pl
from jax.experimental.pallas import tpu as pltpu

BF = jnp.bfloat16
F32 = jnp.float32
EPS = 1e-6

LANE = 128
VMEM_LIMIT = 56 * 1024 * 1024

SB_HEADS = 6
SB_DIM = 64
SB_W = SB_HEADS * SB_DIM
POOL_WINDOWS = (2, 4, 8, 16)
POOL_GDIM = 64
POOL_W = len(POOL_WINDOWS) * POOL_GDIM
POOL_HIST = max(POOL_WINDOWS) - 1
HIST_PAD = 16
GLA_HEADS = 4
GLA_D = 96
GLA_DP = LANE
GLA_W = GLA_HEADS * GLA_D
GLA_WP = GLA_HEADS * GLA_DP
GLA_RANK = 16
GLA_TAU = 16.0
X_HEADS = 4

C_Q = 0
C_K = C_Q + SB_W
C_V = C_K + SB_W
C_U = C_V + SB_W
C_GLA = C_U + POOL_W
C_LR = C_GLA + 4 * GLA_WP
C_GATE = C_LR + LANE


def _cparams(sem):
    return pltpu.CompilerParams(dimension_semantics=sem, vmem_limit_bytes=VMEM_LIMIT)


def _rms(x, g):
    return x * lax.rsqrt(jnp.mean(x * x, axis=-1, keepdims=True) + EPS) * g


def _softplus_neg_abs(z):
    return jnp.log(1.0 + jnp.exp(-jnp.abs(z)))


def _sigmoid(z):
    return 1.0 / (1.0 + jnp.exp(-z))


def _dot(a, b):
    return jnp.dot(a, b, preferred_element_type=F32)


def _dot_nt(a, b):
    return lax.dot_general(a, b, (((1,), (1,)), ((), ())), preferred_element_type=F32)


def _dot_tn(a, b):
    return lax.dot_general(a, b, (((0,), (0,)), ((), ())), preferred_element_type=F32)


def _const_spec(shape):
    zeros = (0,) * len(shape)
    return pl.BlockSpec(shape, lambda *_: zeros)


def _proj_kernel(x_ref, g_ref, w_ref, q_ref, k_ref, v_ref, kb_ref, vb_ref, u_ref,
                 gla_ref, lr_ref, gate_ref, *, d_model):
    h = _rms(x_ref[...], g_ref[...]).astype(BF)

    def mm(lo, width):
        return _dot(h, w_ref[:, lo:lo + width])

    q_ref[...] = (mm(C_Q, SB_W) * (SB_DIM ** -0.5)).astype(BF)
    k = mm(C_K, SB_W)
    k_ref[...] = k
    kb_ref[...] = k.astype(BF)
    v = mm(C_V, SB_W)
    v_ref[...] = v
    vb_ref[...] = v.astype(BF)
    u_ref[...] = mm(C_U, POOL_W)
    for j in range(4):
        gla_ref[:, j * GLA_WP:(j + 1) * GLA_WP] = mm(C_GLA + j * GLA_WP, GLA_WP).astype(BF)
    lr_ref[...] = mm(C_LR, LANE).astype(BF)
    for j in range(3):
        gate_ref[:, j * d_model:(j + 1) * d_model] = mm(C_GATE + j * d_model, d_model).astype(BF)


def _proj(x, g, w, tm):
    n, d = x.shape
    wtot = w.shape[1]
    row = lambda width: pl.BlockSpec((tm, width), lambda i: (i, 0))
    outs = [
        (SB_W, BF), (SB_W, F32), (SB_W, F32), (SB_W, BF), (SB_W, BF), (POOL_W, F32),
        (4 * GLA_WP, BF), (LANE, BF), (3 * d, BF),
    ]
    return pl.pallas_call(
        functools.partial(_proj_kernel, d_model=d),
        grid=(n // tm,),
        in_specs=[row(d), _const_spec((1, d)), _const_spec((d, wtot))],
        out_specs=[row(wd) for wd, _ in outs],
        out_shape=[jax.ShapeDtypeStruct((n, wd), dt) for wd, dt in outs],
        compiler_params=_cparams(("parallel",)),
        name="in_proj",
    )(x, g, w)


def _sb_block(qh, kblk, vblk, carry, acc, upper, mask):
    z = _dot_nt(qh, kblk)
    sp = _softplus_neg_abs(z)
    log_beta = jnp.minimum(z, 0.0) - sp
    log_fail = log_beta - z
    if mask is not None:
        log_fail = jnp.where(mask, log_fail, 0.0)
    after = _dot(log_fail.astype(BF), upper)
    w = jnp.exp(log_beta + after + carry)
    if mask is not None:
        w = jnp.where(mask, w, 0.0)
    acc = acc + _dot(w.astype(BF), vblk)
    carry = carry + jnp.sum(log_fail, axis=-1, keepdims=True)
    return carry, acc


def _upper_ones(n):
    r = lax.broadcasted_iota(jnp.int32, (n, n), 0)
    c = lax.broadcasted_iota(jnp.int32, (n, n), 1)
    return jnp.where(r > c, 1.0, 0.0).astype(BF)


def _strict_causal(n):
    r = lax.broadcasted_iota(jnp.int32, (n, n), 0)
    c = lax.broadcasted_iota(jnp.int32, (n, n), 1)
    return c < r


def _sb_prompt_kernel(q_ref, k_ref, v_ref, o_ref, *, blk):
    i = pl.program_id(1)
    upper = _upper_ones(blk)
    causal = _strict_causal(blk)
    for h in range(SB_HEADS):
        hs = slice(h * SB_DIM, (h + 1) * SB_DIM)
        qh = q_ref[:, hs]

        def visit(kb, state, mask, hs=hs, qh=qh):
            start = pl.multiple_of(kb * blk, blk)
            return _sb_block(qh, k_ref[pl.ds(start, blk), hs], v_ref[pl.ds(start, blk), hs],
                             state[0], state[1], upper, mask)

        state = (jnp.zeros((blk, 1), F32), jnp.zeros((blk, SB_DIM), F32))
        state = visit(i, state, causal)
        state = lax.fori_loop(0, i, lambda j, s, visit=visit: visit(i - 1 - j, s, None), state)
        o_ref[:, hs] = state[1].astype(BF)


def _sb_prompt(q, kb, vb, batch, seq, blk):
    nblk = seq // blk
    return pl.pallas_call(
        functools.partial(_sb_prompt_kernel, blk=blk),
        grid=(batch, nblk),
        in_specs=[
            pl.BlockSpec((blk, SB_W), lambda b, i: (b * nblk + i, 0)),
            pl.BlockSpec((seq, SB_W), lambda b, i: (b, 0)),
            pl.BlockSpec((seq, SB_W), lambda b, i: (b, 0)),
        ],
        out_specs=pl.BlockSpec((blk, SB_W), lambda b, i: (b * nblk + i, 0)),
        out_shape=jax.ShapeDtypeStruct((batch * seq, SB_W), BF),
        compiler_params=_cparams(("parallel", "arbitrary")),
        name="sb_prompt",
    )(q, kb, vb)


def _sb_sample_kernel(q_ref, kn_ref, vn_ref, kc_ref, vc_ref, o_ref, *, seq, past, blk):
    upper_new = _upper_ones(seq)
    causal = _strict_causal(seq)
    upper_past = _upper_ones(blk)
    for h in range(SB_HEADS):
        hs = slice(h * SB_DIM, (h + 1) * SB_DIM)
        qh = q_ref[:, hs]
        state = (jnp.zeros((seq, 1), F32), jnp.zeros((seq, SB_DIM), F32))
        state = _sb_block(qh, kn_ref[:, hs], vn_ref[:, hs], state[0], state[1], upper_new, causal)
        for j in reversed(range(past // blk)):
            rows = slice(j * blk, (j + 1) * blk)
            state = _sb_block(qh, kc_ref[0, 0, rows, hs].astype(BF), vc_ref[0, 0, rows, hs].astype(BF),
                              state[0], state[1], upper_past, None)
        o_ref[:, hs] = state[1].astype(BF)


def _sb_sample(q, kb, vb, cache_k, cache_v, layer, batch, seq, blk):
    past = cache_k.shape[2]
    new = pl.BlockSpec((seq, SB_W), lambda b: (b, 0))
    cache = pl.BlockSpec((1, 1, past, SB_W), lambda b: (layer, b, 0, 0))
    return pl.pallas_call(
        functools.partial(_sb_sample_kernel, seq=seq, past=past, blk=blk),
        grid=(batch,),
        in_specs=[new, new, new, cache, cache],
        out_specs=new,
        out_shape=jax.ShapeDtypeStruct((batch * seq, SB_W), BF),
        compiler_params=_cparams(("parallel",)),
        name="sb_sample",
    )(q, kb, vb, cache_k, cache_v)


def _pool_kernel(u_ref, hist_ref, w_ref, scale_ref, o_ref, buf_ref, *, tm, pos0):
    i = pl.program_id(1)

    @pl.when(i == 0)
    def _():
        buf_ref[0:HIST_PAD, :] = hist_ref[0]

    @pl.when(i > 0)
    def _():
        buf_ref[0:HIST_PAD, :] = buf_ref[tm:tm + HIST_PAD, :]

    u = u_ref[...]
    buf_ref[HIST_PAD:HIST_PAD + tm, :] = u

    lane = lax.broadcasted_iota(jnp.int32, (tm, POOL_W), 1)
    group = lane // POOL_GDIM
    pos = pos0 + i * tm + lax.broadcasted_iota(jnp.int32, (tm, POOL_W), 0)
    run = u
    win_sum = jnp.zeros((tm, POOL_W), F32)
    width = jnp.zeros((tm, POOL_W), jnp.int32)
    for d in range(1, max(POOL_WINDOWS)):
        run = run + buf_ref[HIST_PAD - d:HIST_PAD - d + tm, :]
        if d + 1 in POOL_WINDOWS:
            g = POOL_WINDOWS.index(d + 1)
            win_sum = jnp.where(group == g, run, win_sum)
            width = jnp.where(group == g, d + 1, width)
    cnt = jnp.minimum(pos + 1, width).astype(F32)
    pooled = win_sum / cnt - u
    o_ref[...] = (_dot(pooled.astype(BF), w_ref[...]) * scale_ref[...]).astype(BF)


def _pool(u, hist, w_bd, scale, batch, seq, tm, pos0):
    nt = seq // tm
    return pl.pallas_call(
        functools.partial(_pool_kernel, tm=tm, pos0=pos0),
        grid=(batch, nt),
        in_specs=[
            pl.BlockSpec((tm, POOL_W), lambda b, i: (b * nt + i, 0)),
            pl.BlockSpec((1, HIST_PAD, POOL_W), lambda b, i: (b, 0, 0)),
            _const_spec((POOL_W, POOL_W)),
            _const_spec((1, POOL_W)),
        ],
        out_specs=pl.BlockSpec((tm, POOL_W), lambda b, i: (b * nt + i, 0)),
        out_shape=jax.ShapeDtypeStruct((batch * seq, POOL_W), BF),
        scratch_shapes=[pltpu.VMEM((tm + HIST_PAD, POOL_W), F32)],
        compiler_params=_cparams(("parallel", "arbitrary")),
        name="pool_mix",
    )(u, hist, w_bd, scale)


def _gla_levels(rows):
    out, s = [], 1
    while s < rows:
        out.append(s)
        s *= 2
    return out


def _gla_exponent_matrix(rows):
    t = np.arange(rows)[:, None]
    j = np.arange(rows)[None, :]
    blocks = [j <= t, j > t]
    for s in _gla_levels(rows):
        start = (t // s) * s
        blocks.append((j >= start) & (j <= t))
        if s > 1:
            blocks.append((j > t) & (j <= start + s - 1))
    return np.concatenate(blocks, axis=0).astype(np.float32)


def _gla_kernel(qkvg_ref, lr_ref, wa_ref, ba_ref, em_ref, gn_ref, s0_ref, o_ref, sout_ref, st_ref,
                *, rows):
    i = pl.program_id(1)
    last = pl.num_programs(1) - 1

    @pl.when(i == 0)
    def _():
        st_ref[...] = s0_ref[0]

    x = _dot(lr_ref[...], wa_ref[...]) + ba_ref[...]
    log_a = (jnp.minimum(x, 0.0) - _softplus_neg_abs(x)) * (1.0 / GLA_TAU)
    expo = _dot(em_ref[...], log_a.astype(BF))

    def block(n):
        return expo[n * rows:(n + 1) * rows]

    q = qkvg_ref[:, 0:GLA_WP].astype(F32) * (GLA_D ** -0.5)
    k = qkvg_ref[:, GLA_WP:2 * GLA_WP].astype(F32)
    vb = qkvg_ref[:, 2 * GLA_WP:3 * GLA_WP]
    gate = qkvg_ref[:, 3 * GLA_WP:4 * GLA_WP].astype(F32)

    b = block(0)
    q_dec = (q * jnp.exp(b)).astype(BF)
    k_dec = (k * jnp.exp(block(1))).astype(BF)
    b_last = b[rows - 1:rows]

    r = lax.broadcasted_iota(jnp.int32, (rows, rows), 0)
    c = lax.broadcasted_iota(jnp.int32, (rows, rows), 1)
    heads = [slice(h * GLA_DP, (h + 1) * GLA_DP) for h in range(GLA_HEADS)]

    qb = q.astype(BF)
    kb = k.astype(BF)
    att = [jnp.where(r == c, _dot_nt(qb[:, hs], kb[:, hs]), 0.0) for hs in heads]
    n = 2
    for s in _gla_levels(rows):
        shift = s.bit_length() - 1
        rb = r >> shift
        pair = ((rb & 1) == 1) & ((c >> shift) == rb - 1)
        q_lvl = (q * jnp.exp(block(n))).astype(BF)
        n += 1
        if s > 1:
            k_lvl = (k * jnp.exp(block(n))).astype(BF)
            n += 1
        else:
            k_lvl = kb
        for h, hs in enumerate(heads):
            att[h] = att[h] + jnp.where(pair, _dot_nt(q_lvl[:, hs], k_lvl[:, hs]), 0.0)

    for h, hs in enumerate(heads):
        st = st_ref[h]
        o = _dot(att[h].astype(BF), vb[:, hs]) + _dot_nt(q_dec[:, hs], st.astype(BF))
        ms = jnp.sum(o * o, axis=-1, keepdims=True) * (1.0 / GLA_D)
        y = o * lax.rsqrt(ms + EPS) * gn_ref[:, hs]
        gh = gate[:, hs]
        o_ref[:, hs] = (y * (gh * _sigmoid(gh))).astype(BF)
        st_new = st * jnp.exp(b_last[:, hs]) + _dot_tn(vb[:, hs], k_dec[:, hs])
        st_ref[h] = st_new

        @pl.when(i == last)
        def _(st_new=st_new, h=h):
            sout_ref[0, h] = st_new


def _gla(qkvg, lr, wa, ba, gn, s0t, batch, seq, rows):
    nt = seq // rows
    em = jnp.asarray(_gla_exponent_matrix(rows), BF)
    state = pl.BlockSpec((1, GLA_HEADS, GLA_DP, GLA_DP), lambda b, i: (b, 0, 0, 0))
    return pl.pallas_call(
        functools.partial(_gla_kernel, rows=rows),
        grid=(batch, nt),
        in_specs=[
            pl.BlockSpec((rows, 4 * GLA_WP), lambda b, i: (b * nt + i, 0)),
            pl.BlockSpec((rows, LANE), lambda b, i: (b * nt + i, 0)),
            _const_spec((LANE, GLA_WP)),
            _const_spec((1, GLA_WP)),
            _const_spec(em.shape),
            _const_spec((1, GLA_WP)),
            state,
        ],
        out_specs=[pl.BlockSpec((rows, GLA_WP), lambda b, i: (b * nt + i, 0)), state],
        out_shape=[jax.ShapeDtypeStruct((batch * seq, GLA_WP), BF),
                   jax.ShapeDtypeStruct((batch, GLA_HEADS, GLA_DP, GLA_DP), F32)],
        scratch_shapes=[pltpu.VMEM((GLA_HEADS, GLA_DP, GLA_DP), F32)],
        compiler_params=_cparams(("parallel", "arbitrary")),
        name="gla",
    )(qkvg, lr, wa, ba, em, gn, s0t)


def _merge_kernel(x_ref, oa_ref, ob_ref, oc_ref, gate_ref, wa_ref, wb_ref, wc_ref, wo_ref, g_ref,
                  y_ref, *, d_model):
    def gate(j):
        return _sigmoid(gate_ref[:, j * d_model:(j + 1) * d_model].astype(F32))

    merged = gate(0) * _dot(oa_ref[...], wa_ref[...])
    merged = merged + gate(1) * _dot(ob_ref[...], wb_ref[...])
    merged = merged + gate(2) * _dot(oc_ref[...], wc_ref[...])
    mix = _dot(merged.astype(BF), wo_ref[...])
    y_ref[...] = x_ref[...] + _rms(mix, g_ref[...])


def _merge(x, oa, ob, oc, gates, wa, wb, wc, wo, g, tm):
    n, d = x.shape
    row = lambda width: pl.BlockSpec((tm, width), lambda i: (i, 0))
    return pl.pallas_call(
        functools.partial(_merge_kernel, d_model=d),
        grid=(n // tm,),
        in_specs=[row(d), row(SB_W), row(POOL_W), row(GLA_WP), row(3 * d),
                  _const_spec(wa.shape), _const_spec(wb.shape), _const_spec(wc.shape),
                  _const_spec(wo.shape), _const_spec((1, d))],
        out_specs=row(d),
        out_shape=jax.ShapeDtypeStruct((n, d), F32),
        compiler_params=_cparams(("parallel",)),
        name="merge_out",
    )(x, oa, ob, oc, gates, wa, wb, wc, wo, g)


def _xattn_kernel(x_ref, mk_ref, mv_ref, wq_ref, wo_ref, gpre_ref, gpost_ref, y_ref, *, d_model):
    x = x_ref[...]
    q = _dot(_rms(x, gpre_ref[...]).astype(BF), wq_ref[...])
    hd = d_model // X_HEADS
    outs = []
    for h in range(X_HEADS):
        hs = slice(h * hd, (h + 1) * hd)
        s = _dot_nt(q[:, hs].astype(BF), mk_ref[0, :, hs].astype(BF)) * (hd ** -0.5)
        p = jnp.exp(s - jnp.max(s, axis=-1, keepdims=True))
        p = p / jnp.sum(p, axis=-1, keepdims=True)
        outs.append(_dot(p.astype(BF), mv_ref[0, :, hs].astype(BF)).astype(BF))
    o = jnp.concatenate(outs, axis=-1)
    y_ref[...] = x + _rms(_dot(o, wo_ref[...]), gpost_ref[...])


def _xattn(x, mk, mv, mem_index, wq, wo, gpre, gpost, batch, seq, tm):
    n, d = x.shape
    nt = seq // tm
    mem_len = mk.shape[-2]
    row = pl.BlockSpec((tm, d), lambda b, i: (b * nt + i, 0))
    mem = pl.BlockSpec((1, mem_len, d), lambda b, i: (mem_index(b), 0, 0))
    return pl.pallas_call(
        functools.partial(_xattn_kernel, d_model=d),
        grid=(batch, nt),
        in_specs=[row, mem, mem, _const_spec((d, d)), _const_spec((d, d)),
                  _const_spec((1, d)), _const_spec((1, d))],
        out_specs=row,
        out_shape=jax.ShapeDtypeStruct((n, d), F32),
        compiler_params=_cparams(("parallel", "parallel")),
        name="cross_attn",
    )(x, mk, mv, wq, wo, gpre, gpost)


def _ffn_kernel(x_ref, wg_ref, wu_ref, wo_ref, gpre_ref, gpost_ref, y_ref, *, chunk):
    x = x_ref[...]
    h = _rms(x, gpre_ref[...]).astype(BF)
    ff = wg_ref.shape[1]
    acc = jnp.zeros(x.shape, F32)
    for j in range(ff // chunk):
        cs = slice(j * chunk, (j + 1) * chunk)
        g = _dot(h, wg_ref[:, cs])
        u = _dot(h, wu_ref[:, cs])
        act = (g * _sigmoid(g) * u).astype(BF)
        acc = acc + _dot(act, wo_ref[cs, :])
    y_ref[...] = x + _rms(acc, gpost_ref[...])


def _ffn(x, wg, wu, wo, gpre, gpost, tm, chunk):
    n, d = x.shape
    row = pl.BlockSpec((tm, d), lambda i: (i, 0))
    return pl.pallas_call(
        functools.partial(_ffn_kernel, chunk=chunk),
        grid=(n // tm,),
        in_specs=[row, _const_spec(wg.shape), _const_spec(wu.shape), _const_spec(wo.shape),
                  _const_spec((1, d)), _const_spec((1, d))],
        out_specs=row,
        out_shape=jax.ShapeDtypeStruct((n, d), F32),
        compiler_params=_cparams(("parallel",)),
        name="ffn",
    )(x, wg, wu, wo, gpre, gpost)


def _memkv_kernel(m_ref, g_ref, wk_ref, wv_ref, k_ref, v_ref):
    m = _rms(m_ref[...], g_ref[0]).astype(BF)
    k_ref[0] = _dot(m, wk_ref[0].astype(BF))
    v_ref[0] = _dot(m, wv_ref[0].astype(BF))


def _memkv(mem, g, wk, wv, tm):
    depth, d, _ = wk.shape
    n = mem.shape[0]
    w = pl.BlockSpec((1, d, d), lambda l, i: (l, 0, 0))
    out = pl.BlockSpec((1, tm, d), lambda l, i: (l, i, 0))
    return pl.pallas_call(
        _memkv_kernel,
        grid=(depth, n // tm),
        in_specs=[pl.BlockSpec((tm, d), lambda l, i: (i, 0)),
                  pl.BlockSpec((1, 1, d), lambda l, i: (l, 0, 0)), w, w],
        out_specs=[out, out],
        out_shape=[jax.ShapeDtypeStruct((depth, n, d), F32)] * 2,
        compiler_params=_cparams(("parallel", "parallel")),
        name="mem_kv",
    )(mem, g, wk, wv)


def _pad_heads(w, axis):
    shape = w.shape
    w = w.reshape(shape[:axis] + (GLA_HEADS, GLA_D) + shape[axis + 1:])
    pad = [(0, 0)] * w.ndim
    pad[axis + 1] = (0, GLA_DP - GLA_D)
    w = jnp.pad(w, pad)
    return w.reshape(shape[:axis] + (GLA_WP,) + shape[axis + 1:])


def _pack_layer(p, l):
    d = p['w_in'].shape[1]
    w_in = p['w_in'][l]
    o = 3 * SB_W + POOL_W
    sb_u = w_in[:, :o]
    gla = [_pad_heads(w_in[:, o + j * GLA_W:o + (j + 1) * GLA_W], 1) for j in range(4)]
    o += 4 * GLA_W
    lr = jnp.pad(w_in[:, o:o + GLA_RANK], ((0, 0), (0, LANE - GLA_RANK)))
    gates = w_in[:, o + GLA_RANK:]
    eye = jnp.eye(len(POOL_WINDOWS), dtype=F32)
    w_pool_bd = (eye[:, None, :, None] * p['w_pool'][l][:, :, None, :]).reshape(POOL_W, POOL_W)
    ff = p['w_ffn_out'].shape[1]
    row = lambda v: v.reshape(1, -1)
    return dict(
        w_in=jnp.concatenate([sb_u] + gla + [lr, gates], axis=1).astype(BF),
        w_a2=jnp.pad(_pad_heads(p['w_gla_a2'][l], 1), ((0, LANE - GLA_RANK), (0, 0))).astype(BF),
        b_a=row(_pad_heads(p['b_gla_a'][l], 0)),
        gla_norm=row(_pad_heads(p['gla_norm'][l], 0)),
        w_pool=w_pool_bd.astype(BF),
        pool_scale=row(p['pool_scale'][l]),
        w_a=p['w_branch_a'][l].astype(BF),
        w_b=p['w_branch_b'][l].astype(BF),
        w_c=_pad_heads(p['w_branch_c'][l], 0).astype(BF),
        w_mix=p['w_mix_out'][l].astype(BF),
        w_xq=p['w_xq'][l].astype(BF),
        w_xo=p['w_xo'][l].astype(BF),
        w_g=p['w_ffn_in'][l][:, :ff].astype(BF),
        w_u=p['w_ffn_in'][l][:, ff:].astype(BF),
        w_o=p['w_ffn_out'][l].astype(BF),
        n_mix_pre=row(p['norm_mix_pre'][l]), n_mix_post=row(p['norm_mix_post'][l]),
        n_x_pre=row(p['norm_x_pre'][l]), n_x_post=row(p['norm_x_post'][l]),
        n_ffn_pre=row(p['norm_ffn_pre'][l]), n_ffn_post=row(p['norm_ffn_post'][l]),
    )


def _pad_state(s):
    s = jnp.swapaxes(s, -1, -2)
    return jnp.pad(s, ((0, 0), (0, 0), (0, GLA_DP - GLA_D), (0, GLA_DP - GLA_D)))


def _unpad_state(st):
    return jnp.swapaxes(st[:, :, :GLA_D, :GLA_D], -1, -2)


def _tiles(batch, seq):
    n = batch * seq
    return dict(
        proj=min(256, n),
        sb=min(256, seq),
        pool=min(256, seq),
        gla=min(128, seq),
        merge=min(512, n),
        xattn=min(512, seq),
        ffn=min(512, n),
    )


def _layer(x, w, batch, seq, sb_attend, pool_hist, pos0, s0t, mk, mv, mem_index):
    t = _tiles(batch, seq)
    q, k, v, kb, vb, u, qkvg, lr, gates = _proj(x, w['n_mix_pre'], w['w_in'], t['proj'])
    oa = sb_attend(q, kb, vb)
    ob = _pool(u, pool_hist, w['w_pool'], w['pool_scale'], batch, seq, t['pool'], pos0)
    oc, st = _gla(qkvg, lr, w['w_a2'], w['b_a'], w['gla_norm'], s0t, batch, seq, t['gla'])
    x = _merge(x, oa, ob, oc, gates, w['w_a'], w['w_b'], w['w_c'], w['w_mix'], w['n_mix_post'],
               t['merge'])
    x = _xattn(x, mk, mv, mem_index, w['w_xq'], w['w_xo'], w['n_x_pre'], w['n_x_post'],
               batch, seq, t['xattn'])
    x = _ffn(x, w['w_g'], w['w_u'], w['w_o'], w['n_ffn_pre'], w['n_ffn_post'], t['ffn'], 256)
    return x, k, v, u, st


def kernel(x_prompt, x_sample, mem_prompt, cache_sb_k, cache_sb_v, state_pool, state_gla, cache_mem_k, cache_mem_v, w_in, w_gla_a2, b_gla_a, gla_norm, w_pool, pool_scale, w_branch_a, w_branch_b, w_branch_c, w_mix_out, mem_norm, w_xq, w_xk, w_xv, w_xo, w_ffn_in, w_ffn_out, norm_mix_pre, norm_mix_post, norm_x_pre, norm_x_post, norm_ffn_pre, norm_ffn_post):
    p = dict(w_in=w_in, w_gla_a2=w_gla_a2, b_gla_a=b_gla_a, gla_norm=gla_norm, w_pool=w_pool,
             pool_scale=pool_scale, w_branch_a=w_branch_a, w_branch_b=w_branch_b,
             w_branch_c=w_branch_c, w_mix_out=w_mix_out, w_xq=w_xq, w_xo=w_xo,
             w_ffn_in=w_ffn_in, w_ffn_out=w_ffn_out, norm_mix_pre=norm_mix_pre,
             norm_mix_post=norm_mix_post, norm_x_pre=norm_x_pre, norm_x_post=norm_x_post,
             norm_ffn_pre=norm_ffn_pre, norm_ffn_post=norm_ffn_post)
    depth = w_in.shape[0]
    bp, tp, d = x_prompt.shape
    bs, ts, _ = x_sample.shape
    mem_len = mem_prompt.shape[1]
    past = cache_sb_k.shape[2]

    mk_all, mv_all = _memkv(mem_prompt.reshape(bp * mem_len, d), mem_norm.reshape(depth, 1, d),
                            w_xk, w_xv, min(512, bp * mem_len))
    mk_p = mk_all.reshape(depth * bp, mem_len, d)
    mv_p = mv_all.reshape(depth * bp, mem_len, d)
    mk_s = cache_mem_k.reshape(depth * bs, mem_len, d)
    mv_s = cache_mem_v.reshape(depth * bs, mem_len, d)
    ck = cache_sb_k.reshape(depth, bs, past, SB_W)
    cv = cache_sb_v.reshape(depth, bs, past, SB_W)

    yp = x_prompt.reshape(bp * tp, d)
    ys = x_sample.reshape(bs * ts, d)
    hist_zero = jnp.zeros((bp, HIST_PAD, POOL_W), F32)
    state_zero = jnp.zeros((bp, GLA_HEADS, GLA_DP, GLA_DP), F32)
    tiles_p = _tiles(bp, tp)
    kp, vp, poolp, glap, kss, vss, pools, glas = ([] for _ in range(8))
    for l in range(depth):
        w = _pack_layer(p, l)
        yp, k, v, u, st = _layer(
            yp, w, bp, tp,
            functools.partial(_sb_prompt, batch=bp, seq=tp, blk=tiles_p['sb']),
            hist_zero, 0, state_zero, mk_p, mv_p, lambda b, l=l: l * bp + b)
        kp.append(k.reshape(bp, tp, SB_HEADS, SB_DIM))
        vp.append(v.reshape(bp, tp, SB_HEADS, SB_DIM))
        poolp.append(u.reshape(bp, tp, POOL_W)[:, tp - POOL_HIST:])
        glap.append(_unpad_state(st))

        hist = jnp.pad(state_pool[l], ((0, 0), (HIST_PAD - POOL_HIST, 0), (0, 0)))
        ys, k, v, u, st = _layer(
            ys, w, bs, ts,
            functools.partial(_sb_sample, cache_k=ck, cache_v=cv, layer=l, batch=bs, seq=ts,
                              blk=min(256, past)),
            hist, past, _pad_state(state_gla[l]), mk_s, mv_s, lambda b, l=l: l * bs + b)
        kss.append(k.reshape(bs, ts, SB_HEADS, SB_DIM))
        vss.append(v.reshape(bs, ts, SB_HEADS, SB_DIM))
        full = jnp.concatenate([state_pool[l], u.reshape(bs, ts, POOL_W)], axis=1)
        pools.append(full[:, -POOL_HIST:])
        glas.append(_unpad_state(st))

    return (yp.reshape(bp, tp, d), ys.reshape(bs, ts, d),
            jnp.stack(kp), jnp.stack(vp), jnp.stack(poolp), jnp.stack(glap),
            mk_all.reshape(depth, bp, mem_len, X_HEADS, d // X_HEADS),
            mv_all.reshape(depth, bp, mem_len, X_HEADS, d // X_HEADS),
            jnp.stack(kss), jnp.stack(vss), jnp.stack(pools), jnp.stack(glas))
```

```python
import functools

import numpy as np
import jax
import jax.numpy as jnp
from jax import lax
from jax.experimental import pallas as pl
from jax.experimental.pallas import tpu as pltpu

BF = jnp.bfloat16
F32 = jnp.float32
EPS = 1e-6

LANE = 128
VMEM_LIMIT = 56 * 1024 * 1024

SB_HEADS = 6
SB_DIM = 64
SB_W = SB_HEADS * SB_DIM
POOL_WINDOWS = (2, 4, 8, 16)
POOL_GDIM = 64
POOL_W = len(POOL_WINDOWS) * POOL_GDIM
POOL_HIST = max(POOL_WINDOWS) - 1
HIST_PAD = 16
GLA_HEADS = 4
GLA_D = 96
GLA_DP = LANE
GLA_W = GLA_HEADS * GLA_D
GLA_WP = GLA_HEADS * GLA_DP
GLA_RANK = 16
GLA_TAU = 16.0
X_HEADS = 4

C_Q = 0
C_K = C_Q + SB_W
C_V = C_K + SB_W
C_U = C_V + SB_W
C_GLA = C_U + POOL_W
C_LR = C_GLA + 4 * GLA_WP
C_GATE = C_LR + LANE


def _cparams(sem):
    return pltpu.CompilerParams(dimension_semantics=sem, vmem_limit_bytes=VMEM_LIMIT)


def _rms(x, g):
    return x * lax.rsqrt(jnp.mean(x * x, axis=-1, keepdims=True) + EPS) * g


def _softplus_neg_abs(z):
    return jnp.log(1.0 + jnp.exp(-jnp.abs(z)))


def _sigmoid(z):
    return 1.0 / (1.0 + jnp.exp(-z))


def _dot(a, b):
    return jnp.dot(a, b, preferred_element_type=F32)


def _dot_nt(a, b):
    return lax.dot_general(a, b, (((1,), (1,)), ((), ())), preferred_element_type=F32)


def _dot_tn(a, b):
    return lax.dot_general(a, b, (((0,), (0,)), ((), ())), preferred_element_type=F32)


def _const_spec(shape):
    zeros = (0,) * len(shape)
    return pl.BlockSpec(shape, lambda *_: zeros)


def _proj_kernel(*refs, d_model, n_alias, transposed_kv):
    x_ref, g_ref, w_ref = refs[:3]
    q_ref, k_ref, v_ref, kb_ref, vb_ref, u_ref, gla_ref, lr_ref, gate_ref = refs[3 + n_alias:]
    h = _rms(x_ref[...], g_ref[...]).astype(BF)

    def mm(lo, width):
        return _dot(h, w_ref[:, lo:lo + width])

    q_ref[...] = (mm(C_Q, SB_W) * (SB_DIM ** -0.5)).astype(BF)
    for lo, f32_ref, bf_ref in ((C_K, k_ref, kb_ref), (C_V, v_ref, vb_ref)):
        kv = mm(lo, SB_W)
        bf_ref[...] = kv.astype(BF)
        if transposed_kv:
            f32_ref[0, 0] = kv.T
        else:
            f32_ref[...] = kv
    u_ref[...] = mm(C_U, POOL_W)
    for j in range(4):
        gla_ref[:, j * GLA_WP:(j + 1) * GLA_WP] = mm(C_GLA + j * GLA_WP, GLA_WP).astype(BF)
    lr_ref[...] = mm(C_LR, LANE).astype(BF)
    for j in range(3):
        gate_ref[:, j * d_model:(j + 1) * d_model] = mm(C_GATE + j * d_model, d_model).astype(BF)


def _proj(x, g, w, tm, kv_stack=None):
    n, d = x.shape
    wtot = w.shape[1]
    row = lambda width: pl.BlockSpec((tm, width), lambda i: (i, 0))
    if kv_stack is None:
        kv_spec, kv_shape, extra, aliases = row(SB_W), (n, SB_W), [], {}
    else:
        layer, depth, batch, seq, bufs = kv_stack
        nt = seq // tm
        kv_spec = pl.BlockSpec((1, 1, SB_W, tm), lambda i: (layer, i // nt, 0, i % nt))
        kv_shape = (depth, batch, SB_W, seq)
        extra = [] if bufs is None else list(bufs)
        aliases = {3 + j: 1 + j for j in range(len(extra))}
    outs = [
        (row(SB_W), (n, SB_W), BF), (kv_spec, kv_shape, F32), (kv_spec, kv_shape, F32),
        (row(SB_W), (n, SB_W), BF), (row(SB_W), (n, SB_W), BF), (row(POOL_W), (n, POOL_W), F32),
        (row(4 * GLA_WP), (n, 4 * GLA_WP), BF), (row(LANE), (n, LANE), BF),
        (row(3 * d), (n, 3 * d), BF),
    ]
    return pl.pallas_call(
        functools.partial(_proj_kernel, d_model=d, n_alias=len(extra),
                          transposed_kv=kv_stack is not None),
        grid=(n // tm,),
        in_specs=[row(d), _const_spec((1, d)), _const_spec((d, wtot))]
                 + [pl.BlockSpec(memory_space=pl.ANY)] * len(extra),
        out_specs=[spec for spec, _, _ in outs],
        out_shape=[jax.ShapeDtypeStruct(shape, dt) for _, shape, dt in outs],
        input_output_aliases=aliases,
        compiler_params=_cparams(("parallel",)),
        name="in_proj",
    )(x, g, w, *extra)


def _upper_ones(n):
    r = lax.broadcasted_iota(jnp.int32, (n, n), 0)
    c = lax.broadcasted_iota(jnp.int32, (n, n), 1)
    return jnp.where(r >= c, 1.0, 0.0).astype(BF)


def _strict_causal(n):
    r = lax.broadcasted_iota(jnp.int32, (n, n), 0)
    c = lax.broadcasted_iota(jnp.int32, (n, n), 1)
    return c < r


def _sb_weights(z, carry, upper, mask):
    t = jnp.maximum(z, 0.0) + _softplus_neg_abs(z)
    if mask is not None:
        t = jnp.where(mask, t, 0.0)
    w = jnp.exp(z - _dot(t.astype(BF), upper) - carry)
    if mask is not None:
        w = jnp.where(mask, w, 0.0)
    return w.astype(BF), carry + jnp.sum(t, axis=-1, keepdims=True)


def _sb_prompt_kernel(q_ref, k_ref, v_ref, o_ref, acc_ref, *, blk):
    i = pl.program_id(1)
    pair_w = 2 * SB_DIM
    n_pairs = SB_HEADS // 2
    upper = _upper_ones(blk)
    causal = _strict_causal(blk)
    low = lax.broadcasted_iota(jnp.int32, (blk, pair_w), 1) < SB_DIM
    zero = jnp.zeros((blk, pair_w), BF)

    def split(x):
        return jnp.where(low, x, zero), jnp.where(low, zero, x)

    q_heads = []
    for p in range(n_pairs):
        q_heads.extend(split(q_ref[:, p * pair_w:(p + 1) * pair_w]))
    acc_ref[...] = jnp.zeros(acc_ref.shape, F32)

    def visit(kb, carries, mask):
        start = pl.multiple_of(kb * blk, blk)
        out = []
        for p in range(n_pairs):
            ps = slice(p * pair_w, (p + 1) * pair_w)
            kp = k_ref[pl.ds(start, blk), ps]
            v_lo, v_hi = split(v_ref[pl.ds(start, blk), ps])
            ws = []
            for e in range(2):
                h = 2 * p + e
                w, carry = _sb_weights(_dot_nt(q_heads[h], kp), carries[h], upper, mask)
                ws.append(w)
                out.append(carry)
            acc_ref[:, ps] += _dot(jnp.concatenate(ws, axis=1), jnp.concatenate([v_lo, v_hi], axis=0))
        return tuple(out)

    carries = tuple(jnp.zeros((blk, 1), F32) for _ in range(SB_HEADS))
    carries = visit(i, carries, causal)
    lax.fori_loop(0, i, lambda j, c: visit(i - 1 - j, c, None), carries)
    o_ref[...] = acc_ref[...].astype(BF)


def _sb_prompt(q, kb, vb, batch, seq, blk):
    nblk = seq // blk
    return pl.pallas_call(
        functools.partial(_sb_prompt_kernel, blk=blk),
        grid=(batch, nblk),
        in_specs=[
            pl.BlockSpec((blk, SB_W), lambda b, i: (b * nblk + i, 0)),
            pl.BlockSpec((seq, SB_W), lambda b, i: (b, 0)),
            pl.BlockSpec((seq, SB_W), lambda b, i: (b, 0)),
        ],
        out_specs=pl.BlockSpec((blk, SB_W), lambda b, i: (b * nblk + i, 0)),
        out_shape=jax.ShapeDtypeStruct((batch * seq, SB_W), BF),
        scratch_shapes=[pltpu.VMEM((blk, SB_W), F32)],
        compiler_params=_cparams(("parallel", "arbitrary")),
        name="sb_prompt",
    )(q, kb, vb)


def _sb_sample_kernel(q_ref, kn_ref, vn_ref, kc_ref, vc_ref, o_ref, *, seq, past, blk):
    pair_w = 2 * SB_DIM
    n_pairs = SB_HEADS // 2
    pairs = [slice(p * pair_w, (p + 1) * pair_w) for p in range(n_pairs)]

    def split(x, axis):
        low = lax.broadcasted_iota(jnp.int32, x.shape, axis) < SB_DIM
        zero = jnp.zeros(x.shape, x.dtype)
        return jnp.where(low, x, zero), jnp.where(low, zero, x)

    q_heads = []
    for ps in pairs:
        q_heads.extend(split(q_ref[:, ps], 1))
    r = lax.broadcasted_iota(jnp.int32, (SB_HEADS * seq, seq), 0) % seq
    causal = lax.broadcasted_iota(jnp.int32, (SB_HEADS * seq, seq), 1) < r

    def visit(k_pairs, v_pairs, transposed, carry, acc, upper, mask):
        qk = _dot if transposed else _dot_nt
        pv = _dot_nt if transposed else _dot
        z = jnp.concatenate([qk(q_heads[h], k_pairs[h // 2]) for h in range(SB_HEADS)], axis=0)
        w, carry = _sb_weights(z, carry, upper, mask)
        out = []
        for p in range(n_pairs):
            v_lo, v_hi = split(v_pairs[p], 0 if transposed else 1)
            w_lo = w[(2 * p) * seq:(2 * p + 1) * seq]
            w_hi = w[(2 * p + 1) * seq:(2 * p + 2) * seq]
            out.append(acc[p] + pv(w_lo, v_lo) + pv(w_hi, v_hi))
        return carry, out

    carry = jnp.zeros((SB_HEADS * seq, 1), F32)
    acc = [jnp.zeros((seq, pair_w), F32) for _ in pairs]
    carry, acc = visit([kn_ref[:, ps] for ps in pairs], [vn_ref[:, ps] for ps in pairs], False,
                       carry, acc, _upper_ones(seq), causal)
    upper_past = _upper_ones(blk)
    for j in reversed(range(past // blk)):
        cols = slice(j * blk, (j + 1) * blk)
        carry, acc = visit([kc_ref[0, 0, ps, cols].astype(BF) for ps in pairs],
                           [vc_ref[0, 0, ps, cols].astype(BF) for ps in pairs], True,
                           carry, acc, upper_past, None)
    for p, ps in enumerate(pairs):
        o_ref[:, ps] = acc[p].astype(BF)


def _sb_sample(q, kb, vb, cache_k, cache_v, layer, batch, seq, blk):
    past = cache_k.shape[3]
    new = pl.BlockSpec((seq, SB_W), lambda b: (b, 0))
    cache = pl.BlockSpec((1, 1, SB_W, past), lambda b: (layer, b, 0, 0))
    return pl.pallas_call(
        functools.partial(_sb_sample_kernel, seq=seq, past=past, blk=blk),
        grid=(batch,),
        in_specs=[new, new, new, cache, cache],
        out_specs=new,
        out_shape=jax.ShapeDtypeStruct((batch * seq, SB_W), BF),
        compiler_params=_cparams(("parallel",)),
        name="sb_sample",
    )(q, kb, vb, cache_k, cache_v)


def _pool_kernel(u_ref, hist_ref, w_ref, scale_ref, o_ref, buf_ref, *, tm, pos0):
    i = pl.program_id(1)

    @pl.when(i == 0)
    def _():
        buf_ref[0:HIST_PAD, :] = hist_ref[0]

    @pl.when(i > 0)
    def _():
        buf_ref[0:HIST_PAD, :] = buf_ref[tm:tm + HIST_PAD, :]

    u = u_ref[...]
    buf_ref[HIST_PAD:HIST_PAD + tm, :] = u

    lane = lax.broadcasted_iota(jnp.int32, (tm, POOL_W), 1)
    group = lane // POOL_GDIM
    pos = pos0 + i * tm + lax.broadcasted_iota(jnp.int32, (tm, POOL_W), 0)
    run = u
    win_sum = jnp.zeros((tm, POOL_W), F32)
    width = jnp.zeros((tm, POOL_W), jnp.int32)
    for d in range(1, max(POOL_WINDOWS)):
        run = run + buf_ref[HIST_PAD - d:HIST_PAD - d + tm, :]
        if d + 1 in POOL_WINDOWS:
            g = POOL_WINDOWS.index(d + 1)
            win_sum = jnp.where(group == g, run, win_sum)
            width = jnp.where(group == g, d + 1, width)
    cnt = jnp.minimum(pos + 1, width).astype(F32)
    pooled = win_sum / cnt - u
    o_ref[...] = (_dot(pooled.astype(BF), w_ref[...]) * scale_ref[...]).astype(BF)


def _pool(u, hist, w_bd, scale, batch, seq, tm, pos0):
    nt = seq // tm
    return pl.pallas_call(
        functools.partial(_pool_kernel, tm=tm, pos0=pos0),
        grid=(batch, nt),
        in_specs=[
            pl.BlockSpec((tm, POOL_W), lambda b, i: (b * nt + i, 0)),
            pl.BlockSpec((1, HIST_PAD, POOL_W), lambda b, i: (b, 0, 0)),
            _const_spec((POOL_W, POOL_W)),
            _const_spec((1, POOL_W)),
        ],
        out_specs=pl.BlockSpec((tm, POOL_W), lambda b, i: (b * nt + i, 0)),
        out_shape=jax.ShapeDtypeStruct((batch * seq, POOL_W), BF),
        scratch_shapes=[pltpu.VMEM((tm + HIST_PAD, POOL_W), F32)],
        compiler_params=_cparams(("parallel", "arbitrary")),
        name="pool_mix",
    )(u, hist, w_bd, scale)


def _gla_levels(rows):
    out, s = [], 1
    while s < rows:
        out.append(s)
        s *= 2
    return out


def _gla_exponent_matrix(rows):
    t = np.arange(rows)[:, None]
    j = np.arange(rows)[None, :]
    blocks = [j <= t, j > t]
    for s in _gla_levels(rows):
        start = (t // s) * s
        blocks.append((j >= start) & (j <= t))
        if s > 1:
            blocks.append((j > t) & (j <= start + s - 1))
    return np.concatenate(blocks, axis=0).astype(np.float32)


def _gla_kernel(qkvg_ref, lr_ref, wa_ref, ba_ref, em_ref, gn_ref, s0_ref, o_ref, sout_ref, st_ref,
                *, rows):
    i = pl.program_id(1)
    last = pl.num_programs(1) - 1

    @pl.when(i == 0)
    def _():
        st_ref[...] = s0_ref[0]

    x = _dot(lr_ref[...], wa_ref[...]) + ba_ref[...]
    log_a = (jnp.minimum(x, 0.0) - _softplus_neg_abs(x)) * (1.0 / GLA_TAU)
    expo = _dot(em_ref[...], log_a.astype(BF))

    def block(n):
        return expo[n * rows:(n + 1) * rows]

    q = qkvg_ref[:, 0:GLA_WP].astype(F32) * (GLA_D ** -0.5)
    k = qkvg_ref[:, GLA_WP:2 * GLA_WP].astype(F32)
    vb = qkvg_ref[:, 2 * GLA_WP:3 * GLA_WP]
    gate = qkvg_ref[:, 3 * GLA_WP:4 * GLA_WP].astype(F32)

    b = block(0)
    q_dec = (q * jnp.exp(b)).astype(BF)
    k_dec = (k * jnp.exp(block(1))).astype(BF)
    b_last = b[rows - 1:rows]

    r = lax.broadcasted_iota(jnp.int32, (rows, rows), 0)
    c = lax.broadcasted_iota(jnp.int32, (rows, rows), 1)
    heads = [slice(h * GLA_DP, (h + 1) * GLA_DP) for h in range(GLA_HEADS)]

    qb = q.astype(BF)
    kb = k.astype(BF)
    att = [jnp.where(r == c, _dot_nt(qb[:, hs], kb[:, hs]), 0.0) for hs in heads]
    n = 2
    for s in _gla_levels(rows):
        shift = s.bit_length() - 1
        rb = r >> shift
        pair = ((rb & 1) == 1) & ((c >> shift) == rb - 1)
        q_lvl = (q * jnp.exp(block(n))).astype(BF)
        n += 1
        if s > 1:
            k_lvl = (k * jnp.exp(block(n))).astype(BF)
            n += 1
        else:
            k_lvl = kb
        for h, hs in enumerate(heads):
            att[h] = att[h] + jnp.where(pair, _dot_nt(q_lvl[:, hs], k_lvl[:, hs]), 0.0)

    for h, hs in enumerate(heads):
        st = st_ref[h]
        o = _dot(att[h].astype(BF), vb[:, hs]) + _dot_nt(q_dec[:, hs], st.astype(BF))
        ms = jnp.sum(o * o, axis=-1, keepdims=True) * (1.0 / GLA_D)
        y = o * lax.rsqrt(ms + EPS) * gn_ref[:, hs]
        gh = gate[:, hs]
        o_ref[:, hs] = (y * (gh * _sigmoid(gh))).astype(BF)
        st_new = st * jnp.exp(b_last[:, hs]) + _dot_tn(vb[:, hs], k_dec[:, hs])
        st_ref[h] = st_new

        @pl.when(i == last)
        def _(st_new=st_new, h=h):
            sout_ref[0, h] = st_new


def _gla(qkvg, lr, wa, ba, gn, s0t, batch, seq, rows):
    nt = seq // rows
    em = jnp.asarray(_gla_exponent_matrix(rows), BF)
    state = pl.BlockSpec((1, GLA_HEADS, GLA_DP, GLA_DP), lambda b, i: (b, 0, 0, 0))
    return pl.pallas_call(
        functools.partial(_gla_kernel, rows=rows),
        grid=(batch, nt),
        in_specs=[
            pl.BlockSpec((rows, 4 * GLA_WP), lambda b, i: (b * nt + i, 0)),
            pl.BlockSpec((rows, LANE), lambda b, i: (b * nt + i, 0)),
            _const_spec((LANE, GLA_WP)),
            _const_spec((1, GLA_WP)),
            _const_spec(em.shape),
            _const_spec((1, GLA_WP)),
            state,
        ],
        out_specs=[pl.BlockSpec((rows, GLA_WP), lambda b, i: (b * nt + i, 0)), state],
        out_shape=[jax.ShapeDtypeStruct((batch * seq, GLA_WP), BF),
                   jax.ShapeDtypeStruct((batch, GLA_HEADS, GLA_DP, GLA_DP), F32)],
        scratch_shapes=[pltpu.VMEM((GLA_HEADS, GLA_DP, GLA_DP), F32)],
        compiler_params=_cparams(("parallel", "arbitrary")),
        name="gla",
    )(qkvg, lr, wa, ba, em, gn, s0t)


def _merge_kernel(x_ref, oa_ref, ob_ref, oc_ref, gate_ref, wa_ref, wb_ref, wc_ref, wo_ref, g_ref,
                  y_ref, *, d_model):
    def gate(j):
        return _sigmoid(gate_ref[:, j * d_model:(j + 1) * d_model].astype(F32))

    merged = gate(0) * _dot(oa_ref[...], wa_ref[...])
    merged = merged + gate(1) * _dot(ob_ref[...], wb_ref[...])
    merged = merged + gate(2) * _dot(oc_ref[...], wc_ref[...])
    mix = _dot(merged.astype(BF), wo_ref[...])
    y_ref[...] = x_ref[...] + _rms(mix, g_ref[...])


def _merge(x, oa, ob, oc, gates, wa, wb, wc, wo, g, tm):
    n, d = x.shape
    row = lambda width: pl.BlockSpec((tm, width), lambda i: (i, 0))
    return pl.pallas_call(
        functools.partial(_merge_kernel, d_model=d),
        grid=(n // tm,),
        in_specs=[row(d), row(SB_W), row(POOL_W), row(GLA_WP), row(3 * d),
                  _const_spec(wa.shape), _const_spec(wb.shape), _const_spec(wc.shape),
                  _const_spec(wo.shape), _const_spec((1, d))],
        out_specs=row(d),
        out_shape=jax.ShapeDtypeStruct((n, d), F32),
        compiler_params=_cparams(("parallel",)),
        name="merge_out",
    )(x, oa, ob, oc, gates, wa, wb, wc, wo, g)


def _xattn_kernel(x_ref, mk_ref, mv_ref, wq_ref, wo_ref, gpre_ref, gpost_ref, y_ref, *, d_model):
    x = x_ref[...]
    q = _dot(_rms(x, gpre_ref[...]).astype(BF), wq_ref[...])
    hd = d_model // X_HEADS
    outs = []
    for h in range(X_HEADS):
        hs = slice(h * hd, (h + 1) * hd)
        s = _dot_nt(q[:, hs].astype(BF), mk_ref[0, :, hs].astype(BF)) * (hd ** -0.5)
        p = jnp.exp(s - jnp.max(s, axis=-1, keepdims=True))
        p = p / jnp.sum(p, axis=-1, keepdims=True)
        outs.append(_dot(p.astype(BF), mv_ref[0, :, hs].astype(BF)).astype(BF))
    o = jnp.concatenate(outs, axis=-1)
    y_ref[...] = x + _rms(_dot(o, wo_ref[...]), gpost_ref[...])


def _xattn(x, mk, mv, mem_index, wq, wo, gpre, gpost, batch, seq, tm):
    n, d = x.shape
    nt = seq // tm
    mem_len = mk.shape[-2]
    row = pl.BlockSpec((tm, d), lambda b, i: (b * nt + i, 0))
    mem = pl.BlockSpec((1, mem_len, d), lambda b, i: (mem_index(b), 0, 0))
    return pl.pallas_call(
        functools.partial(_xattn_kernel, d_model=d),
        grid=(batch, nt),
        in_specs=[row, mem, mem, _const_spec((d, d)), _const_spec((d, d)),
                  _const_spec((1, d)), _const_spec((1, d))],
        out_specs=row,
        out_shape=jax.ShapeDtypeStruct((n, d), F32),
        compiler_params=_cparams(("parallel", "parallel")),
        name="cross_attn",
    )(x, mk, mv, wq, wo, gpre, gpost)


def _ffn_kernel(x_ref, wg_ref, wu_ref, wo_ref, gpre_ref, gpost_ref, y_ref, *, chunk):
    x = x_ref[...]
    h = _rms(x, gpre_ref[...]).astype(BF)
    ff = wg_ref.shape[1]
    acc = jnp.zeros(x.shape, F32)
    for j in range(ff // chunk):
        cs = slice(j * chunk, (j + 1) * chunk)
        g = _dot(h, wg_ref[:, cs])
        u = _dot(h, wu_ref[:, cs])
        act = (g * _sigmoid(g) * u).astype(BF)
        acc = acc + _dot(act, wo_ref[cs, :])
    y_ref[...] = x + _rms(acc, gpost_ref[...])


def _ffn(x, wg, wu, wo, gpre, gpost, tm, chunk):
    n, d = x.shape
    row = pl.BlockSpec((tm, d), lambda i: (i, 0))
    return pl.pallas_call(
        functools.partial(_ffn_kernel, chunk=chunk),
        grid=(n // tm,),
        in_specs=[row, _const_spec(wg.shape), _const_spec(wu.shape), _const_spec(wo.shape),
                  _const_spec((1, d)), _const_spec((1, d))],
        out_specs=row,
        out_shape=jax.ShapeDtypeStruct((n, d), F32),
        compiler_params=_cparams(("parallel",)),
        name="ffn",
    )(x, wg, wu, wo, gpre, gpost)


def _memkv_kernel(m_ref, g_ref, wk_ref, wv_ref, k_ref, v_ref):
    m = _rms(m_ref[...], g_ref[0]).astype(BF)
    k_ref[0] = _dot(m, wk_ref[0].astype(BF))
    v_ref[0] = _dot(m, wv_ref[0].astype(BF))


def _memkv(mem, g, wk, wv, tm):
    depth, d, _ = wk.shape
    n = mem.shape[0]
    w = pl.BlockSpec((1, d, d), lambda l, i: (l, 0, 0))
    out = pl.BlockSpec((1, tm, d), lambda l, i: (l, i, 0))
    return pl.pallas_call(
        _memkv_kernel,
        grid=(depth, n // tm),
        in_specs=[pl.BlockSpec((tm, d), lambda l, i: (i, 0)),
                  pl.BlockSpec((1, 1, d), lambda l, i: (l, 0, 0)), w, w],
        out_specs=[out, out],
        out_shape=[jax.ShapeDtypeStruct((depth, n, d), F32)] * 2,
        compiler_params=_cparams(("parallel", "parallel")),
        name="mem_kv",
    )(mem, g, wk, wv)


def _pad_heads(w, axis):
    shape = w.shape
    w = w.reshape(shape[:axis] + (GLA_HEADS, GLA_D) + shape[axis + 1:])
    pad = [(0, 0)] * w.ndim
    pad[axis + 1] = (0, GLA_DP - GLA_D)
    w = jnp.pad(w, pad)
    return w.reshape(shape[:axis] + (GLA_WP,) + shape[axis + 1:])


def _pack_layer(p, l):
    d = p['w_in'].shape[1]
    w_in = p['w_in'][l]
    o = 3 * SB_W + POOL_W
    sb_u = w_in[:, :o]
    gla = [_pad_heads(w_in[:, o + j * GLA_W:o + (j + 1) * GLA_W], 1) for j in range(4)]
    o += 4 * GLA_W
    lr = jnp.pad(w_in[:, o:o + GLA_RANK], ((0, 0), (0, LANE - GLA_RANK)))
    gates = w_in[:, o + GLA_RANK:]
    eye = jnp.eye(len(POOL_WINDOWS), dtype=F32)
    w_pool_bd = (eye[:, None, :, None] * p['w_pool'][l][:, :, None, :]).reshape(POOL_W, POOL_W)
    ff = p['w_ffn_out'].shape[1]
    row = lambda v: v.reshape(1, -1)
    return dict(
        w_in=jnp.concatenate([sb_u] + gla + [lr, gates], axis=1).astype(BF),
        w_a2=jnp.pad(_pad_heads(p['w_gla_a2'][l], 1), ((0, LANE - GLA_RANK), (0, 0))).astype(BF),
        b_a=row(_pad_heads(p['b_gla_a'][l], 0)),
        gla_norm=row(_pad_heads(p['gla_norm'][l], 0)),
        w_pool=w_pool_bd.astype(BF),
        pool_scale=row(p['pool_scale'][l]),
        w_a=p['w_branch_a'][l].astype(BF),
        w_b=p['w_branch_b'][l].astype(BF),
        w_c=_pad_heads(p['w_branch_c'][l], 0).astype(BF),
        w_mix=p['w_mix_out'][l].astype(BF),
        w_xq=p['w_xq'][l].astype(BF),
        w_xo=p['w_xo'][l].astype(BF),
        w_g=p['w_ffn_in'][l][:, :ff].astype(BF),
        w_u=p['w_ffn_in'][l][:, ff:].astype(BF),
        w_o=p['w_ffn_out'][l].astype(BF),
        n_mix_pre=row(p['norm_mix_pre'][l]), n_mix_post=row(p['norm_mix_post'][l]),
        n_x_pre=row(p['norm_x_pre'][l]), n_x_post=row(p['norm_x_post'][l]),
        n_ffn_pre=row(p['norm_ffn_pre'][l]), n_ffn_post=row(p['norm_ffn_post'][l]),
    )


def _pad_state(s):
    s = jnp.swapaxes(s, -1, -2)
    return jnp.pad(s, ((0, 0), (0, 0), (0, GLA_DP - GLA_D), (0, GLA_DP - GLA_D)))


def _unpad_state(st):
    return jnp.swapaxes(st[:, :, :GLA_D, :GLA_D], -1, -2)


def _tiles(batch, seq):
    n = batch * seq
    return dict(
        proj=min(256, n),
        sb=min(256, seq),
        pool=min(256, seq),
        gla=min(128, seq),
        merge=min(512, n),
        xattn=min(512, seq),
        ffn=min(512, n),
    )


def _layer(x, w, batch, seq, sb_attend, pool_hist, pos0, s0t, mk, mv, mem_index, kv_stack=None):
    t = _tiles(batch, seq)
    q, k, v, kb, vb, u, qkvg, lr, gates = _proj(x, w['n_mix_pre'], w['w_in'], t['proj'], kv_stack)
    oa = sb_attend(q, kb, vb)
    ob = _pool(u, pool_hist, w['w_pool'], w['pool_scale'], batch, seq, t['pool'], pos0)
    oc, st = _gla(qkvg, lr, w['w_a2'], w['b_a'], w['gla_norm'], s0t, batch, seq, t['gla'])
    x = _merge(x, oa, ob, oc, gates, w['w_a'], w['w_b'], w['w_c'], w['w_mix'], w['n_mix_post'],
               t['merge'])
    x = _xattn(x, mk, mv, mem_index, w['w_xq'], w['w_xo'], w['n_x_pre'], w['n_x_post'],
               batch, seq, t['xattn'])
    x = _ffn(x, w['w_g'], w['w_u'], w['w_o'], w['n_ffn_pre'], w['n_ffn_post'], t['ffn'], 256)
    return x, k, v, u, st


def kernel(x_prompt, x_sample, mem_prompt, cache_sb_k, cache_sb_v, state_pool, state_gla, cache_mem_k, cache_mem_v, w_in, w_gla_a2, b_gla_a, gla_norm, w_pool, pool_scale, w_branch_a, w_branch_b, w_branch_c, w_mix_out, mem_norm, w_xq, w_xk, w_xv, w_xo, w_ffn_in, w_ffn_out, norm_mix_pre, norm_mix_post, norm_x_pre, norm_x_post, norm_ffn_pre, norm_ffn_post):
    p = dict(w_in=w_in, w_gla_a2=w_gla_a2, b_gla_a=b_gla_a, gla_norm=gla_norm, w_pool=w_pool,
             pool_scale=pool_scale, w_branch_a=w_branch_a, w_branch_b=w_branch_b,
             w_branch_c=w_branch_c, w_mix_out=w_mix_out, w_xq=w_xq, w_xo=w_xo,
             w_ffn_in=w_ffn_in, w_ffn_out=w_ffn_out, norm_mix_pre=norm_mix_pre,
             norm_mix_post=norm_mix_post, norm_x_pre=norm_x_pre, norm_x_post=norm_x_post,
             norm_ffn_pre=norm_ffn_pre, norm_ffn_post=norm_ffn_post)
    depth = w_in.shape[0]
    bp, tp, d = x_prompt.shape
    bs, ts, _ = x_sample.shape
    mem_len = mem_prompt.shape[1]
    past = cache_sb_k.shape[2]

    mk_all, mv_all = _memkv(mem_prompt.reshape(bp * mem_len, d), mem_norm.reshape(depth, 1, d),
                            w_xk, w_xv, min(512, bp * mem_len))
    mk_p = mk_all.reshape(depth * bp, mem_len, d)
    mv_p = mv_all.reshape(depth * bp, mem_len, d)
    mk_s = cache_mem_k.reshape(depth * bs, mem_len, d)
    mv_s = cache_mem_v.reshape(depth * bs, mem_len, d)
    ck = jnp.transpose(cache_sb_k, (0, 1, 3, 4, 2)).reshape(depth, bs, SB_W, past)
    cv = jnp.transpose(cache_sb_v, (0, 1, 3, 4, 2)).reshape(depth, bs, SB_W, past)

    yp = x_prompt.reshape(bp * tp, d)
    ys = x_sample.reshape(bs * ts, d)
    hist_zero = jnp.zeros((bp, HIST_PAD, POOL_W), F32)
    state_zero = jnp.zeros((bp, GLA_HEADS, GLA_DP, GLA_DP), F32)
    tiles_p = _tiles(bp, tp)
    kv_t = None
    poolp, glap, kss, vss, pools, glas = ([] for _ in range(6))
    for l in range(depth):
        w = _pack_layer(p, l)
        yp, k, v, u, st = _layer(
            yp, w, bp, tp,
            functools.partial(_sb_prompt, batch=bp, seq=tp, blk=tiles_p['sb']),
            hist_zero, 0, state_zero, mk_p, mv_p, lambda b, l=l: l * bp + b,
            kv_stack=(l, depth, bp, tp, kv_t))
        kv_t = (k, v)
        poolp.append(u.reshape(bp, tp, POOL_W)[:, tp - POOL_HIST:])
        glap.append(_unpad_state(st))

        hist = jnp.pad(state_pool[l], ((0, 0), (HIST_PAD - POOL_HIST, 0), (0, 0)))
        ys, k, v, u, st = _layer(
            ys, w, bs, ts,
            functools.partial(_sb_sample, cache_k=ck, cache_v=cv, layer=l, batch=bs, seq=ts,
                              blk=min(256, past)),
            hist, past, _pad_state(state_gla[l]), mk_s, mv_s, lambda b, l=l: l * bs + b)
        kss.append(k.reshape(bs, ts, SB_HEADS, SB_DIM))
        vss.append(v.reshape(bs, ts, SB_HEADS, SB_DIM))
        full = jnp.concatenate([state_pool[l], u.reshape(bs, ts, POOL_W)], axis=1)
        pools.append(full[:, -POOL_HIST:])
        glas.append(_unpad_state(st))

    kp, vp = (jnp.transpose(a.reshape(depth, bp, SB_HEADS, SB_DIM, tp), (0, 1, 4, 2, 3)) for a in kv_t)
    return (yp.reshape(bp, tp, d), ys.reshape(bs, ts, d),
            kp, vp, jnp.stack(poolp), jnp.stack(glap),
            mk_all.reshape(depth, bp, mem_len, X_HEADS, d // X_HEADS),
            mv_all.reshape(depth, bp, mem_len, X_HEADS, d // X_HEADS),
            jnp.stack(kss), jnp.stack(vss), jnp.stack(pools), jnp.stack(glas))
```

```python
import functools

import numpy as np
import jax
import jax.numpy as jnp
from jax import lax
from jax.experimental import pallas as pl
from jax.experimental.pallas import tpu as pltpu

BF = jnp.bfloat16
F32 = jnp.float32
EPS = 1e-6

LANE = 128
VMEM_LIMIT = 56 * 1024 * 1024

SB_HEADS = 6
SB_DIM = 64
SB_W = SB_HEADS * SB_DIM
POOL_WINDOWS = (2, 4, 8, 16)
POOL_GDIM = 64
POOL_W = len(POOL_WINDOWS) * POOL_GDIM
POOL_HIST = max(POOL_WINDOWS) - 1
HIST_PAD = 16
GLA_HEADS = 4
GLA_D = 96
GLA_DP = LANE
GLA_W = GLA_HEADS * GLA_D
GLA_WP = GLA_HEADS * GLA_DP
GLA_RANK = 16
GLA_TAU = 16.0
X_HEADS = 4

C_Q = 0
C_K = C_Q + SB_W
C_V = C_K + SB_W
C_U = C_V + SB_W
C_GLA = C_U + POOL_W
C_LR = C_GLA + 4 * GLA_WP
C_GATE = C_LR + LANE


def _cparams(sem):
    return pltpu.CompilerParams(dimension_semantics=sem, vmem_limit_bytes=VMEM_LIMIT)


def _rms(x, g):
    return x * lax.rsqrt(jnp.mean(x * x, axis=-1, keepdims=True) + EPS) * g


def _softplus_neg_abs(z):
    return jnp.log(1.0 + jnp.exp(-jnp.abs(z)))


def _sigmoid(z):
    return 1.0 / (1.0 + jnp.exp(-z))


def _dot(a, b):
    return jnp.dot(a, b, preferred_element_type=F32)


def _dot_nt(a, b):
    return lax.dot_general(a, b, (((1,), (1,)), ((), ())), preferred_element_type=F32)


def _dot_tn(a, b):
    return lax.dot_general(a, b, (((0,), (0,)), ((), ())), preferred_element_type=F32)


def _const_spec(shape):
    zeros = (0,) * len(shape)
    return pl.BlockSpec(shape, lambda *_: zeros)


def _proj_kernel(*refs, d_model, n_alias, transposed_kv):
    x_ref, g_ref, w_ref, wa_ref, ba_ref = refs[:N_PROJ_IN]
    q_ref, k_ref, v_ref, kb_ref, vb_ref, u_ref, gla_ref, la_ref, gate_ref = refs[N_PROJ_IN + n_alias:]
    h = _rms(x_ref[...], g_ref[...]).astype(BF)

    def mm(lo, width):
        return _dot_nt(h, w_ref[lo:lo + width, :])

    q_ref[...] = (mm(C_Q, SB_W) * (SB_DIM ** -0.5)).astype(BF)
    for lo, f32_ref, bf_ref in ((C_K, k_ref, kb_ref), (C_V, v_ref, vb_ref)):
        kv = mm(lo, SB_W)
        bf_ref[...] = kv.astype(BF)
        if transposed_kv:
            f32_ref[0, 0] = kv.T
        else:
            f32_ref[...] = kv
    u_ref[...] = mm(C_U, POOL_W)
    for j in range(4):
        gla_ref[:, j * GLA_WP:(j + 1) * GLA_WP] = mm(C_GLA + j * GLA_WP, GLA_WP).astype(BF)
    xa = _dot(mm(C_LR, LANE).astype(BF), wa_ref[...]) + ba_ref[...]
    la_ref[...] = ((jnp.minimum(xa, 0.0) - _softplus_neg_abs(xa)) * (1.0 / GLA_TAU)).astype(BF)
    for j in range(3):
        gate_ref[:, j * d_model:(j + 1) * d_model] = mm(C_GATE + j * d_model, d_model).astype(BF)


N_PROJ_IN = 5


def _proj(x, g, w, wa, ba, tm, kv_stack=None):
    n, d = x.shape
    wtot = w.shape[0]
    row = lambda width: pl.BlockSpec((tm, width), lambda i: (i, 0))
    if kv_stack is None:
        kv_spec, kv_shape, extra, aliases = row(SB_W), (n, SB_W), [], {}
    else:
        layer, depth, batch, seq, bufs = kv_stack
        nt = seq // tm
        kv_spec = pl.BlockSpec((1, 1, SB_W, tm), lambda i: (layer, i // nt, 0, i % nt))
        kv_shape = (depth, batch, SB_W, seq)
        extra = [] if bufs is None else list(bufs)
        aliases = {N_PROJ_IN + j: 1 + j for j in range(len(extra))}
    outs = [
        (row(SB_W), (n, SB_W), BF), (kv_spec, kv_shape, F32), (kv_spec, kv_shape, F32),
        (row(SB_W), (n, SB_W), BF), (row(SB_W), (n, SB_W), BF), (row(POOL_W), (n, POOL_W), F32),
        (row(4 * GLA_WP), (n, 4 * GLA_WP), BF), (row(GLA_WP), (n, GLA_WP), BF),
        (row(3 * d), (n, 3 * d), BF),
    ]
    return pl.pallas_call(
        functools.partial(_proj_kernel, d_model=d, n_alias=len(extra),
                          transposed_kv=kv_stack is not None),
        grid=(n // tm,),
        in_specs=[row(d), _const_spec((1, d)), _const_spec((wtot, d)),
                  _const_spec((LANE, GLA_WP)), _const_spec((1, GLA_WP))]
                 + [pl.BlockSpec(memory_space=pl.ANY)] * len(extra),
        out_specs=[spec for spec, _, _ in outs],
        out_shape=[jax.ShapeDtypeStruct(shape, dt) for _, shape, dt in outs],
        input_output_aliases=aliases,
        compiler_params=_cparams(("parallel",)),
        name="in_proj",
    )(x, g, w, wa, ba, *extra)


def _upper_ones(n):
    r = lax.broadcasted_iota(jnp.int32, (n, n), 0)
    c = lax.broadcasted_iota(jnp.int32, (n, n), 1)
    return jnp.where(r >= c, 1.0, 0.0).astype(BF)


def _strict_causal(n):
    r = lax.broadcasted_iota(jnp.int32, (n, n), 0)
    c = lax.broadcasted_iota(jnp.int32, (n, n), 1)
    return c < r


def _sb_weights(z, carry, upper, mask):
    zb = z.astype(BF)
    t = jnp.maximum(zb, 0.0) + _softplus_neg_abs(zb)
    if mask is not None:
        t = jnp.where(mask, t, jnp.zeros_like(t))
    cum = _dot(t, upper)
    w = jnp.exp(z - cum - carry)
    if mask is not None:
        w = jnp.where(mask, w, 0.0)
    return w.astype(BF), carry + cum[:, 0:1]


def _sb_prompt_kernel(q_ref, k_ref, v_ref, o_ref, acc_ref, *, blk):
    i = pl.program_id(1)
    pair_w = 2 * SB_DIM
    n_pairs = SB_HEADS // 2
    upper = _upper_ones(blk)
    causal = _strict_causal(blk)
    low = lax.broadcasted_iota(jnp.int32, (blk, pair_w), 1) < SB_DIM
    zero = jnp.zeros((blk, pair_w), BF)

    def split(x):
        return jnp.where(low, x, zero), jnp.where(low, zero, x)

    q_heads = []
    for p in range(n_pairs):
        q_heads.extend(split(q_ref[:, p * pair_w:(p + 1) * pair_w]))
    acc_ref[...] = jnp.zeros(acc_ref.shape, F32)

    def visit(kb, carries, mask):
        start = pl.multiple_of(kb * blk, blk)
        out = []
        for p in range(n_pairs):
            ps = slice(p * pair_w, (p + 1) * pair_w)
            kp = k_ref[pl.ds(start, blk), ps]
            v_lo, v_hi = split(v_ref[pl.ds(start, blk), ps])
            ws = []
            for e in range(2):
                h = 2 * p + e
                w, carry = _sb_weights(_dot_nt(q_heads[h], kp), carries[h], upper, mask)
                ws.append(w)
                out.append(carry)
            acc_ref[:, ps] += _dot(jnp.concatenate(ws, axis=1), jnp.concatenate([v_lo, v_hi], axis=0))
        return tuple(out)

    carries = tuple(jnp.zeros((blk, 1), F32) for _ in range(SB_HEADS))
    carries = visit(i, carries, causal)
    lax.fori_loop(0, i, lambda j, c: visit(i - 1 - j, c, None), carries)
    o_ref[...] = acc_ref[...].astype(BF)


def _sb_prompt(q, kb, vb, batch, seq, blk):
    nblk = seq // blk
    return pl.pallas_call(
        functools.partial(_sb_prompt_kernel, blk=blk),
        grid=(batch, nblk),
        in_specs=[
            pl.BlockSpec((blk, SB_W), lambda b, i: (b * nblk + i, 0)),
            pl.BlockSpec((seq, SB_W), lambda b, i: (b, 0)),
            pl.BlockSpec((seq, SB_W), lambda b, i: (b, 0)),
        ],
        out_specs=pl.BlockSpec((blk, SB_W), lambda b, i: (b * nblk + i, 0)),
        out_shape=jax.ShapeDtypeStruct((batch * seq, SB_W), BF),
        scratch_shapes=[pltpu.VMEM((blk, SB_W), F32)],
        compiler_params=_cparams(("parallel", "arbitrary")),
        name="sb_prompt",
    )(q, kb, vb)


def _sb_sample_kernel(q_ref, kn_ref, vn_ref, kc_ref, vc_ref, o_ref, *, seq, past, blk):
    pair_w = 2 * SB_DIM
    n_pairs = SB_HEADS // 2
    pairs = [slice(p * pair_w, (p + 1) * pair_w) for p in range(n_pairs)]

    def split(x, axis):
        low = lax.broadcasted_iota(jnp.int32, x.shape, axis) < SB_DIM
        zero = jnp.zeros(x.shape, x.dtype)
        return jnp.where(low, x, zero), jnp.where(low, zero, x)

    q_heads = []
    for ps in pairs:
        q_heads.extend(split(q_ref[:, ps], 1))
    r = lax.broadcasted_iota(jnp.int32, (SB_HEADS * seq, seq), 0) % seq
    causal = lax.broadcasted_iota(jnp.int32, (SB_HEADS * seq, seq), 1) < r

    def visit(k_pairs, v_pairs, transposed, carry, acc, upper, mask):
        qk = _dot if transposed else _dot_nt
        pv = _dot_nt if transposed else _dot
        z = jnp.concatenate([qk(q_heads[h], k_pairs[h // 2]) for h in range(SB_HEADS)], axis=0)
        w, carry = _sb_weights(z, carry, upper, mask)
        out = []
        for p in range(n_pairs):
            v_lo, v_hi = split(v_pairs[p], 0 if transposed else 1)
            w_lo = w[(2 * p) * seq:(2 * p + 1) * seq]
            w_hi = w[(2 * p + 1) * seq:(2 * p + 2) * seq]
            out.append(acc[p] + pv(w_lo, v_lo) + pv(w_hi, v_hi))
        return carry, out

    carry = jnp.zeros((SB_HEADS * seq, 1), F32)
    acc = [jnp.zeros((seq, pair_w), F32) for _ in pairs]
    carry, acc = visit([kn_ref[:, ps] for ps in pairs], [vn_ref[:, ps] for ps in pairs], False,
                       carry, acc, _upper_ones(seq), causal)
    upper_past = _upper_ones(blk)
    for j in reversed(range(past // blk)):
        cols = slice(j * blk, (j + 1) * blk)
        carry, acc = visit([kc_ref[0, 0, ps, cols].astype(BF) for ps in pairs],
                           [vc_ref[0, 0, ps, cols].astype(BF) for ps in pairs], True,
                           carry, acc, upper_past, None)
    for p, ps in enumerate(pairs):
        o_ref[:, ps] = acc[p].astype(BF)


def _sb_sample(q, kb, vb, cache_k, cache_v, layer, batch, seq, blk):
    past = cache_k.shape[3]
    new = pl.BlockSpec((seq, SB_W), lambda b: (b, 0))
    cache = pl.BlockSpec((1, 1, SB_W, past), lambda b: (layer, b, 0, 0))
    return pl.pallas_call(
        functools.partial(_sb_sample_kernel, seq=seq, past=past, blk=blk),
        grid=(batch,),
        in_specs=[new, new, new, cache, cache],
        out_specs=new,
        out_shape=jax.ShapeDtypeStruct((batch * seq, SB_W), BF),
        compiler_params=_cparams(("parallel",)),
        name="sb_sample",
    )(q, kb, vb, cache_k, cache_v)


def _pool_kernel(u_ref, hist_ref, w_ref, scale_ref, o_ref, buf_ref, *, tm, pos0):
    i = pl.program_id(1)

    @pl.when(i == 0)
    def _():
        buf_ref[0:HIST_PAD, :] = hist_ref[0]

    @pl.when(i > 0)
    def _():
        buf_ref[0:HIST_PAD, :] = buf_ref[tm:tm + HIST_PAD, :]

    u = u_ref[...]
    buf_ref[HIST_PAD:HIST_PAD + tm, :] = u

    lane = lax.broadcasted_iota(jnp.int32, (tm, POOL_W), 1)
    group = lane // POOL_GDIM
    pos = pos0 + i * tm + lax.broadcasted_iota(jnp.int32, (tm, POOL_W), 0)
    run = u
    win_sum = jnp.zeros((tm, POOL_W), F32)
    width = jnp.zeros((tm, POOL_W), jnp.int32)
    for d in range(1, max(POOL_WINDOWS)):
        run = run + buf_ref[HIST_PAD - d:HIST_PAD - d + tm, :]
        if d + 1 in POOL_WINDOWS:
            g = POOL_WINDOWS.index(d + 1)
            win_sum = jnp.where(group == g, run, win_sum)
            width = jnp.where(group == g, d + 1, width)
    cnt = jnp.minimum(pos + 1, width).astype(F32)
    pooled = win_sum / cnt - u
    o_ref[...] = (_dot(pooled.astype(BF), w_ref[...]) * scale_ref[...]).astype(BF)


def _pool(u, hist, w_bd, scale, batch, seq, tm, pos0):
    nt = seq // tm
    return pl.pallas_call(
        functools.partial(_pool_kernel, tm=tm, pos0=pos0),
        grid=(batch, nt),
        in_specs=[
            pl.BlockSpec((tm, POOL_W), lambda b, i: (b * nt + i, 0)),
            pl.BlockSpec((1, HIST_PAD, POOL_W), lambda b, i: (b, 0, 0)),
            _const_spec((POOL_W, POOL_W)),
            _const_spec((1, POOL_W)),
        ],
        out_specs=pl.BlockSpec((tm, POOL_W), lambda b, i: (b * nt + i, 0)),
        out_shape=jax.ShapeDtypeStruct((batch * seq, POOL_W), BF),
        scratch_shapes=[pltpu.VMEM((tm + HIST_PAD, POOL_W), F32)],
        compiler_params=_cparams(("parallel", "arbitrary")),
        name="pool_mix",
    )(u, hist, w_bd, scale)


def _gla_levels(rows):
    out, s = [], 1
    while s < rows:
        out.append(s)
        s *= 2
    return out


def _gla_exponent_matrix(rows):
    t = np.arange(rows)[:, None]
    j = np.arange(rows)[None, :]
    blocks = [j <= t]
    for s in _gla_levels(rows)[1:]:
        start = (t // s) * s
        later = (t // s) % 2 == 1
        blocks.append(np.where(later, (j >= start) & (j <= t), (j > t) & (j <= start + s - 1)))
    return np.concatenate(blocks, axis=0).astype(np.float32)


def _gla_kernel(qkvg_ref, la_ref, em_ref, gn_ref, s0_ref, o_ref, st_ref, *, rows):
    @pl.when(pl.program_id(1) == 0)
    def _():
        st_ref[...] = s0_ref[...]

    expo = _dot(em_ref[...], la_ref[...])
    log_a = la_ref[...].astype(F32)

    q = qkvg_ref[:, 0:GLA_WP].astype(F32) * (GLA_D ** -0.5)
    k = qkvg_ref[:, GLA_WP:2 * GLA_WP].astype(F32)
    vb = qkvg_ref[:, 2 * GLA_WP:3 * GLA_WP]
    gate = qkvg_ref[:, 3 * GLA_WP:4 * GLA_WP].astype(F32)

    b = expo[0:rows]
    b_last = b[rows - 1:rows]
    q_dec = (q * jnp.exp(b)).astype(BF)
    k_dec = (k * jnp.exp(b_last - b)).astype(BF)

    row = lax.broadcasted_iota(jnp.int32, (rows, GLA_WP), 0)
    r = lax.broadcasted_iota(jnp.int32, (rows, rows), 0)
    c = lax.broadcasted_iota(jnp.int32, (rows, rows), 1)
    heads = [slice(h * GLA_DP, (h + 1) * GLA_DP) for h in range(GLA_HEADS)]

    qb = q.astype(BF)
    kb = k.astype(BF)
    att = [jnp.where(r == c, _dot_nt(qb[:, hs], kb[:, hs]), 0.0) for hs in heads]
    for n, s in enumerate(_gla_levels(rows)):
        shift = s.bit_length() - 1
        later = ((row >> shift) & 1) == 1
        rb = r >> shift
        pair = ((rb & 1) == 1) & ((c >> shift) == rb - 1)
        e = jnp.where(later, log_a, 0.0) if s == 1 else expo[n * rows:(n + 1) * rows]
        x_lvl = (jnp.where(later, q, k) * jnp.exp(e)).astype(BF)
        for h, hs in enumerate(heads):
            att[h] = att[h] + jnp.where(pair, _dot_nt(x_lvl[:, hs], x_lvl[:, hs]), 0.0)

    for h, hs in enumerate(heads):
        st = st_ref[0, h]
        o = _dot(att[h].astype(BF), vb[:, hs]) + _dot_nt(q_dec[:, hs], st.astype(BF))
        ms = jnp.sum(o * o, axis=-1, keepdims=True) * (1.0 / GLA_D)
        y = o * lax.rsqrt(ms + EPS) * gn_ref[:, hs]
        gh = gate[:, hs]
        o_ref[:, hs] = (y * (gh * _sigmoid(gh))).astype(BF)
        st_ref[0, h] = st * jnp.exp(b_last[:, hs]) + _dot_tn(vb[:, hs], k_dec[:, hs])


def _gla(qkvg, log_a, gn, s0t, batch, seq, rows):
    nt = seq // rows
    em = jnp.asarray(_gla_exponent_matrix(rows), BF)
    state = pl.BlockSpec((1, GLA_HEADS, GLA_DP, GLA_DP), lambda b, i: (b, 0, 0, 0))
    return pl.pallas_call(
        functools.partial(_gla_kernel, rows=rows),
        grid=(batch, nt),
        in_specs=[
            pl.BlockSpec((rows, 4 * GLA_WP), lambda b, i: (b * nt + i, 0)),
            pl.BlockSpec((rows, GLA_WP), lambda b, i: (b * nt + i, 0)),
            _const_spec(em.shape),
            _const_spec((1, GLA_WP)),
            state,
        ],
        out_specs=[pl.BlockSpec((rows, GLA_WP), lambda b, i: (b * nt + i, 0)), state],
        out_shape=[jax.ShapeDtypeStruct((batch * seq, GLA_WP), BF),
                   jax.ShapeDtypeStruct((batch, GLA_HEADS, GLA_DP, GLA_DP), F32)],
        compiler_params=_cparams(("parallel", "arbitrary")),
        name="gla",
    )(qkvg, log_a, em, gn, s0t)


def _merge_kernel(x_ref, oa_ref, ob_ref, oc_ref, gate_ref, wa_ref, wb_ref, wc_ref, wo_ref, g_ref,
                  y_ref, *, d_model):
    def gate(j):
        return _sigmoid(gate_ref[:, j * d_model:(j + 1) * d_model].astype(F32))

    merged = gate(0) * _dot(oa_ref[...], wa_ref[...])
    merged = merged + gate(1) * _dot(ob_ref[...], wb_ref[...])
    merged = merged + gate(2) * _dot(oc_ref[...], wc_ref[...])
    mix = _dot(merged.astype(BF), wo_ref[...])
    y_ref[...] = x_ref[...] + _rms(mix, g_ref[...])


def _merge(x, oa, ob, oc, gates, wa, wb, wc, wo, g, tm):
    n, d = x.shape
    row = lambda width: pl.BlockSpec((tm, width), lambda i: (i, 0))
    return pl.pallas_call(
        functools.partial(_merge_kernel, d_model=d),
        grid=(n // tm,),
        in_specs=[row(d), row(SB_W), row(POOL_W), row(GLA_WP), row(3 * d),
                  _const_spec(wa.shape), _const_spec(wb.shape), _const_spec(wc.shape),
                  _const_spec(wo.shape), _const_spec((1, d))],
        out_specs=row(d),
        out_shape=jax.ShapeDtypeStruct((n, d), F32),
        compiler_params=_cparams(("parallel",)),
        name="merge_out",
    )(x, oa, ob, oc, gates, wa, wb, wc, wo, g)


def _xattn_kernel(x_ref, mk_ref, mv_ref, wq_ref, wo_ref, gpre_ref, gpost_ref, y_ref, *, d_model):
    x = x_ref[...]
    q = _dot(_rms(x, gpre_ref[...]).astype(BF), wq_ref[...])
    hd = d_model // X_HEADS
    outs = []
    for h in range(X_HEADS):
        hs = slice(h * hd, (h + 1) * hd)
        s = _dot_nt(q[:, hs].astype(BF), mk_ref[0, :, hs].astype(BF)) * (hd ** -0.5)
        p = jnp.exp(s - jnp.max(s, axis=-1, keepdims=True))
        p = p / jnp.sum(p, axis=-1, keepdims=True)
        outs.append(_dot(p.astype(BF), mv_ref[0, :, hs].astype(BF)).astype(BF))
    o = jnp.concatenate(outs, axis=-1)
    y_ref[...] = x + _rms(_dot(o, wo_ref[...]), gpost_ref[...])


def _xattn(x, mk, mv, mem_index, wq, wo, gpre, gpost, batch, seq, tm):
    n, d = x.shape
    nt = seq // tm
    mem_len = mk.shape[-2]
    row = pl.BlockSpec((tm, d), lambda b, i: (b * nt + i, 0))
    mem = pl.BlockSpec((1, mem_len, d), lambda b, i: (mem_index(b), 0, 0))
    return pl.pallas_call(
        functools.partial(_xattn_kernel, d_model=d),
        grid=(batch, nt),
        in_specs=[row, mem, mem, _const_spec((d, d)), _const_spec((d, d)),
                  _const_spec((1, d)), _const_spec((1, d))],
        out_specs=row,
        out_shape=jax.ShapeDtypeStruct((n, d), F32),
        compiler_params=_cparams(("parallel", "parallel")),
        name="cross_attn",
    )(x, mk, mv, wq, wo, gpre, gpost)


def _ffn_kernel(x_ref, wg_ref, wu_ref, wo_ref, gpre_ref, gpost_ref, y_ref, *, chunk):
    x = x_ref[...]
    h = _rms(x, gpre_ref[...]).astype(BF)
    ff = wg_ref.shape[1]
    acc = jnp.zeros(x.shape, F32)
    for j in range(ff // chunk):
        cs = slice(j * chunk, (j + 1) * chunk)
        g = _dot(h, wg_ref[:, cs])
        u = _dot(h, wu_ref[:, cs])
        act = (g * _sigmoid(g) * u).astype(BF)
        acc = acc + _dot(act, wo_ref[cs, :])
    y_ref[...] = x + _rms(acc, gpost_ref[...])


def _ffn(x, wg, wu, wo, gpre, gpost, tm, chunk):
    n, d = x.shape
    row = pl.BlockSpec((tm, d), lambda i: (i, 0))
    return pl.pallas_call(
        functools.partial(_ffn_kernel, chunk=chunk),
        grid=(n // tm,),
        in_specs=[row, _const_spec(wg.shape), _const_spec(wu.shape), _const_spec(wo.shape),
                  _const_spec((1, d)), _const_spec((1, d))],
        out_specs=row,
        out_shape=jax.ShapeDtypeStruct((n, d), F32),
        compiler_params=_cparams(("parallel",)),
        name="ffn",
    )(x, wg, wu, wo, gpre, gpost)


def _memkv_kernel(m_ref, g_ref, wk_ref, wv_ref, k_ref, v_ref):
    m = _rms(m_ref[...], g_ref[0]).astype(BF)
    k_ref[0] = _dot(m, wk_ref[0].astype(BF))
    v_ref[0] = _dot(m, wv_ref[0].astype(BF))


def _memkv(mem, g, wk, wv, tm):
    depth, d, _ = wk.shape
    n = mem.shape[0]
    w = pl.BlockSpec((1, d, d), lambda l, i: (l, 0, 0))
    out = pl.BlockSpec((1, tm, d), lambda l, i: (l, i, 0))
    return pl.pallas_call(
        _memkv_kernel,
        grid=(depth, n // tm),
        in_specs=[pl.BlockSpec((tm, d), lambda l, i: (i, 0)),
                  pl.BlockSpec((1, 1, d), lambda l, i: (l, 0, 0)), w, w],
        out_specs=[out, out],
        out_shape=[jax.ShapeDtypeStruct((depth, n, d), F32)] * 2,
        compiler_params=_cparams(("parallel", "parallel")),
        name="mem_kv",
    )(mem, g, wk, wv)


def _pad_heads(w, axis):
    shape = w.shape
    w = w.reshape(shape[:axis] + (GLA_HEADS, GLA_D) + shape[axis + 1:])
    pad = [(0, 0)] * w.ndim
    pad[axis + 1] = (0, GLA_DP - GLA_D)
    w = jnp.pad(w, pad)
    return w.reshape(shape[:axis] + (GLA_WP,) + shape[axis + 1:])


def _pack_layer(p, l):
    w_in = jnp.swapaxes(p['w_in'][l], 0, 1)
    o = 3 * SB_W + POOL_W
    sb_u = w_in[:o]
    gla = [_pad_heads(w_in[o + j * GLA_W:o + (j + 1) * GLA_W], 0) for j in range(4)]
    o += 4 * GLA_W
    lr = jnp.pad(w_in[o:o + GLA_RANK], ((0, LANE - GLA_RANK), (0, 0)))
    gates = w_in[o + GLA_RANK:]
    eye = jnp.eye(len(POOL_WINDOWS), dtype=F32)
    w_pool_bd = (eye[:, None, :, None] * p['w_pool'][l][:, :, None, :]).reshape(POOL_W, POOL_W)
    ff = p['w_ffn_out'].shape[1]
    row = lambda v: v.reshape(1, -1)
    return dict(
        w_in=jnp.concatenate([sb_u] + gla + [lr, gates], axis=0).astype(BF),
        w_a2=jnp.pad(_pad_heads(p['w_gla_a2'][l], 1), ((0, LANE - GLA_RANK), (0, 0))).astype(BF),
        b_a=row(_pad_heads(p['b_gla_a'][l], 0)),
        gla_norm=row(_pad_heads(p['gla_norm'][l], 0)),
        w_pool=w_pool_bd.astype(BF),
        pool_scale=row(p['pool_scale'][l]),
        w_a=p['w_branch_a'][l].astype(BF),
        w_b=p['w_branch_b'][l].astype(BF),
        w_c=_pad_heads(p['w_branch_c'][l], 0).astype(BF),
        w_mix=p['w_mix_out'][l].astype(BF),
        w_xq=p['w_xq'][l].astype(BF),
        w_xo=p['w_xo'][l].astype(BF),
        w_g=p['w_ffn_in'][l][:, :ff].astype(BF),
        w_u=p['w_ffn_in'][l][:, ff:].astype(BF),
        w_o=p['w_ffn_out'][l].astype(BF),
        n_mix_pre=row(p['norm_mix_pre'][l]), n_mix_post=row(p['norm_mix_post'][l]),
        n_x_pre=row(p['norm_x_pre'][l]), n_x_post=row(p['norm_x_post'][l]),
        n_ffn_pre=row(p['norm_ffn_pre'][l]), n_ffn_post=row(p['norm_ffn_post'][l]),
    )


def _pad_state(s):
    s = jnp.swapaxes(s, -1, -2)
    return jnp.pad(s, ((0, 0), (0, 0), (0, GLA_DP - GLA_D), (0, GLA_DP - GLA_D)))


def _unpad_state(st):
    return jnp.swapaxes(st[:, :, :GLA_D, :GLA_D], -1, -2)


def _tiles(batch, seq):
    n = batch * seq
    return dict(
        proj=min(256, n),
        sb=min(256, seq),
        pool=min(256, seq),
        gla=min(128, seq),
        merge=min(512, n),
        xattn=min(512, seq),
        ffn=min(512, n),
    )


def _layer(x, w, batch, seq, sb_attend, pool_hist, pos0, s0t, mk, mv, mem_index, kv_stack=None):
    t = _tiles(batch, seq)
    q, k, v, kb, vb, u, qkvg, log_a, gates = _proj(x, w['n_mix_pre'], w['w_in'], w['w_a2'], w['b_a'],
                                                   t['proj'], kv_stack)
    oa = sb_attend(q, kb, vb)
    ob = _pool(u, pool_hist, w['w_pool'], w['pool_scale'], batch, seq, t['pool'], pos0)
    oc, st = _gla(qkvg, log_a, w['gla_norm'], s0t, batch, seq, t['gla'])
    x = _merge(x, oa, ob, oc, gates, w['w_a'], w['w_b'], w['w_c'], w['w_mix'], w['n_mix_post'],
               t['merge'])
    x = _xattn(x, mk, mv, mem_index, w['w_xq'], w['w_xo'], w['n_x_pre'], w['n_x_post'],
               batch, seq, t['xattn'])
    x = _ffn(x, w['w_g'], w['w_u'], w['w_o'], w['n_ffn_pre'], w['n_ffn_post'], t['ffn'], 256)
    return x, k, v, u, st


def kernel(x_prompt, x_sample, mem_prompt, cache_sb_k, cache_sb_v, state_pool, state_gla, cache_mem_k, cache_mem_v, w_in, w_gla_a2, b_gla_a, gla_norm, w_pool, pool_scale, w_branch_a, w_branch_b, w_branch_c, w_mix_out, mem_norm, w_xq, w_xk, w_xv, w_xo, w_ffn_in, w_ffn_out, norm_mix_pre, norm_mix_post, norm_x_pre, norm_x_post, norm_ffn_pre, norm_ffn_post):
    p = dict(w_in=w_in, w_gla_a2=w_gla_a2, b_gla_a=b_gla_a, gla_norm=gla_norm, w_pool=w_pool,
             pool_scale=pool_scale, w_branch_a=w_branch_a, w_branch_b=w_branch_b,
             w_branch_c=w_branch_c, w_mix_out=w_mix_out, w_xq=w_xq, w_xo=w_xo,
             w_ffn_in=w_ffn_in, w_ffn_out=w_ffn_out, norm_mix_pre=norm_mix_pre,
             norm_mix_post=norm_mix_post, norm_x_pre=norm_x_pre, norm_x_post=norm_x_post,
             norm_ffn_pre=norm_ffn_pre, norm_ffn_post=norm_ffn_post)
    depth = w_in.shape[0]
    bp, tp, d = x_prompt.shape
    bs, ts, _ = x_sample.shape
    mem_len = mem_prompt.shape[1]
    past = cache_sb_k.shape[2]

    mk_all, mv_all = _memkv(mem_prompt.reshape(bp * mem_len, d), mem_norm.reshape(depth, 1, d),
                            w_xk, w_xv, min(512, bp * mem_len))
    mk_p = mk_all.reshape(depth * bp, mem_len, d)
    mv_p = mv_all.reshape(depth * bp, mem_len, d)
    mk_s = cache_mem_k.reshape(depth * bs, mem_len, d)
    mv_s = cache_mem_v.reshape(depth * bs, mem_len, d)
    ck = jnp.transpose(cache_sb_k, (0, 1, 3, 4, 2)).reshape(depth, bs, SB_W, past)
    cv = jnp.transpose(cache_sb_v, (0, 1, 3, 4, 2)).reshape(depth, bs, SB_W, past)

    yp = x_prompt.reshape(bp * tp, d)
    ys = x_sample.reshape(bs * ts, d)
    hist_zero = jnp.zeros((bp, HIST_PAD, POOL_W), F32)
    state_zero = jnp.zeros((bp, GLA_HEADS, GLA_DP, GLA_DP), F32)
    tiles_p = _tiles(bp, tp)
    kv_t = None
    poolp, glap, kss, vss, pools, glas = ([] for _ in range(6))
    for l in range(depth):
        w = _pack_layer(p, l)
        yp, k, v, u, st = _layer(
            yp, w, bp, tp,
            functools.partial(_sb_prompt, batch=bp, seq=tp, blk=tiles_p['sb']),
            hist_zero, 0, state_zero, mk_p, mv_p, lambda b, l=l: l * bp + b,
            kv_stack=(l, depth, bp, tp, kv_t))
        kv_t = (k, v)
        poolp.append(u.reshape(bp, tp, POOL_W)[:, tp - POOL_HIST:])
        glap.append(_unpad_state(st))

        hist = jnp.pad(state_pool[l], ((0, 0), (HIST_PAD - POOL_HIST, 0), (0, 0)))
        ys, k, v, u, st = _layer(
            ys, w, bs, ts,
            functools.partial(_sb_sample, cache_k=ck, cache_v=cv, layer=l, batch=bs, seq=ts,
                              blk=min(256, past)),
            hist, past, _pad_state(state_gla[l]), mk_s, mv_s, lambda b, l=l: l * bs + b)
        kss.append(k.reshape(bs, ts, SB_HEADS, SB_DIM))
        vss.append(v.reshape(bs, ts, SB_HEADS, SB_DIM))
        full = jnp.concatenate([state_pool[l], u.reshape(bs, ts, POOL_W)], axis=1)
        pools.append(full[:, -POOL_HIST:])
        glas.append(_unpad_state(st))

    kp, vp = (jnp.transpose(a.reshape(depth, bp, SB_HEADS, SB_DIM, tp), (0, 1, 4, 2, 3)) for a in kv_t)
    return (yp.reshape(bp, tp, d), ys.reshape(bs, ts, d),
            kp, vp, jnp.stack(poolp), jnp.stack(glap),
            mk_all.reshape(depth, bp, mem_len, X_HEADS, d // X_HEADS),
            mv_all.reshape(depth, bp, mem_len, X_HEADS, d // X_HEADS),
            jnp.stack(kss), jnp.stack(vss), jnp.stack(pools), jnp.stack(glas))
```

```python
import functools

import numpy as np
import jax
import jax.numpy as jnp
from jax import lax
from jax.experimental import pallas as pl
from jax.experimental.pallas import tpu as pltpu

BF = jnp.bfloat16
F32 = jnp.float32
EPS = 1e-6

LANE = 128
VMEM_LIMIT = 56 * 1024 * 1024

SB_HEADS = 6
SB_DIM = 64
SB_W = SB_HEADS * SB_DIM
POOL_WINDOWS = (2, 4, 8, 16)
POOL_GDIM = 64
POOL_W = len(POOL_WINDOWS) * POOL_GDIM
POOL_HIST = max(POOL_WINDOWS) - 1
HIST_PAD = 16
GLA_HEADS = 4
GLA_D = 96
GLA_DP = LANE
GLA_W = GLA_HEADS * GLA_D
GLA_WP = GLA_HEADS * GLA_DP
GLA_RANK = 16
GLA_TAU = 16.0
X_HEADS = 4

C_Q = 0
C_K = C_Q + SB_W
C_V = C_K + SB_W
C_U = C_V + SB_W
C_GLA = C_U + POOL_W
C_LR = C_GLA + 4 * GLA_WP
C_GATE = C_LR + LANE


def _cparams(sem):
    return pltpu.CompilerParams(dimension_semantics=sem, vmem_limit_bytes=VMEM_LIMIT)


def _rms(x, g):
    return x * lax.rsqrt(jnp.mean(x * x, axis=-1, keepdims=True) + EPS) * g


def _softplus_neg_abs(z):
    return jnp.log(1.0 + jnp.exp(-jnp.abs(z)))


def _sigmoid(z):
    return 1.0 / (1.0 + jnp.exp(-z))


def _dot(a, b):
    return jnp.dot(a, b, preferred_element_type=F32)


def _dot_nt(a, b):
    return lax.dot_general(a, b, (((1,), (1,)), ((), ())), preferred_element_type=F32)


def _dot_tn(a, b):
    return lax.dot_general(a, b, (((0,), (0,)), ((), ())), preferred_element_type=F32)


def _const_spec(shape):
    zeros = (0,) * len(shape)
    return pl.BlockSpec(shape, lambda *_: zeros, pipeline_mode=pl.Buffered(1))


def _proj_kernel(*refs, d_model, n_alias, transposed_kv):
    x_ref, g_ref, w_ref, wa_ref, ba_ref = refs[:N_PROJ_IN]
    q_ref, k_ref, v_ref, kb_ref, vb_ref, u_ref, gla_ref, la_ref, gate_ref = refs[N_PROJ_IN + n_alias:]
    h = _rms(x_ref[...], g_ref[...]).astype(BF)

    def mm(lo, width):
        return _dot_nt(h, w_ref[lo:lo + width, :])

    q_ref[...] = (mm(C_Q, SB_W) * (SB_DIM ** -0.5)).astype(BF)
    for lo, f32_ref, bf_ref in ((C_K, k_ref, kb_ref), (C_V, v_ref, vb_ref)):
        kv = mm(lo, SB_W)
        bf_ref[...] = kv.astype(BF)
        if transposed_kv:
            f32_ref[0, 0] = kv.T
        else:
            f32_ref[...] = kv
    u_ref[...] = mm(C_U, POOL_W)
    for j in range(4):
        gla_ref[:, j * GLA_WP:(j + 1) * GLA_WP] = mm(C_GLA + j * GLA_WP, GLA_WP).astype(BF)
    xa = _dot(mm(C_LR, LANE).astype(BF), wa_ref[...]) + ba_ref[...]
    la_ref[...] = ((jnp.minimum(xa, 0.0) - _softplus_neg_abs(xa)) * (1.0 / GLA_TAU)).astype(BF)
    for j in range(3):
        gate_ref[:, j * d_model:(j + 1) * d_model] = mm(C_GATE + j * d_model, d_model).astype(BF)


N_PROJ_IN = 5


def _proj(x, g, w, wa, ba, tm, kv_stack=None):
    n, d = x.shape
    wtot = w.shape[0]
    row = lambda width: pl.BlockSpec((tm, width), lambda i: (i, 0))
    if kv_stack is None:
        kv_spec, kv_shape, extra, aliases = row(SB_W), (n, SB_W), [], {}
    else:
        layer, depth, batch, seq, bufs = kv_stack
        nt = seq // tm
        kv_spec = pl.BlockSpec((1, 1, SB_W, tm), lambda i: (layer, i // nt, 0, i % nt))
        kv_shape = (depth, batch, SB_W, seq)
        extra = [] if bufs is None else list(bufs)
        aliases = {N_PROJ_IN + j: 1 + j for j in range(len(extra))}
    outs = [
        (row(SB_W), (n, SB_W), BF), (kv_spec, kv_shape, F32), (kv_spec, kv_shape, F32),
        (row(SB_W), (n, SB_W), BF), (row(SB_W), (n, SB_W), BF), (row(POOL_W), (n, POOL_W), F32),
        (row(4 * GLA_WP), (n, 4 * GLA_WP), BF), (row(GLA_WP), (n, GLA_WP), BF),
        (row(3 * d), (n, 3 * d), BF),
    ]
    return pl.pallas_call(
        functools.partial(_proj_kernel, d_model=d, n_alias=len(extra),
                          transposed_kv=kv_stack is not None),
        grid=(n // tm,),
        in_specs=[row(d), _const_spec((1, d)), _const_spec((wtot, d)),
                  _const_spec((LANE, GLA_WP)), _const_spec((1, GLA_WP))]
                 + [pl.BlockSpec(memory_space=pl.ANY)] * len(extra),
        out_specs=[spec for spec, _, _ in outs],
        out_shape=[jax.ShapeDtypeStruct(shape, dt) for _, shape, dt in outs],
        input_output_aliases=aliases,
        compiler_params=_cparams(("parallel",)),
        name="in_proj",
    )(x, g, w, wa, ba, *extra)


def _upper_ones(n):
    r = lax.broadcasted_iota(jnp.int32, (n, n), 0)
    c = lax.broadcasted_iota(jnp.int32, (n, n), 1)
    return jnp.where(r >= c, 1.0, 0.0).astype(BF)


def _strict_causal(n):
    r = lax.broadcasted_iota(jnp.int32, (n, n), 0)
    c = lax.broadcasted_iota(jnp.int32, (n, n), 1)
    return c < r


def _sb_weights(z, carry, upper, mask):
    return _sb_finish(z, _dot(_sb_softplus(z, mask), upper), carry, mask)


def _sb_softplus(z, mask):
    zb = z.astype(BF)
    t = jnp.maximum(zb, 0.0) + _softplus_neg_abs(zb)
    return t if mask is None else jnp.where(mask, t, jnp.zeros_like(t))


def _sb_finish(z, cum, carry, mask):
    w = jnp.exp(z - cum - carry)
    if mask is not None:
        w = jnp.where(mask, w, 0.0)
    return w.astype(BF), carry + cum[:, 0:1]


SB_SKIP = "skip"
SB_LAG = 1


def _sb_prompt_kernel(q_ref, k_ref, v_ref, o_ref, acc_ref, *, blk, nq):
    first = pl.program_id(1) * nq
    pair_w = 2 * SB_DIM
    n_pairs = SB_HEADS // 2
    upper = _upper_ones(blk)
    causal = _strict_causal(blk)
    low = lax.broadcasted_iota(jnp.int32, (blk, pair_w), 1) < SB_DIM
    zero = jnp.zeros((blk, pair_w), BF)

    def split(x):
        return jnp.where(low, x, zero), jnp.where(low, zero, x)

    q_heads = []
    for s in range(nq):
        heads = []
        for p in range(n_pairs):
            heads.extend(split(q_ref[s * blk:(s + 1) * blk, p * pair_w:(p + 1) * pair_w]))
        q_heads.append(heads)
    acc_ref[...] = jnp.zeros(acc_ref.shape, F32)

    def visit(kb, carries, masks):
        start = pl.multiple_of(kb * blk, blk)
        out = list(carries)
        pairs = [slice(p * pair_w, (p + 1) * pair_w) for p in range(n_pairs)]
        live = [(s, 2 * p + e) for p in range(n_pairs) for s in range(nq)
                if masks[s] is not SB_SKIP for e in range(2)]
        k_pairs = [k_ref[pl.ds(start, blk), ps] for ps in pairs]
        v_pairs = [jnp.concatenate(split(v_ref[pl.ds(start, blk), ps]), axis=0) for ps in pairs]
        z, cum, w = {}, {}, {}
        for i in range(len(live) + 2 * SB_LAG):
            if i < len(live):
                s, h = live[i]
                z[i] = _dot_nt(q_heads[s][h], k_pairs[h // 2])
            j = i - SB_LAG
            if 0 <= j < len(live):
                cum[j] = _dot(_sb_softplus(z[j], masks[live[j][0]]), upper)
            j = i - 2 * SB_LAG
            if 0 <= j < len(live):
                s, h = live[j]
                c = s * SB_HEADS + h
                w[j], out[c] = _sb_finish(z[j], cum[j], carries[c], masks[s])
                if h % 2 == 1:
                    w_pair = jnp.concatenate([w[j - 1], w[j]], axis=1)
                    ps = pairs[h // 2]
                    acc_ref[s * blk:(s + 1) * blk, ps] += _dot(w_pair, v_pairs[h // 2])
        return tuple(out)

    carries = tuple(jnp.zeros((blk, 1), F32) for _ in range(nq * SB_HEADS))
    for d in reversed(range(nq)):
        masks = tuple(SB_SKIP if d > s else (causal if d == s else None) for s in range(nq))
        carries = visit(first + d, carries, masks)
    lax.fori_loop(0, first, lambda j, c: visit(first - 1 - j, c, (None,) * nq), carries)
    o_ref[...] = acc_ref[...].astype(BF)


def _sb_prompt(q, kb, vb, batch, seq, blk, nq):
    nblk = seq // (blk * nq)
    return pl.pallas_call(
        functools.partial(_sb_prompt_kernel, blk=blk, nq=nq),
        grid=(batch, nblk),
        in_specs=[
            pl.BlockSpec((nq * blk, SB_W), lambda b, i: (b * nblk + i, 0)),
            pl.BlockSpec((seq, SB_W), lambda b, i: (b, 0)),
            pl.BlockSpec((seq, SB_W), lambda b, i: (b, 0)),
        ],
        out_specs=pl.BlockSpec((nq * blk, SB_W), lambda b, i: (b * nblk + i, 0)),
        out_shape=jax.ShapeDtypeStruct((batch * seq, SB_W), BF),
        scratch_shapes=[pltpu.VMEM((nq * blk, SB_W), F32)],
        compiler_params=_cparams(("parallel", "arbitrary")),
        name="sb_prompt",
    )(q, kb, vb)


def _sb_sample_kernel(q_ref, kn_ref, vn_ref, kc_ref, vc_ref, o_ref, *, seq, past, blk):
    pair_w = 2 * SB_DIM
    n_pairs = SB_HEADS // 2
    pairs = [slice(p * pair_w, (p + 1) * pair_w) for p in range(n_pairs)]

    def split(x, axis):
        low = lax.broadcasted_iota(jnp.int32, x.shape, axis) < SB_DIM
        zero = jnp.zeros(x.shape, x.dtype)
        return jnp.where(low, x, zero), jnp.where(low, zero, x)

    q_heads = []
    for ps in pairs:
        q_heads.extend(split(q_ref[:, ps], 1))
    r = lax.broadcasted_iota(jnp.int32, (SB_HEADS * seq, seq), 0) % seq
    causal = lax.broadcasted_iota(jnp.int32, (SB_HEADS * seq, seq), 1) < r

    def visit(k_pairs, v_pairs, transposed, carry, acc, upper, mask):
        qk = _dot if transposed else _dot_nt
        pv = _dot_nt if transposed else _dot
        z = jnp.concatenate([qk(q_heads[h], k_pairs[h // 2]) for h in range(SB_HEADS)], axis=0)
        w, carry = _sb_weights(z, carry, upper, mask)
        out = []
        for p in range(n_pairs):
            v_lo, v_hi = split(v_pairs[p], 0 if transposed else 1)
            w_lo = w[(2 * p) * seq:(2 * p + 1) * seq]
            w_hi = w[(2 * p + 1) * seq:(2 * p + 2) * seq]
            out.append(acc[p] + pv(w_lo, v_lo) + pv(w_hi, v_hi))
        return carry, out

    carry = jnp.zeros((SB_HEADS * seq, 1), F32)
    acc = [jnp.zeros((seq, pair_w), F32) for _ in pairs]
    carry, acc = visit([kn_ref[:, ps] for ps in pairs], [vn_ref[:, ps] for ps in pairs], False,
                       carry, acc, _upper_ones(seq), causal)
    upper_past = _upper_ones(blk)
    for j in reversed(range(past // blk)):
        cols = slice(j * blk, (j + 1) * blk)
        carry, acc = visit([kc_ref[0, 0, ps, cols].astype(BF) for ps in pairs],
                           [vc_ref[0, 0, ps, cols].astype(BF) for ps in pairs], True,
                           carry, acc, upper_past, None)
    for p, ps in enumerate(pairs):
        o_ref[:, ps] = acc[p].astype(BF)


def _sb_sample(q, kb, vb, cache_k, cache_v, layer, batch, seq, blk):
    past = cache_k.shape[3]
    new = pl.BlockSpec((seq, SB_W), lambda b: (b, 0))
    cache = pl.BlockSpec((1, 1, SB_W, past), lambda b: (layer, b, 0, 0))
    return pl.pallas_call(
        functools.partial(_sb_sample_kernel, seq=seq, past=past, blk=blk),
        grid=(batch,),
        in_specs=[new, new, new, cache, cache],
        out_specs=new,
        out_shape=jax.ShapeDtypeStruct((batch * seq, SB_W), BF),
        compiler_params=_cparams(("parallel",)),
        name="sb_sample",
    )(q, kb, vb, cache_k, cache_v)


def _pool_kernel(u_ref, hist_ref, w_ref, scale_ref, o_ref, buf_ref, *, tm, pos0):
    i = pl.program_id(1)

    @pl.when(i == 0)
    def _():
        buf_ref[0:HIST_PAD, :] = hist_ref[0]

    @pl.when(i > 0)
    def _():
        buf_ref[0:HIST_PAD, :] = buf_ref[tm:tm + HIST_PAD, :]

    u = u_ref[...]
    buf_ref[HIST_PAD:HIST_PAD + tm, :] = u

    lane = lax.broadcasted_iota(jnp.int32, (tm, POOL_W), 1)
    group = lane // POOL_GDIM
    pos = pos0 + i * tm + lax.broadcasted_iota(jnp.int32, (tm, POOL_W), 0)
    run = u
    win_sum = jnp.zeros((tm, POOL_W), F32)
    width = jnp.zeros((tm, POOL_W), jnp.int32)
    for d in range(1, max(POOL_WINDOWS)):
        run = run + buf_ref[HIST_PAD - d:HIST_PAD - d + tm, :]
        if d + 1 in POOL_WINDOWS:
            g = POOL_WINDOWS.index(d + 1)
            win_sum = jnp.where(group == g, run, win_sum)
            width = jnp.where(group == g, d + 1, width)
    cnt = jnp.minimum(pos + 1, width).astype(F32)
    pooled = win_sum / cnt - u
    o_ref[...] = (_dot(pooled.astype(BF), w_ref[...]) * scale_ref[...]).astype(BF)


def _pool(u, hist, w_bd, scale, batch, seq, tm, pos0):
    nt = seq // tm
    return pl.pallas_call(
        functools.partial(_pool_kernel, tm=tm, pos0=pos0),
        grid=(batch, nt),
        in_specs=[
            pl.BlockSpec((tm, POOL_W), lambda b, i: (b * nt + i, 0)),
            pl.BlockSpec((1, HIST_PAD, POOL_W), lambda b, i: (b, 0, 0)),
            _const_spec((POOL_W, POOL_W)),
            _const_spec((1, POOL_W)),
        ],
        out_specs=pl.BlockSpec((tm, POOL_W), lambda b, i: (b * nt + i, 0)),
        out_shape=jax.ShapeDtypeStruct((batch * seq, POOL_W), BF),
        scratch_shapes=[pltpu.VMEM((tm + HIST_PAD, POOL_W), F32)],
        compiler_params=_cparams(("parallel", "arbitrary")),
        name="pool_mix",
    )(u, hist, w_bd, scale)


def _gla_levels(rows):
    out, s = [], 1
    while s < rows:
        out.append(s)
        s *= 2
    return out


def _gla_exponent_matrix(rows):
    t = np.arange(rows)[:, None]
    j = np.arange(rows)[None, :]
    blocks = [j <= t]
    for s in _gla_levels(rows)[1:]:
        start = (t // s) * s
        later = (t // s) % 2 == 1
        blocks.append(np.where(later, (j >= start) & (j <= t), (j > t) & (j <= start + s - 1)))
    return np.concatenate(blocks, axis=0).astype(np.float32)


def _gla_kernel(qkvg_ref, la_ref, em_ref, gn_ref, s0_ref, o_ref, st_ref, *, rows):
    @pl.when(pl.program_id(1) == 0)
    def _():
        st_ref[...] = s0_ref[...]

    expo = _dot(em_ref[...], la_ref[...])
    log_a = la_ref[...].astype(F32)

    q = qkvg_ref[:, 0:GLA_WP].astype(F32) * (GLA_D ** -0.5)
    k = qkvg_ref[:, GLA_WP:2 * GLA_WP].astype(F32)
    vb = qkvg_ref[:, 2 * GLA_WP:3 * GLA_WP]
    gate = qkvg_ref[:, 3 * GLA_WP:4 * GLA_WP].astype(F32)

    b = expo[0:rows]
    b_last = b[rows - 1:rows]
    q_dec = (q * jnp.exp(b)).astype(BF)
    k_dec = (k * jnp.exp(b_last - b)).astype(BF)

    row = lax.broadcasted_iota(jnp.int32, (rows, GLA_WP), 0)
    r = lax.broadcasted_iota(jnp.int32, (rows, rows), 0)
    c = lax.broadcasted_iota(jnp.int32, (rows, rows), 1)
    heads = [slice(h * GLA_DP, (h + 1) * GLA_DP) for h in range(GLA_HEADS)]

    qb = q.astype(BF)
    kb = k.astype(BF)
    att = [jnp.where(r == c, _dot_nt(qb[:, hs], kb[:, hs]), 0.0) for hs in heads]
    for n, s in enumerate(_gla_levels(rows)):
        shift = s.bit_length() - 1
        later = ((row >> shift) & 1) == 1
        rb = r >> shift
        pair = ((rb & 1) == 1) & ((c >> shift) == rb - 1)
        e = jnp.where(later, log_a, 0.0) if s == 1 else expo[n * rows:(n + 1) * rows]
        x_lvl = (jnp.where(later, q, k) * jnp.exp(e)).astype(BF)
        for h, hs in enumerate(heads):
            att[h] = att[h] + jnp.where(pair, _dot_nt(x_lvl[:, hs], x_lvl[:, hs]), 0.0)

    for h, hs in enumerate(heads):
        st = st_ref[0, h]
        o = _dot(att[h].astype(BF), vb[:, hs]) + _dot_nt(q_dec[:, hs], st.astype(BF))
        ms = jnp.sum(o * o, axis=-1, keepdims=True) * (1.0 / GLA_D)
        y = o * lax.rsqrt(ms + EPS) * gn_ref[:, hs]
        gh = gate[:, hs]
        o_ref[:, hs] = (y * (gh * _sigmoid(gh))).astype(BF)
        st_ref[0, h] = st * jnp.exp(b_last[:, hs]) + _dot_tn(vb[:, hs], k_dec[:, hs])


def _gla(qkvg, log_a, gn, s0t, batch, seq, rows):
    nt = seq // rows
    em = jnp.asarray(_gla_exponent_matrix(rows), BF)
    state = pl.BlockSpec((1, GLA_HEADS, GLA_DP, GLA_DP), lambda b, i: (b, 0, 0, 0))
    return pl.pallas_call(
        functools.partial(_gla_kernel, rows=rows),
        grid=(batch, nt),
        in_specs=[
            pl.BlockSpec((rows, 4 * GLA_WP), lambda b, i: (b * nt + i, 0)),
            pl.BlockSpec((rows, GLA_WP), lambda b, i: (b * nt + i, 0)),
            _const_spec(em.shape),
            _const_spec((1, GLA_WP)),
            state,
        ],
        out_specs=[pl.BlockSpec((rows, GLA_WP), lambda b, i: (b * nt + i, 0)), state],
        out_shape=[jax.ShapeDtypeStruct((batch * seq, GLA_WP), BF),
                   jax.ShapeDtypeStruct((batch, GLA_HEADS, GLA_DP, GLA_DP), F32)],
        compiler_params=_cparams(("parallel", "arbitrary")),
        name="gla",
    )(qkvg, log_a, em, gn, s0t)


def _merge_kernel(x_ref, oa_ref, ob_ref, oc_ref, gate_ref, wa_ref, wb_ref, wc_ref, wo_ref, g_ref,
                  y_ref, *, d_model):
    def gate(j):
        return _sigmoid(gate_ref[:, j * d_model:(j + 1) * d_model].astype(F32))

    merged = gate(0) * _dot(oa_ref[...], wa_ref[...])
    merged = merged + gate(1) * _dot(ob_ref[...], wb_ref[...])
    merged = merged + gate(2) * _dot(oc_ref[...], wc_ref[...])
    mix = _dot(merged.astype(BF), wo_ref[...])
    y_ref[...] = x_ref[...] + _rms(mix, g_ref[...])


def _merge(x, oa, ob, oc, gates, wa, wb, wc, wo, g, tm):
    n, d = x.shape
    row = lambda width: pl.BlockSpec((tm, width), lambda i: (i, 0))
    return pl.pallas_call(
        functools.partial(_merge_kernel, d_model=d),
        grid=(n // tm,),
        in_specs=[row(d), row(SB_W), row(POOL_W), row(GLA_WP), row(3 * d),
                  _const_spec(wa.shape), _const_spec(wb.shape), _const_spec(wc.shape),
                  _const_spec(wo.shape), _const_spec((1, d))],
        out_specs=row(d),
        out_shape=jax.ShapeDtypeStruct((n, d), F32),
        compiler_params=_cparams(("parallel",)),
        name="merge_out",
    )(x, oa, ob, oc, gates, wa, wb, wc, wo, g)


def _xattn_kernel(x_ref, mk_ref, mv_ref, wq_ref, wo_ref, gpre_ref, gpost_ref, y_ref, *, d_model):
    x = x_ref[...]
    q = _dot(_rms(x, gpre_ref[...]).astype(BF), wq_ref[...]).astype(BF)
    hd = d_model // X_HEADS
    heads = [slice(h * hd, (h + 1) * hd) for h in range(X_HEADS)]
    s, o = {}, {}
    for i in range(X_HEADS + 1):
        if i < X_HEADS:
            s[i] = _dot_nt(q[:, heads[i]], mk_ref[0, :, heads[i]].astype(BF)) * (hd ** -0.5)
        j = i - 1
        if 0 <= j < X_HEADS:
            p = jnp.exp(s[j] - jnp.max(s[j], axis=-1, keepdims=True))
            p = p / jnp.sum(p, axis=-1, keepdims=True)
            o[j] = _dot(p.astype(BF), mv_ref[0, :, heads[j]].astype(BF)).astype(BF)
    attn = jnp.concatenate([o[h] for h in range(X_HEADS)], axis=-1)
    y_ref[...] = x + _rms(_dot(attn, wo_ref[...]), gpost_ref[...])


def _xattn(x, mk, mv, mem_index, wq, wo, gpre, gpost, batch, seq, tm):
    n, d = x.shape
    nt = seq // tm
    mem_len = mk.shape[-2]
    row = pl.BlockSpec((tm, d), lambda b, i: (b * nt + i, 0))
    mem = pl.BlockSpec((1, mem_len, d), lambda b, i: (mem_index(b), 0, 0))
    return pl.pallas_call(
        functools.partial(_xattn_kernel, d_model=d),
        grid=(batch, nt),
        in_specs=[row, mem, mem, _const_spec((d, d)), _const_spec((d, d)),
                  _const_spec((1, d)), _const_spec((1, d))],
        out_specs=row,
        out_shape=jax.ShapeDtypeStruct((n, d), F32),
        compiler_params=_cparams(("parallel", "parallel")),
        name="cross_attn",
    )(x, mk, mv, wq, wo, gpre, gpost)


def _ffn_kernel(x_ref, wg_ref, wu_ref, wo_ref, gpre_ref, gpost_ref, y_ref, *, chunk):
    x = x_ref[...]
    h = _rms(x, gpre_ref[...]).astype(BF)
    ff = wg_ref.shape[1]
    acc = jnp.zeros(x.shape, F32)
    for j in range(ff // chunk):
        cs = slice(j * chunk, (j + 1) * chunk)
        g = _dot(h, wg_ref[:, cs])
        u = _dot(h, wu_ref[:, cs])
        act = (g * _sigmoid(g) * u).astype(BF)
        acc = acc + _dot(act, wo_ref[cs, :])
    y_ref[...] = x + _rms(acc, gpost_ref[...])


def _ffn(x, wg, wu, wo, gpre, gpost, tm, chunk):
    n, d = x.shape
    row = pl.BlockSpec((tm, d), lambda i: (i, 0))
    return pl.pallas_call(
        functools.partial(_ffn_kernel, chunk=chunk),
        grid=(n // tm,),
        in_specs=[row, _const_spec(wg.shape), _const_spec(wu.shape), _const_spec(wo.shape),
                  _const_spec((1, d)), _const_spec((1, d))],
        out_specs=row,
        out_shape=jax.ShapeDtypeStruct((n, d), F32),
        compiler_params=_cparams(("parallel",)),
        name="ffn",
    )(x, wg, wu, wo, gpre, gpost)


def _memkv_kernel(m_ref, g_ref, wk_ref, wv_ref, k_ref, v_ref):
    m = _rms(m_ref[...], g_ref[0]).astype(BF)
    k_ref[0] = _dot(m, wk_ref[0].astype(BF))
    v_ref[0] = _dot(m, wv_ref[0].astype(BF))


def _memkv(mem, g, wk, wv, tm):
    depth, d, _ = wk.shape
    n = mem.shape[0]
    w = pl.BlockSpec((1, d, d), lambda l, i: (l, 0, 0))
    out = pl.BlockSpec((1, tm, d), lambda l, i: (l, i, 0))
    return pl.pallas_call(
        _memkv_kernel,
        grid=(depth, n // tm),
        in_specs=[pl.BlockSpec((tm, d), lambda l, i: (i, 0)),
                  pl.BlockSpec((1, 1, d), lambda l, i: (l, 0, 0)), w, w],
        out_specs=[out, out],
        out_shape=[jax.ShapeDtypeStruct((depth, n, d), F32)] * 2,
        compiler_params=_cparams(("parallel", "parallel")),
        name="mem_kv",
    )(mem, g, wk, wv)


def _pad_heads(w, axis):
    shape = w.shape
    w = w.reshape(shape[:axis] + (GLA_HEADS, GLA_D) + shape[axis + 1:])
    pad = [(0, 0)] * w.ndim
    pad[axis + 1] = (0, GLA_DP - GLA_D)
    w = jnp.pad(w, pad)
    return w.reshape(shape[:axis] + (GLA_WP,) + shape[axis + 1:])


def _pack_layer(p, l):
    w_in = jnp.swapaxes(p['w_in'][l], 0, 1)
    o = 3 * SB_W + POOL_W
    sb_u = w_in[:o]
    gla = [_pad_heads(w_in[o + j * GLA_W:o + (j + 1) * GLA_W], 0) for j in range(4)]
    o += 4 * GLA_W
    lr = jnp.pad(w_in[o:o + GLA_RANK], ((0, LANE - GLA_RANK), (0, 0)))
    gates = w_in[o + GLA_RANK:]
    eye = jnp.eye(len(POOL_WINDOWS), dtype=F32)
    w_pool_bd = (eye[:, None, :, None] * p['w_pool'][l][:, :, None, :]).reshape(POOL_W, POOL_W)
    ff = p['w_ffn_out'].shape[1]
    row = lambda v: v.reshape(1, -1)
    return dict(
        w_in=jnp.concatenate([sb_u] + gla + [lr, gates], axis=0).astype(BF),
        w_a2=jnp.pad(_pad_heads(p['w_gla_a2'][l], 1), ((0, LANE - GLA_RANK), (0, 0))).astype(BF),
        b_a=row(_pad_heads(p['b_gla_a'][l], 0)),
        gla_norm=row(_pad_heads(p['gla_norm'][l], 0)),
        w_pool=w_pool_bd.astype(BF),
        pool_scale=row(p['pool_scale'][l]),
        w_a=p['w_branch_a'][l].astype(BF),
        w_b=p['w_branch_b'][l].astype(BF),
        w_c=_pad_heads(p['w_branch_c'][l], 0).astype(BF),
        w_mix=p['w_mix_out'][l].astype(BF),
        w_xq=p['w_xq'][l].astype(BF),
        w_xo=p['w_xo'][l].astype(BF),
        w_g=p['w_ffn_in'][l][:, :ff].astype(BF),
        w_u=p['w_ffn_in'][l][:, ff:].astype(BF),
        w_o=p['w_ffn_out'][l].astype(BF),
        n_mix_pre=row(p['norm_mix_pre'][l]), n_mix_post=row(p['norm_mix_post'][l]),
        n_x_pre=row(p['norm_x_pre'][l]), n_x_post=row(p['norm_x_post'][l]),
        n_ffn_pre=row(p['norm_ffn_pre'][l]), n_ffn_post=row(p['norm_ffn_post'][l]),
    )


def _pad_state(s):
    s = jnp.swapaxes(s, -1, -2)
    return jnp.pad(s, ((0, 0), (0, 0), (0, GLA_DP - GLA_D), (0, GLA_DP - GLA_D)))


def _unpad_state(st):
    return jnp.swapaxes(st[:, :, :GLA_D, :GLA_D], -1, -2)


def _tiles(batch, seq):
    n = batch * seq
    return dict(
        proj=min(512, n),
        sb=min(256, seq),
        pool=min(256, seq),
        gla=min(128, seq),
        merge=min(512, n),
        xattn=min(512, seq),
        ffn=min(512, n),
    )


def _layer(x, w, batch, seq, sb_attend, pool_hist, pos0, s0t, mk, mv, mem_index, kv_stack=None):
    t = _tiles(batch, seq)
    q, k, v, kb, vb, u, qkvg, log_a, gates = _proj(x, w['n_mix_pre'], w['w_in'], w['w_a2'], w['b_a'],
                                                   t['proj'], kv_stack)
    oa = sb_attend(q, kb, vb)
    ob = _pool(u, pool_hist, w['w_pool'], w['pool_scale'], batch, seq, t['pool'], pos0)
    oc, st = _gla(qkvg, log_a, w['gla_norm'], s0t, batch, seq, t['gla'])
    x = _merge(x, oa, ob, oc, gates, w['w_a'], w['w_b'], w['w_c'], w['w_mix'], w['n_mix_post'],
               t['merge'])
    x = _xattn(x, mk, mv, mem_index, w['w_xq'], w['w_xo'], w['n_x_pre'], w['n_x_post'],
               batch, seq, t['xattn'])
    x = _ffn(x, w['w_g'], w['w_u'], w['w_o'], w['n_ffn_pre'], w['n_ffn_post'], t['ffn'], 256)
    return x, k, v, u, st


def kernel(x_prompt, x_sample, mem_prompt, cache_sb_k, cache_sb_v, state_pool, state_gla, cache_mem_k, cache_mem_v, w_in, w_gla_a2, b_gla_a, gla_norm, w_pool, pool_scale, w_branch_a, w_branch_b, w_branch_c, w_mix_out, mem_norm, w_xq, w_xk, w_xv, w_xo, w_ffn_in, w_ffn_out, norm_mix_pre, norm_mix_post, norm_x_pre, norm_x_post, norm_ffn_pre, norm_ffn_post):
    p = dict(w_in=w_in, w_gla_a2=w_gla_a2, b_gla_a=b_gla_a, gla_norm=gla_norm, w_pool=w_pool,
             pool_scale=pool_scale, w_branch_a=w_branch_a, w_branch_b=w_branch_b,
             w_branch_c=w_branch_c, w_mix_out=w_mix_out, w_xq=w_xq, w_xo=w_xo,
             w_ffn_in=w_ffn_in, w_ffn_out=w_ffn_out, norm_mix_pre=norm_mix_pre,
             norm_mix_post=norm_mix_post, norm_x_pre=norm_x_pre, norm_x_post=norm_x_post,
             norm_ffn_pre=norm_ffn_pre, norm_ffn_post=norm_ffn_post)
    depth = w_in.shape[0]
    bp, tp, d = x_prompt.shape
    bs, ts, _ = x_sample.shape
    mem_len = mem_prompt.shape[1]
    past = cache_sb_k.shape[2]

    mk_all, mv_all = _memkv(mem_prompt.reshape(bp * mem_len, d), mem_norm.reshape(depth, 1, d),
                            w_xk, w_xv, min(512, bp * mem_len))
    mk_p = mk_all.reshape(depth * bp, mem_len, d)
    mv_p = mv_all.reshape(depth * bp, mem_len, d)
    mk_s = cache_mem_k.reshape(depth * bs, mem_len, d)
    mv_s = cache_mem_v.reshape(depth * bs, mem_len, d)
    ck = jnp.transpose(cache_sb_k, (0, 1, 3, 4, 2)).reshape(depth, bs, SB_W, past)
    cv = jnp.transpose(cache_sb_v, (0, 1, 3, 4, 2)).reshape(depth, bs, SB_W, past)

    yp = x_prompt.reshape(bp * tp, d)
    ys = x_sample.reshape(bs * ts, d)
    hist_zero = jnp.zeros((bp, HIST_PAD, POOL_W), F32)
    state_zero = jnp.zeros((bp, GLA_HEADS, GLA_DP, GLA_DP), F32)
    tiles_p = _tiles(bp, tp)
    kv_t = None
    poolp, glap, kss, vss, pools, glas = ([] for _ in range(6))
    for l in range(depth):
        w = _pack_layer(p, l)
        yp, k, v, u, st = _layer(
            yp, w, bp, tp,
            functools.partial(_sb_prompt, batch=bp, seq=tp, blk=tiles_p['sb'],
                              nq=2 if tp % (2 * tiles_p['sb']) == 0 else 1),
            hist_zero, 0, state_zero, mk_p, mv_p, lambda b, l=l: l * bp + b,
            kv_stack=(l, depth, bp, tp, kv_t))
        kv_t = (k, v)
        poolp.append(u.reshape(bp, tp, POOL_W)[:, tp - POOL_HIST:])
        glap.append(_unpad_state(st))

        hist = jnp.pad(state_pool[l], ((0, 0), (HIST_PAD - POOL_HIST, 0), (0, 0)))
        ys, k, v, u, st = _layer(
            ys, w, bs, ts,
            functools.partial(_sb_sample, cache_k=ck, cache_v=cv, layer=l, batch=bs, seq=ts,
                              blk=min(256, past)),
            hist, past, _pad_state(state_gla[l]), mk_s, mv_s, lambda b, l=l: l * bs + b)
        kss.append(k.reshape(bs, ts, SB_HEADS, SB_DIM))
        vss.append(v.reshape(bs, ts, SB_HEADS, SB_DIM))
        full = jnp.concatenate([state_pool[l], u.reshape(bs, ts, POOL_W)], axis=1)
        pools.append(full[:, -POOL_HIST:])
        glas.append(_unpad_state(st))

    kp, vp = (jnp.transpose(a.reshape(depth, bp, SB_HEADS, SB_DIM, tp), (0, 1, 4, 2, 3)) for a in kv_t)
    return (yp.reshape(bp, tp, d), ys.reshape(bs, ts, d),
            kp, vp, jnp.stack(poolp), jnp.stack(glap),
            mk_all.reshape(depth, bp, mem_len, X_HEADS, d // X_HEADS),
            mv_all.reshape(depth, bp, mem_len, X_HEADS, d // X_HEADS),
            jnp.stack(kss), jnp.stack(vss), jnp.stack(pools), jnp.stack(glas))
```

```python
import functools

import numpy as np
import jax
import jax.numpy as jnp
from jax import lax
from jax.experimental import pallas as pl
from jax.experimental.pallas import tpu as pltpu

BF = jnp.bfloat16
F32 = jnp.float32
EPS = 1e-6

LANE = 128
VMEM_LIMIT = 56 * 1024 * 1024

SB_HEADS = 6
SB_DIM = 64
SB_W = SB_HEADS * SB_DIM
POOL_WINDOWS = (2, 4, 8, 16)
POOL_GDIM = 64
POOL_W = len(POOL_WINDOWS) * POOL_GDIM
POOL_HIST = max(POOL_WINDOWS) - 1
HIST_PAD = 16
assert all(w & (w - 1) == 0 for w in POOL_WINDOWS) and 2 * POOL_GDIM == LANE and POOL_HIST <= HIST_PAD
GLA_HEADS = 4
GLA_D = 96
GLA_DP = LANE
GLA_W = GLA_HEADS * GLA_D
GLA_WP = GLA_HEADS * GLA_DP
GLA_RANK = 16
GLA_TAU = 16.0
X_HEADS = 4
XATTN_SHORT_SEQ = 128

C_Q = 0
C_K = C_Q + SB_W
C_V = C_K + SB_W
C_U = C_V + SB_W
C_GLA = C_U + POOL_W
C_LR = C_GLA + 4 * GLA_WP
C_GATE = C_LR + LANE


def _cparams(sem):
    return pltpu.CompilerParams(dimension_semantics=sem, vmem_limit_bytes=VMEM_LIMIT)


def _rms(x, g):
    return x * lax.rsqrt(jnp.mean(x * x, axis=-1, keepdims=True) + EPS) * g


def _softplus_neg_abs(z):
    return jnp.log(1.0 + jnp.exp(-jnp.abs(z)))


def _sigmoid(z):
    return 0.5 * jnp.tanh(0.5 * z) + 0.5


def _dot(a, b):
    return jnp.dot(a, b, preferred_element_type=F32)


def _dot_nt(a, b):
    return lax.dot_general(a, b, (((1,), (1,)), ((), ())), preferred_element_type=F32)


def _dot_tn(a, b):
    return lax.dot_general(a, b, (((0,), (0,)), ((), ())), preferred_element_type=F32)


def _const_spec(shape):
    zeros = (0,) * len(shape)
    return pl.BlockSpec(shape, lambda *_: zeros, pipeline_mode=pl.Buffered(1))


def _proj_kernel(*refs, d_model, n_alias, transposed_kv):
    x_ref, g_ref, w_ref, wa_ref, ba_ref = refs[:N_PROJ_IN]
    q_ref, k_ref, v_ref, kb_ref, vb_ref, u_ref, gla_ref, la_ref, gate_ref = refs[N_PROJ_IN + n_alias:]
    h = _rms(x_ref[...], g_ref[...]).astype(BF)

    def mm(lo, width):
        return _dot_nt(h, w_ref[lo:lo + width, :])

    q_ref[...] = (mm(C_Q, SB_W) * (SB_DIM ** -0.5)).astype(BF)
    for lo, f32_ref, bf_ref in ((C_K, k_ref, kb_ref), (C_V, v_ref, vb_ref)):
        kv = mm(lo, SB_W)
        bf_ref[...] = kv.astype(BF)
        if transposed_kv is None:
            f32_ref[...] = kv
        else:
            for l in range(f32_ref.shape[0]):
                f32_ref[l, 0] = kv.T if l == transposed_kv else jnp.zeros((SB_W, kv.shape[0]), F32)
    u_ref[...] = mm(C_U, POOL_W)
    for j in range(4):
        gla_ref[:, j * GLA_WP:(j + 1) * GLA_WP] = mm(C_GLA + j * GLA_WP, GLA_WP).astype(BF)
    xa = _dot(mm(C_LR, LANE).astype(BF), wa_ref[...]) + ba_ref[...]
    la_ref[...] = ((jnp.minimum(xa, 0.0) - _softplus_neg_abs(xa)) * (1.0 / GLA_TAU)).astype(BF)
    for j in range(3):
        gate_ref[:, j * d_model:(j + 1) * d_model] = mm(C_GATE + j * d_model, d_model).astype(BF)


N_PROJ_IN = 5


def _proj(x, g, w, wa, ba, tm, kv_stack=None):
    n, d = x.shape
    wtot = w.shape[0]
    row = lambda width: pl.BlockSpec((tm, width), lambda i: (i, 0))
    if kv_stack is None:
        kv_spec, kv_shape, extra, aliases = row(SB_W), (n, SB_W), [], {}
    else:
        layer, depth, batch, seq, bufs = kv_stack
        nt = seq // tm
        kv_shape = (depth, batch, SB_W, seq)
        if bufs is None:
            kv_spec = pl.BlockSpec((depth, 1, SB_W, tm), lambda i: (0, i // nt, 0, i % nt))
            kv_slot = layer
        else:
            kv_spec = pl.BlockSpec((1, 1, SB_W, tm), lambda i: (layer, i // nt, 0, i % nt))
            kv_slot = 0
        extra = [] if bufs is None else list(bufs)
        aliases = {N_PROJ_IN + j: 1 + j for j in range(len(extra))}
    outs = [
        (row(SB_W), (n, SB_W), BF), (kv_spec, kv_shape, F32), (kv_spec, kv_shape, F32),
        (row(SB_W), (n, SB_W), BF), (row(SB_W), (n, SB_W), BF), (row(POOL_W), (n, POOL_W), F32),
        (row(4 * GLA_WP), (n, 4 * GLA_WP), BF), (row(GLA_WP), (n, GLA_WP), BF),
        (row(3 * d), (n, 3 * d), BF),
    ]
    return pl.pallas_call(
        functools.partial(_proj_kernel, d_model=d, n_alias=len(extra),
                          transposed_kv=None if kv_stack is None else kv_slot),
        grid=(n // tm,),
        in_specs=[row(d), _const_spec((1, d)), _const_spec((wtot, d)),
                  _const_spec((LANE, GLA_WP)), _const_spec((1, GLA_WP))]
                 + [pl.BlockSpec(memory_space=pl.ANY)] * len(extra),
        out_specs=[spec for spec, _, _ in outs],
        out_shape=[jax.ShapeDtypeStruct(shape, dt) for _, shape, dt in outs],
        input_output_aliases=aliases,
        compiler_params=_cparams(("parallel",)),
        name="in_proj",
    )(x, g, w, wa, ba, *extra)


def _upper_ones(n):
    r = lax.broadcasted_iota(jnp.int32, (n, n), 0)
    c = lax.broadcasted_iota(jnp.int32, (n, n), 1)
    return jnp.where(r >= c, 1.0, 0.0).astype(BF)


def _strict_causal(n):
    r = lax.broadcasted_iota(jnp.int32, (n, n), 0)
    c = lax.broadcasted_iota(jnp.int32, (n, n), 1)
    return c < r


def _sb_weights(z, carry, upper, mask):
    return _sb_finish(z, _dot(_sb_softplus(z, mask), upper), carry, mask)


def _sb_softplus(z, mask):
    zb = z.astype(BF)
    t = jnp.maximum(zb, 0.0) + _softplus_neg_abs(zb)
    return t if mask is None else jnp.where(mask, t, jnp.zeros_like(t))


def _sb_finish(z, cum, carry, mask):
    w = jnp.exp(z - cum - carry)
    if mask is not None:
        w = jnp.where(mask, w, 0.0)
    return w.astype(BF), carry + cum[:, 0:1]


SB_SKIP = "skip"
SB_LAG = 1


def _sb_prompt_kernel(q_ref, k_ref, v_ref, o_ref, acc_ref, *, blk, nq):
    first = pl.program_id(1) * nq
    pair_w = 2 * SB_DIM
    n_pairs = SB_HEADS // 2
    upper = _upper_ones(blk)
    causal = _strict_causal(blk)
    low = lax.broadcasted_iota(jnp.int32, (blk, pair_w), 1) < SB_DIM
    zero = jnp.zeros((blk, pair_w), BF)

    def split(x):
        return jnp.where(low, x, zero), jnp.where(low, zero, x)

    q_heads = []
    for s in range(nq):
        heads = []
        for p in range(n_pairs):
            heads.extend(split(q_ref[s * blk:(s + 1) * blk, p * pair_w:(p + 1) * pair_w]))
        q_heads.append(heads)
    acc_ref[...] = jnp.zeros(acc_ref.shape, F32)

    def visit(kb, carries, masks):
        start = pl.multiple_of(kb * blk, blk)
        out = list(carries)
        pairs = [slice(p * pair_w, (p + 1) * pair_w) for p in range(n_pairs)]
        live = [(s, 2 * p + e) for p in range(n_pairs) for s in range(nq)
                if masks[s] is not SB_SKIP for e in range(2)]
        k_pairs = [k_ref[pl.ds(start, blk), ps] for ps in pairs]
        v_pairs = [jnp.concatenate(split(v_ref[pl.ds(start, blk), ps]), axis=0) for ps in pairs]
        z, cum, w = {}, {}, {}
        for i in range(len(live) + 2 * SB_LAG):
            if i < len(live):
                s, h = live[i]
                z[i] = _dot_nt(q_heads[s][h], k_pairs[h // 2])
            j = i - SB_LAG
            if 0 <= j < len(live):
                cum[j] = _dot(_sb_softplus(z[j], masks[live[j][0]]), upper)
            j = i - 2 * SB_LAG
            if 0 <= j < len(live):
                s, h = live[j]
                c = s * SB_HEADS + h
                w[j], out[c] = _sb_finish(z[j], cum[j], carries[c], masks[s])
                if h % 2 == 1:
                    w_pair = jnp.concatenate([w[j - 1], w[j]], axis=1)
                    ps = pairs[h // 2]
                    acc_ref[s * blk:(s + 1) * blk, ps] += _dot(w_pair, v_pairs[h // 2])
        return tuple(out)

    carries = tuple(jnp.zeros((blk, 1), F32) for _ in range(nq * SB_HEADS))
    for d in reversed(range(nq)):
        masks = tuple(SB_SKIP if d > s else (causal if d == s else None) for s in range(nq))
        carries = visit(first + d, carries, masks)
    lax.fori_loop(0, first, lambda j, c: visit(first - 1 - j, c, (None,) * nq), carries)
    o_ref[...] = acc_ref[...].astype(BF)


def _sb_prompt(q, kb, vb, batch, seq, blk, nq):
    nblk = seq // (blk * nq)
    return pl.pallas_call(
        functools.partial(_sb_prompt_kernel, blk=blk, nq=nq),
        grid=(batch, nblk),
        in_specs=[
            pl.BlockSpec((nq * blk, SB_W), lambda b, i: (b * nblk + i, 0)),
            pl.BlockSpec((seq, SB_W), lambda b, i: (b, 0)),
            pl.BlockSpec((seq, SB_W), lambda b, i: (b, 0)),
        ],
        out_specs=pl.BlockSpec((nq * blk, SB_W), lambda b, i: (b * nblk + i, 0)),
        out_shape=jax.ShapeDtypeStruct((batch * seq, SB_W), BF),
        scratch_shapes=[pltpu.VMEM((nq * blk, SB_W), F32)],
        compiler_params=_cparams(("parallel", "arbitrary")),
        name="sb_prompt",
    )(q, kb, vb)


def _sb_sample_kernel(q_ref, kn_ref, vn_ref, kc_ref, vc_ref, o_ref, *, seq, past, blk):
    pair_w = 2 * SB_DIM
    n_pairs = SB_HEADS // 2
    pairs = [slice(p * pair_w, (p + 1) * pair_w) for p in range(n_pairs)]

    def split(x, axis):
        low = lax.broadcasted_iota(jnp.int32, x.shape, axis) < SB_DIM
        zero = jnp.zeros(x.shape, x.dtype)
        return jnp.where(low, x, zero), jnp.where(low, zero, x)

    q_heads = []
    for ps in pairs:
        q_heads.extend(split(q_ref[:, ps], 1))
    r = lax.broadcasted_iota(jnp.int32, (SB_HEADS * seq, seq), 0) % seq
    causal = lax.broadcasted_iota(jnp.int32, (SB_HEADS * seq, seq), 1) < r

    def visit(k_pairs, v_pairs, transposed, carry, acc, upper, mask):
        qk = _dot if transposed else _dot_nt
        pv = _dot_nt if transposed else _dot
        z = jnp.concatenate([qk(q_heads[h], k_pairs[h // 2]) for h in range(SB_HEADS)], axis=0)
        w, carry = _sb_weights(z, carry, upper, mask)
        out = []
        for p in range(n_pairs):
            v_lo, v_hi = split(v_pairs[p], 0 if transposed else 1)
            w_lo = w[(2 * p) * seq:(2 * p + 1) * seq]
            w_hi = w[(2 * p + 1) * seq:(2 * p + 2) * seq]
            out.append(acc[p] + pv(w_lo, v_lo) + pv(w_hi, v_hi))
        return carry, out

    carry = jnp.zeros((SB_HEADS * seq, 1), F32)
    acc = [jnp.zeros((seq, pair_w), F32) for _ in pairs]
    carry, acc = visit([kn_ref[:, ps] for ps in pairs], [vn_ref[:, ps] for ps in pairs], False,
                       carry, acc, _upper_ones(seq), causal)
    upper_past = _upper_ones(blk)
    for j in reversed(range(past // blk)):
        cols = slice(j * blk, (j + 1) * blk)
        carry, acc = visit([kc_ref[0, 0, ps, cols].astype(BF) for ps in pairs],
                           [vc_ref[0, 0, ps, cols].astype(BF) for ps in pairs], True,
                           carry, acc, upper_past, None)
    for p, ps in enumerate(pairs):
        o_ref[:, ps] = acc[p].astype(BF)


def _sb_sample(q, kb, vb, cache_k, cache_v, layer, batch, seq, blk):
    past = cache_k.shape[3]
    new = pl.BlockSpec((seq, SB_W), lambda b: (b, 0))
    cache = pl.BlockSpec((1, 1, SB_W, past), lambda b: (layer, b, 0, 0))
    return pl.pallas_call(
        functools.partial(_sb_sample_kernel, seq=seq, past=past, blk=blk),
        grid=(batch,),
        in_specs=[new, new, new, cache, cache],
        out_specs=new,
        out_shape=jax.ShapeDtypeStruct((batch * seq, SB_W), BF),
        compiler_params=_cparams(("parallel",)),
        name="sb_sample",
    )(q, kb, vb, cache_k, cache_v)


def _pool_kernel(u_ref, hist_ref, w_ref, scale_ref, o_ref, buf_ref, *, tm, pos0):
    i = pl.program_id(1)

    @pl.when(i == 0)
    def _():
        buf_ref[0:HIST_PAD, :] = hist_ref[0]

    @pl.when(i > 0)
    def _():
        buf_ref[0:HIST_PAD, :] = buf_ref[tm:tm + HIST_PAD, :]

    u = u_ref[...]
    buf_ref[HIST_PAD:HIST_PAD + tm, :] = u

    rows = tm + HIST_PAD
    low = lax.broadcasted_iota(jnp.int32, (tm, LANE), 1) < POOL_GDIM
    pos1 = pos0 + i * tm + lax.broadcasted_iota(jnp.int32, (tm, LANE), 0) + 1
    pooled = []
    for tile in range(POOL_W // LANE):
        cols = slice(tile * LANE, (tile + 1) * LANE)
        w_lo, w_hi = POOL_WINDOWS[2 * tile], POOL_WINDOWS[2 * tile + 1]
        s = buf_ref[0:rows, cols]
        sums, w = {}, 1
        while w < w_hi:
            s = s + pltpu.roll(s, w, 0)
            w *= 2
            sums[w] = s
        win_sum = jnp.where(low, sums[w_lo][HIST_PAD:], sums[w_hi][HIST_PAD:])
        cnt = jnp.minimum(pos1, jnp.where(low, w_lo, w_hi)).astype(F32)
        pooled.append(win_sum / cnt - u[:, cols])
    pooled = jnp.concatenate(pooled, axis=1)
    o_ref[...] = (_dot(pooled.astype(BF), w_ref[...]) * scale_ref[...]).astype(BF)


def _pool(u, hist, w_bd, scale, batch, seq, tm, pos0):
    nt = seq // tm
    return pl.pallas_call(
        functools.partial(_pool_kernel, tm=tm, pos0=pos0),
        grid=(batch, nt),
        in_specs=[
            pl.BlockSpec((tm, POOL_W), lambda b, i: (b * nt + i, 0)),
            pl.BlockSpec((1, HIST_PAD, POOL_W), lambda b, i: (b, 0, 0)),
            _const_spec((POOL_W, POOL_W)),
            _const_spec((1, POOL_W)),
        ],
        out_specs=pl.BlockSpec((tm, POOL_W), lambda b, i: (b * nt + i, 0)),
        out_shape=jax.ShapeDtypeStruct((batch * seq, POOL_W), BF),
        scratch_shapes=[pltpu.VMEM((tm + HIST_PAD, POOL_W), F32)],
        compiler_params=_cparams(("parallel", "arbitrary")),
        name="pool_mix",
    )(u, hist, w_bd, scale)


def _gla_levels(rows):
    out, s = [], 1
    while s < rows:
        out.append(s)
        s *= 2
    return out


def _gla_exponent_matrix(rows):
    t = np.arange(rows)[:, None]
    j = np.arange(rows)[None, :]
    blocks = [j <= t]
    for s in _gla_levels(rows)[1:]:
        start = (t // s) * s
        later = (t // s) % 2 == 1
        blocks.append(np.where(later, (j >= start) & (j <= t), (j > t) & (j <= start + s - 1)))
    return np.concatenate(blocks, axis=0).astype(np.float32)


def _gla_kernel(qkvg_ref, la_ref, em_ref, gn_ref, s0_ref, o_ref, st_ref, *, rows):
    @pl.when(pl.program_id(1) == 0)
    def _():
        st_ref[...] = s0_ref[...]

    expo = _dot(em_ref[...], la_ref[...])
    log_a = la_ref[...].astype(F32)

    q = qkvg_ref[:, 0:GLA_WP].astype(F32) * (GLA_D ** -0.5)
    k = qkvg_ref[:, GLA_WP:2 * GLA_WP].astype(F32)
    vb = qkvg_ref[:, 2 * GLA_WP:3 * GLA_WP]
    gate = qkvg_ref[:, 3 * GLA_WP:4 * GLA_WP].astype(F32)

    b = expo[0:rows]
    b_last = b[rows - 1:rows]
    q_dec = (q * jnp.exp(b)).astype(BF)
    k_dec = (k * jnp.exp(b_last - b)).astype(BF)

    row = lax.broadcasted_iota(jnp.int32, (rows, GLA_WP), 0)
    r = lax.broadcasted_iota(jnp.int32, (rows, rows), 0)
    c = lax.broadcasted_iota(jnp.int32, (rows, rows), 1)
    heads = [slice(h * GLA_DP, (h + 1) * GLA_DP) for h in range(GLA_HEADS)]

    qb = q.astype(BF)
    kb = k.astype(BF)
    att = [jnp.where(r == c, _dot_nt(qb[:, hs], kb[:, hs]), 0.0) for hs in heads]
    for n, s in enumerate(_gla_levels(rows)):
        shift = s.bit_length() - 1
        later = ((row >> shift) & 1) == 1
        rb = r >> shift
        pair = ((rb & 1) == 1) & ((c >> shift) == rb - 1)
        e = jnp.where(later, log_a, 0.0) if s == 1 else expo[n * rows:(n + 1) * rows]
        x_lvl = (jnp.where(later, q, k) * jnp.exp(e)).astype(BF)
        for h, hs in enumerate(heads):
            att[h] = jnp.where(pair, _dot_nt(x_lvl[:, hs], x_lvl[:, hs]), att[h])

    for h, hs in enumerate(heads):
        st = st_ref[0, h]
        o = _dot(att[h].astype(BF), vb[:, hs]) + _dot_nt(q_dec[:, hs], st.astype(BF))
        ms = jnp.sum(o * o, axis=-1, keepdims=True) * (1.0 / GLA_D)
        y = o * lax.rsqrt(ms + EPS) * gn_ref[:, hs]
        gh = gate[:, hs]
        o_ref[:, hs] = (y * (gh * _sigmoid(gh))).astype(BF)
        st_ref[0, h] = st * jnp.exp(b_last[:, hs]) + _dot_tn(vb[:, hs], k_dec[:, hs])


def _gla(qkvg, log_a, gn, s0t, batch, seq, rows):
    nt = seq // rows
    em = jnp.asarray(_gla_exponent_matrix(rows), BF)
    state = pl.BlockSpec((1, GLA_HEADS, GLA_DP, GLA_DP), lambda b, i: (b, 0, 0, 0))
    return pl.pallas_call(
        functools.partial(_gla_kernel, rows=rows),
        grid=(batch, nt),
        in_specs=[
            pl.BlockSpec((rows, 4 * GLA_WP), lambda b, i: (b * nt + i, 0)),
            pl.BlockSpec((rows, GLA_WP), lambda b, i: (b * nt + i, 0)),
            _const_spec(em.shape),
            _const_spec((1, GLA_WP)),
            state,
        ],
        out_specs=[pl.BlockSpec((rows, GLA_WP), lambda b, i: (b * nt + i, 0)), state],
        out_shape=[jax.ShapeDtypeStruct((batch * seq, GLA_WP), BF),
                   jax.ShapeDtypeStruct((batch, GLA_HEADS, GLA_DP, GLA_DP), F32)],
        compiler_params=_cparams(("parallel", "arbitrary")),
        name="gla",
    )(qkvg, log_a, em, gn, s0t)


def _merge_kernel(x_ref, oa_ref, ob_ref, oc_ref, gate_ref, wa_ref, wb_ref, wc_ref, wo_ref, g_ref,
                  y_ref, *, d_model):
    def gate(j):
        return _sigmoid(gate_ref[:, j * d_model:(j + 1) * d_model].astype(F32))

    merged = gate(0) * _dot(oa_ref[...], wa_ref[...])
    merged = merged + gate(1) * _dot(ob_ref[...], wb_ref[...])
    merged = merged + gate(2) * _dot(oc_ref[...], wc_ref[...])
    mix = _dot(merged.astype(BF), wo_ref[...])
    y_ref[...] = x_ref[...] + _rms(mix, g_ref[...])


def _merge(x, oa, ob, oc, gates, wa, wb, wc, wo, g, tm):
    n, d = x.shape
    row = lambda width: pl.BlockSpec((tm, width), lambda i: (i, 0))
    return pl.pallas_call(
        functools.partial(_merge_kernel, d_model=d),
        grid=(n // tm,),
        in_specs=[row(d), row(SB_W), row(POOL_W), row(GLA_WP), row(3 * d),
                  _const_spec(wa.shape), _const_spec(wb.shape), _const_spec(wc.shape),
                  _const_spec(wo.shape), _const_spec((1, d))],
        out_specs=row(d),
        out_shape=jax.ShapeDtypeStruct((n, d), F32),
        compiler_params=_cparams(("parallel",)),
        name="merge_out",
    )(x, oa, ob, oc, gates, wa, wb, wc, wo, g)


def _xattn_kernel(x_ref, mk_ref, mv_ref, wq_ref, wo_ref, gpre_ref, gpost_ref, y_ref, *, d_model):
    x = x_ref[...]
    q = _dot(_rms(x, gpre_ref[...]).astype(BF), wq_ref[...]).astype(BF)
    hd = d_model // X_HEADS
    heads = [slice(h * hd, (h + 1) * hd) for h in range(X_HEADS)]
    s, o = {}, {}
    for i in range(X_HEADS + 1):
        if i < X_HEADS:
            s[i] = _dot_nt(q[:, heads[i]], mk_ref[0, :, heads[i]].astype(BF)) * (hd ** -0.5)
        j = i - 1
        if 0 <= j < X_HEADS:
            p = jnp.exp(s[j] - jnp.max(s[j], axis=-1, keepdims=True))
            p = p / jnp.sum(p, axis=-1, keepdims=True)
            o[j] = _dot(p.astype(BF), mv_ref[0, :, heads[j]].astype(BF)).astype(BF)
    attn = jnp.concatenate([o[h] for h in range(X_HEADS)], axis=-1)
    y_ref[...] = x + _rms(_dot(attn, wo_ref[...]), gpost_ref[...])


def _xattn_short_kernel(x_ref, mk_ref, mv_ref, wq_ref, wo_ref, gpre_ref, gpost_ref, y_ref,
                        q_scr, o_scr, *, d_model, seq):
    b = pl.program_id(0)

    @pl.when(b == 0)
    def _():
        q_scr[...] = _dot(_rms(x_ref[...], gpre_ref[...]).astype(BF), wq_ref[...]).astype(BF)

    rows = pl.ds(pl.multiple_of(b * seq, seq), seq)
    hd = d_model // X_HEADS
    for h in range(X_HEADS):
        hs = slice(h * hd, (h + 1) * hd)
        s = _dot_nt(q_scr[rows, hs], mk_ref[0, :, hs].astype(BF)) * (hd ** -0.5)
        p = jnp.exp(s - jnp.max(s, axis=-1, keepdims=True))
        p = p / jnp.sum(p, axis=-1, keepdims=True)
        o_scr[rows, hs] = _dot(p.astype(BF), mv_ref[0, :, hs].astype(BF)).astype(BF)

    @pl.when(b == pl.num_programs(0) - 1)
    def _():
        y_ref[...] = x_ref[...] + _rms(_dot(o_scr[...], wo_ref[...]), gpost_ref[...])


def _xattn_short(x, mk, mv, mem_index, wq, wo, gpre, gpost, batch, seq):
    n, d = x.shape
    mem_len = mk.shape[-2]
    mem = pl.BlockSpec((1, mem_len, d), lambda b: (mem_index(b), 0, 0))
    return pl.pallas_call(
        functools.partial(_xattn_short_kernel, d_model=d, seq=seq),
        grid=(batch,),
        in_specs=[_const_spec((n, d)), mem, mem, _const_spec((d, d)), _const_spec((d, d)),
                  _const_spec((1, d)), _const_spec((1, d))],
        out_specs=pl.BlockSpec((n, d), lambda b: (0, 0)),
        out_shape=jax.ShapeDtypeStruct((n, d), F32),
        scratch_shapes=[pltpu.VMEM((n, d), BF), pltpu.VMEM((n, d), BF)],
        compiler_params=_cparams(("arbitrary",)),
        name="cross_attn_short",
    )(x, mk, mv, wq, wo, gpre, gpost)


def _xattn(x, mk, mv, mem_index, wq, wo, gpre, gpost, batch, seq, tm):
    n, d = x.shape
    if seq < XATTN_SHORT_SEQ:
        return _xattn_short(x, mk, mv, mem_index, wq, wo, gpre, gpost, batch, seq)
    nt = seq // tm
    mem_len = mk.shape[-2]
    row = pl.BlockSpec((tm, d), lambda b, i: (b * nt + i, 0))
    mem = pl.BlockSpec((1, mem_len, d), lambda b, i: (mem_index(b), 0, 0))
    return pl.pallas_call(
        functools.partial(_xattn_kernel, d_model=d),
        grid=(batch, nt),
        in_specs=[row, mem, mem, _const_spec((d, d)), _const_spec((d, d)),
                  _const_spec((1, d)), _const_spec((1, d))],
        out_specs=row,
        out_shape=jax.ShapeDtypeStruct((n, d), F32),
        compiler_params=_cparams(("parallel", "parallel")),
        name="cross_attn",
    )(x, mk, mv, wq, wo, gpre, gpost)


def _ffn_kernel(x_ref, wg_ref, wu_ref, wo_ref, gpre_ref, gpost_ref, y_ref, *, chunk):
    x = x_ref[...]
    h = _rms(x, gpre_ref[...]).astype(BF)
    ff = wg_ref.shape[1]
    acc = jnp.zeros(x.shape, F32)
    for j in range(ff // chunk):
        cs = slice(j * chunk, (j + 1) * chunk)
        g = _dot(h, wg_ref[:, cs])
        u = _dot(h, wu_ref[:, cs])
        act = (g * _sigmoid(g) * u).astype(BF)
        acc = acc + _dot(act, wo_ref[cs, :])
    y_ref[...] = x + _rms(acc, gpost_ref[...])


def _ffn(x, wg, wu, wo, gpre, gpost, tm, chunk):
    n, d = x.shape
    row = pl.BlockSpec((tm, d), lambda i: (i, 0))
    return pl.pallas_call(
        functools.partial(_ffn_kernel, chunk=chunk),
        grid=(n // tm,),
        in_specs=[row, _const_spec(wg.shape), _const_spec(wu.shape), _const_spec(wo.shape),
                  _const_spec((1, d)), _const_spec((1, d))],
        out_specs=row,
        out_shape=jax.ShapeDtypeStruct((n, d), F32),
        compiler_params=_cparams(("parallel",)),
        name="ffn",
    )(x, wg, wu, wo, gpre, gpost)


def _memkv_kernel(m_ref, g_ref, wk_ref, wv_ref, k_ref, v_ref):
    m = _rms(m_ref[...], g_ref[0]).astype(BF)
    k_ref[0] = _dot(m, wk_ref[0].astype(BF))
    v_ref[0] = _dot(m, wv_ref[0].astype(BF))


def _memkv(mem, g, wk, wv, tm):
    depth, d, _ = wk.shape
    n = mem.shape[0]
    w = pl.BlockSpec((1, d, d), lambda l, i: (l, 0, 0))
    out = pl.BlockSpec((1, tm, d), lambda l, i: (l, i, 0))
    return pl.pallas_call(
        _memkv_kernel,
        grid=(depth, n // tm),
        in_specs=[pl.BlockSpec((tm, d), lambda l, i: (i, 0)),
                  pl.BlockSpec((1, 1, d), lambda l, i: (l, 0, 0)), w, w],
        out_specs=[out, out],
        out_shape=[jax.ShapeDtypeStruct((depth, n, d), F32)] * 2,
        compiler_params=_cparams(("parallel", "parallel")),
        name="mem_kv",
    )(mem, g, wk, wv)


def _pad_heads(w, axis):
    shape = w.shape
    w = w.reshape(shape[:axis] + (GLA_HEADS, GLA_D) + shape[axis + 1:])
    pad = [(0, 0)] * w.ndim
    pad[axis + 1] = (0, GLA_DP - GLA_D)
    w = jnp.pad(w, pad)
    return w.reshape(shape[:axis] + (GLA_WP,) + shape[axis + 1:])


def _pack_layer(p, l):
    w_in = jnp.swapaxes(p['w_in'][l], 0, 1)
    o = 3 * SB_W + POOL_W
    sb_u = w_in[:o]
    gla = [_pad_heads(w_in[o + j * GLA_W:o + (j + 1) * GLA_W], 0) for j in range(4)]
    o += 4 * GLA_W
    lr = jnp.pad(w_in[o:o + GLA_RANK], ((0, LANE - GLA_RANK), (0, 0)))
    gates = w_in[o + GLA_RANK:]
    eye = jnp.eye(len(POOL_WINDOWS), dtype=F32)
    w_pool_bd = (eye[:, None, :, None] * p['w_pool'][l][:, :, None, :]).reshape(POOL_W, POOL_W)
    ff = p['w_ffn_out'].shape[1]
    row = lambda v: v.reshape(1, -1)
    return dict(
        w_in=jnp.concatenate([sb_u] + gla + [lr, gates], axis=0).astype(BF),
        w_a2=jnp.pad(_pad_heads(p['w_gla_a2'][l], 1), ((0, LANE - GLA_RANK), (0, 0))).astype(BF),
        b_a=row(_pad_heads(p['b_gla_a'][l], 0)),
        gla_norm=row(_pad_heads(p['gla_norm'][l], 0)),
        w_pool=w_pool_bd.astype(BF),
        pool_scale=row(p['pool_scale'][l]),
        w_a=p['w_branch_a'][l].astype(BF),
        w_b=p['w_branch_b'][l].astype(BF),
        w_c=_pad_heads(p['w_branch_c'][l], 0).astype(BF),
        w_mix=p['w_mix_out'][l].astype(BF),
        w_xq=p['w_xq'][l].astype(BF),
        w_xo=p['w_xo'][l].astype(BF),
        w_g=p['w_ffn_in'][l][:, :ff].astype(BF),
        w_u=p['w_ffn_in'][l][:, ff:].astype(BF),
        w_o=p['w_ffn_out'][l].astype(BF),
        n_mix_pre=row(p['norm_mix_pre'][l]), n_mix_post=row(p['norm_mix_post'][l]),
        n_x_pre=row(p['norm_x_pre'][l]), n_x_post=row(p['norm_x_post'][l]),
        n_ffn_pre=row(p['norm_ffn_pre'][l]), n_ffn_post=row(p['norm_ffn_post'][l]),
    )


def _pad_state(s):
    s = jnp.swapaxes(s, -1, -2)
    return jnp.pad(s, ((0, 0), (0, 0), (0, GLA_DP - GLA_D), (0, GLA_DP - GLA_D)))


def _unpad_state(st):
    return jnp.swapaxes(st[:, :, :GLA_D, :GLA_D], -1, -2)


def _tiles(batch, seq):
    n = batch * seq
    return dict(
        proj=min(512, n),
        sb=min(256, seq),
        pool=min(256, seq),
        gla=min(128, seq),
        merge=min(512, n),
        xattn=min(512, seq),
        ffn=min(512, n),
    )


def _layer(x, w, batch, seq, sb_attend, pool_hist, pos0, s0t, mk, mv, mem_index, kv_stack=None):
    t = _tiles(batch, seq)
    q, k, v, kb, vb, u, qkvg, log_a, gates = _proj(x, w['n_mix_pre'], w['w_in'], w['w_a2'], w['b_a'],
                                                   t['proj'], kv_stack)
    oa = sb_attend(q, kb, vb)
    ob = _pool(u, pool_hist, w['w_pool'], w['pool_scale'], batch, seq, t['pool'], pos0)
    oc, st = _gla(qkvg, log_a, w['gla_norm'], s0t, batch, seq, t['gla'])
    x = _merge(x, oa, ob, oc, gates, w['w_a'], w['w_b'], w['w_c'], w['w_mix'], w['n_mix_post'],
               t['merge'])
    x = _xattn(x, mk, mv, mem_index, w['w_xq'], w['w_xo'], w['n_x_pre'], w['n_x_post'],
               batch, seq, t['xattn'])
    x = _ffn(x, w['w_g'], w['w_u'], w['w_o'], w['n_ffn_pre'], w['n_ffn_post'], t['ffn'], 256)
    return x, k, v, u, st


def kernel(x_prompt, x_sample, mem_prompt, cache_sb_k, cache_sb_v, state_pool, state_gla, cache_mem_k, cache_mem_v, w_in, w_gla_a2, b_gla_a, gla_norm, w_pool, pool_scale, w_branch_a, w_branch_b, w_branch_c, w_mix_out, mem_norm, w_xq, w_xk, w_xv, w_xo, w_ffn_in, w_ffn_out, norm_mix_pre, norm_mix_post, norm_x_pre, norm_x_post, norm_ffn_pre, norm_ffn_post):
    p = dict(w_in=w_in, w_gla_a2=w_gla_a2, b_gla_a=b_gla_a, gla_norm=gla_norm, w_pool=w_pool,
             pool_scale=pool_scale, w_branch_a=w_branch_a, w_branch_b=w_branch_b,
             w_branch_c=w_branch_c, w_mix_out=w_mix_out, w_xq=w_xq, w_xo=w_xo,
             w_ffn_in=w_ffn_in, w_ffn_out=w_ffn_out, norm_mix_pre=norm_mix_pre,
             norm_mix_post=norm_mix_post, norm_x_pre=norm_x_pre, norm_x_post=norm_x_post,
             norm_ffn_pre=norm_ffn_pre, norm_ffn_post=norm_ffn_post)
    depth = w_in.shape[0]
    bp, tp, d = x_prompt.shape
    bs, ts, _ = x_sample.shape
    mem_len = mem_prompt.shape[1]
    past = cache_sb_k.shape[2]

    mk_all, mv_all = _memkv(mem_prompt.reshape(bp * mem_len, d), mem_norm.reshape(depth, 1, d),
                            w_xk, w_xv, min(512, bp * mem_len))
    mk_p = mk_all.reshape(depth * bp, mem_len, d)
    mv_p = mv_all.reshape(depth * bp, mem_len, d)
    mk_s = cache_mem_k.reshape(depth * bs, mem_len, d)
    mv_s = cache_mem_v.reshape(depth * bs, mem_len, d)
    ck = jnp.transpose(cache_sb_k, (0, 1, 3, 4, 2)).reshape(depth, bs, SB_W, past)
    cv = jnp.transpose(cache_sb_v, (0, 1, 3, 4, 2)).reshape(depth, bs, SB_W, past)

    yp = x_prompt.reshape(bp * tp, d)
    ys = x_sample.reshape(bs * ts, d)
    hist_zero = jnp.zeros((bp, HIST_PAD, POOL_W), F32)
    state_zero = jnp.zeros((bp, GLA_HEADS, GLA_DP, GLA_DP), F32)
    tiles_p = _tiles(bp, tp)
    kv_t = None
    poolp, glap, kss, vss, pools, glas = ([] for _ in range(6))
    for l in range(depth):
        w = _pack_layer(p, l)
        yp, k, v, u, st = _layer(
            yp, w, bp, tp,
            functools.partial(_sb_prompt, batch=bp, seq=tp, blk=tiles_p['sb'],
                              nq=2 if tp % (2 * tiles_p['sb']) == 0 else 1),
            hist_zero, 0, state_zero, mk_p, mv_p, lambda b, l=l: l * bp + b,
            kv_stack=(l, depth, bp, tp, kv_t))
        kv_t = (k, v)
        poolp.append(u.reshape(bp, tp, POOL_W)[:, tp - POOL_HIST:])
        glap.append(_unpad_state(st))

        hist = jnp.pad(state_pool[l], ((0, 0), (HIST_PAD - POOL_HIST, 0), (0, 0)))
        ys, k, v, u, st = _layer(
            ys, w, bs, ts,
            functools.partial(_sb_sample, cache_k=ck, cache_v=cv, layer=l, batch=bs, seq=ts,
                              blk=min(256, past)),
            hist, past, _pad_state(state_gla[l]), mk_s, mv_s, lambda b, l=l: l * bs + b)
        kss.append(k.reshape(bs, ts, SB_HEADS, SB_DIM))
        vss.append(v.reshape(bs, ts, SB_HEADS, SB_DIM))
        full = jnp.concatenate([state_pool[l], u.reshape(bs, ts, POOL_W)], axis=1)
        pools.append(full[:, -POOL_HIST:])
        glas.append(_unpad_state(st))

    kp, vp = (jnp.transpose(a.reshape(depth, bp, SB_HEADS, SB_DIM, tp), (0, 1, 4, 2, 3)) for a in kv_t)
    return (yp.reshape(bp, tp, d), ys.reshape(bs, ts, d),
            kp, vp, jnp.stack(poolp), jnp.stack(glap),
            mk_all.reshape(depth, bp, mem_len, X_HEADS, d // X_HEADS),
            mv_all.reshape(depth, bp, mem_len, X_HEADS, d // X_HEADS),
            jnp.stack(kss), jnp.stack(vss), jnp.stack(pools), jnp.stack(glas))
```

```python
import functools

import numpy as np
import jax
import jax.numpy as jnp
from jax import lax
from jax.experimental import pallas as pl
from jax.experimental.pallas import tpu as pltpu

BF = jnp.bfloat16
F32 = jnp.float32
EPS = 1e-6
LOG2_E = 1.4426950408889634

LANE = 128
VMEM_LIMIT = 56 * 1024 * 1024

SB_HEADS = 6
SB_DIM = 64
SB_W = SB_HEADS * SB_DIM
POOL_WINDOWS = (2, 4, 8, 16)
POOL_GDIM = 64
POOL_W = len(POOL_WINDOWS) * POOL_GDIM
POOL_HIST = max(POOL_WINDOWS) - 1
HIST_PAD = 16
assert all(w & (w - 1) == 0 for w in POOL_WINDOWS) and 2 * POOL_GDIM == LANE and POOL_HIST <= HIST_PAD
GLA_HEADS = 4
GLA_D = 96
GLA_DP = LANE
GLA_W = GLA_HEADS * GLA_D
GLA_WP = GLA_HEADS * GLA_DP
GLA_RANK = 16
GLA_TAU = 16.0
X_HEADS = 4
XATTN_SHORT_SEQ = 128

C_Q = 0
C_K = C_Q + SB_W
C_V = C_K + SB_W
C_U = C_V + SB_W
C_GLA = C_U + POOL_W
C_LR = C_GLA + 4 * GLA_WP
C_END = C_LR + LANE


def _cparams(sem):
    return pltpu.CompilerParams(dimension_semantics=sem, vmem_limit_bytes=VMEM_LIMIT)


def _rms(x, g):
    return x * lax.rsqrt(jnp.mean(x * x, axis=-1, keepdims=True) + EPS) * g


def _softplus_neg_abs(z):
    return jnp.log(1.0 + jnp.exp(-jnp.abs(z)))


def _sigmoid(z):
    return 0.5 * jnp.tanh(0.5 * z) + 0.5


def _dot(a, b):
    return jnp.dot(a, b, preferred_element_type=F32)


def _dot_nt(a, b):
    return lax.dot_general(a, b, (((1,), (1,)), ((), ())), preferred_element_type=F32)


def _dot_tn(a, b):
    return lax.dot_general(a, b, (((0,), (0,)), ((), ())), preferred_element_type=F32)


def _const_spec(shape):
    zeros = (0,) * len(shape)
    return pl.BlockSpec(shape, lambda *_: zeros, pipeline_mode=pl.Buffered(1))


def _proj_kernel(*refs, n_alias, transposed_kv):
    x_ref, g_ref, w_ref, wa_ref, ba_ref = refs[:N_PROJ_IN]
    q_ref, k_ref, v_ref, kb_ref, vb_ref, u_ref, gla_ref, la_ref = refs[N_PROJ_IN + n_alias:]
    h = _rms(x_ref[...], g_ref[...]).astype(BF)

    def mm(lo, width):
        return _dot_nt(h, w_ref[lo:lo + width, :])

    q_ref[...] = (mm(C_Q, SB_W) * (SB_DIM ** -0.5)).astype(BF)
    for lo, f32_ref, bf_ref in ((C_K, k_ref, kb_ref), (C_V, v_ref, vb_ref)):
        kv = mm(lo, SB_W)
        bf_ref[...] = kv.astype(BF)
        if transposed_kv is None:
            f32_ref[...] = kv
        else:
            for l in range(f32_ref.shape[0]):
                f32_ref[l, 0] = kv.T if l == transposed_kv else jnp.zeros((SB_W, kv.shape[0]), F32)
    u_ref[...] = mm(C_U, POOL_W)
    for j in range(4):
        gla_ref[:, j * GLA_WP:(j + 1) * GLA_WP] = mm(C_GLA + j * GLA_WP, GLA_WP).astype(BF)
    xa = _dot(mm(C_LR, LANE).astype(BF), wa_ref[...]) + ba_ref[...]
    la_ref[...] = ((jnp.minimum(xa, 0.0) - _softplus_neg_abs(xa)) * (LOG2_E / GLA_TAU)).astype(BF)


N_PROJ_IN = 5


def _proj(x, g, w, wa, ba, tm, kv_stack=None):
    n, d = x.shape
    wtot = w.shape[0]
    row = lambda width: pl.BlockSpec((tm, width), lambda i: (i, 0))
    if kv_stack is None:
        kv_spec, kv_shape, extra, aliases = row(SB_W), (n, SB_W), [], {}
    else:
        layer, depth, batch, seq, bufs = kv_stack
        nt = seq // tm
        kv_shape = (depth, batch, SB_W, seq)
        if bufs is None:
            kv_spec = pl.BlockSpec((depth, 1, SB_W, tm), lambda i: (0, i // nt, 0, i % nt))
            kv_slot = layer
        else:
            kv_spec = pl.BlockSpec((1, 1, SB_W, tm), lambda i: (layer, i // nt, 0, i % nt))
            kv_slot = 0
        extra = [] if bufs is None else list(bufs)
        aliases = {N_PROJ_IN + j: 1 + j for j in range(len(extra))}
    outs = [
        (row(SB_W), (n, SB_W), BF), (kv_spec, kv_shape, F32), (kv_spec, kv_shape, F32),
        (row(SB_W), (n, SB_W), BF), (row(SB_W), (n, SB_W), BF), (row(POOL_W), (n, POOL_W), F32),
        (row(4 * GLA_WP), (n, 4 * GLA_WP), BF), (row(GLA_WP), (n, GLA_WP), BF),
    ]
    return pl.pallas_call(
        functools.partial(_proj_kernel, n_alias=len(extra),
                          transposed_kv=None if kv_stack is None else kv_slot),
        grid=(n // tm,),
        in_specs=[row(d), _const_spec((1, d)), _const_spec((wtot, d)),
                  _const_spec((LANE, GLA_WP)), _const_spec((1, GLA_WP))]
                 + [pl.BlockSpec(memory_space=pl.ANY)] * len(extra),
        out_specs=[spec for spec, _, _ in outs],
        out_shape=[jax.ShapeDtypeStruct(shape, dt) for _, shape, dt in outs],
        input_output_aliases=aliases,
        compiler_params=_cparams(("parallel",)),
        name="in_proj",
    )(x, g, w, wa, ba, *extra)


def _upper_ones(n):
    r = lax.broadcasted_iota(jnp.int32, (n, n), 0)
    c = lax.broadcasted_iota(jnp.int32, (n, n), 1)
    return jnp.where(r >= c, 1.0, 0.0).astype(BF)


def _strict_causal(n):
    r = lax.broadcasted_iota(jnp.int32, (n, n), 0)
    c = lax.broadcasted_iota(jnp.int32, (n, n), 1)
    return c < r


def _sb_weights(z, carry, upper, mask):
    return _sb_finish(z, _dot(_sb_softplus(z, mask), upper), carry, mask)


def _sb_softplus(z, mask):
    zb = z.astype(BF)
    t = jnp.maximum(zb, 0.0) + _softplus_neg_abs(zb)
    return t if mask is None else jnp.where(mask, t, jnp.zeros_like(t))


def _sb_finish(z, cum, carry, mask):
    w = jnp.exp(z - cum - carry)
    if mask is not None:
        w = jnp.where(mask, w, 0.0)
    return w.astype(BF), carry + cum[:, 0:1]


SB_SKIP = "skip"
SB_LAG = 1


def _sb_prompt_kernel(q_ref, k_ref, v_ref, o_ref, acc_ref, *, blk, nq):
    first = pl.program_id(1) * nq
    pair_w = 2 * SB_DIM
    n_pairs = SB_HEADS // 2
    upper = _upper_ones(blk)
    causal = _strict_causal(blk)
    low = lax.broadcasted_iota(jnp.int32, (blk, pair_w), 1) < SB_DIM
    zero = jnp.zeros((blk, pair_w), BF)

    def split(x):
        return jnp.where(low, x, zero), jnp.where(low, zero, x)

    q_heads = []
    for s in range(nq):
        heads = []
        for p in range(n_pairs):
            heads.extend(split(q_ref[s * blk:(s + 1) * blk, p * pair_w:(p + 1) * pair_w]))
        q_heads.append(heads)
    acc_ref[...] = jnp.zeros(acc_ref.shape, F32)

    def visit(kb, carries, masks):
        start = pl.multiple_of(kb * blk, blk)
        out = list(carries)
        pairs = [slice(p * pair_w, (p + 1) * pair_w) for p in range(n_pairs)]
        live = [(s, 2 * p + e) for p in range(n_pairs) for s in range(nq)
                if masks[s] is not SB_SKIP for e in range(2)]
        k_pairs = [k_ref[pl.ds(start, blk), ps] for ps in pairs]
        v_pairs = [jnp.concatenate(split(v_ref[pl.ds(start, blk), ps]), axis=0) for ps in pairs]
        z, cum, w = {}, {}, {}
        for i in range(len(live) + 2 * SB_LAG):
            if i < len(live):
                s, h = live[i]
                z[i] = _dot_nt(q_heads[s][h], k_pairs[h // 2])
            j = i - SB_LAG
            if 0 <= j < len(live):
                cum[j] = _dot(_sb_softplus(z[j], masks[live[j][0]]), upper)
            j = i - 2 * SB_LAG
            if 0 <= j < len(live):
                s, h = live[j]
                c = s * SB_HEADS + h
                w[j], out[c] = _sb_finish(z[j], cum[j], carries[c], masks[s])
                if h % 2 == 1:
                    w_pair = jnp.concatenate([w[j - 1], w[j]], axis=1)
                    ps = pairs[h // 2]
                    acc_ref[s * blk:(s + 1) * blk, ps] += _dot(w_pair, v_pairs[h // 2])
        return tuple(out)

    carries = tuple(jnp.zeros((blk, 1), F32) for _ in range(nq * SB_HEADS))
    for d in reversed(range(nq)):
        masks = tuple(SB_SKIP if d > s else (causal if d == s else None) for s in range(nq))
        carries = visit(first + d, carries, masks)
    lax.fori_loop(0, first, lambda j, c: visit(first - 1 - j, c, (None,) * nq), carries)
    o_ref[...] = acc_ref[...].astype(BF)


def _sb_prompt(q, kb, vb, batch, seq, blk, nq):
    nblk = seq // (blk * nq)
    return pl.pallas_call(
        functools.partial(_sb_prompt_kernel, blk=blk, nq=nq),
        grid=(batch, nblk),
        in_specs=[
            pl.BlockSpec((nq * blk, SB_W), lambda b, i: (b * nblk + i, 0)),
            pl.BlockSpec((seq, SB_W), lambda b, i: (b, 0)),
            pl.BlockSpec((seq, SB_W), lambda b, i: (b, 0)),
        ],
        out_specs=pl.BlockSpec((nq * blk, SB_W), lambda b, i: (b * nblk + i, 0)),
        out_shape=jax.ShapeDtypeStruct((batch * seq, SB_W), BF),
        scratch_shapes=[pltpu.VMEM((nq * blk, SB_W), F32)],
        compiler_params=_cparams(("parallel", "arbitrary")),
        name="sb_prompt",
    )(q, kb, vb)


def _sb_sample_kernel(q_ref, kn_ref, vn_ref, kc_ref, vc_ref, o_ref, *, seq, past, blk):
    pair_w = 2 * SB_DIM
    n_pairs = SB_HEADS // 2
    pairs = [slice(p * pair_w, (p + 1) * pair_w) for p in range(n_pairs)]

    def split(x, axis):
        low = lax.broadcasted_iota(jnp.int32, x.shape, axis) < SB_DIM
        zero = jnp.zeros(x.shape, x.dtype)
        return jnp.where(low, x, zero), jnp.where(low, zero, x)

    q_heads = []
    for ps in pairs:
        q_heads.extend(split(q_ref[:, ps], 1))
    r = lax.broadcasted_iota(jnp.int32, (SB_HEADS * seq, seq), 0) % seq
    causal = lax.broadcasted_iota(jnp.int32, (SB_HEADS * seq, seq), 1) < r

    def visit(k_pairs, v_pairs, transposed, carry, acc, upper, mask):
        qk = _dot if transposed else _dot_nt
        pv = _dot_nt if transposed else _dot
        z = jnp.concatenate([qk(q_heads[h], k_pairs[h // 2]) for h in range(SB_HEADS)], axis=0)
        w, carry = _sb_weights(z, carry, upper, mask)
        out = []
        for p in range(n_pairs):
            v_lo, v_hi = split(v_pairs[p], 0 if transposed else 1)
            w_lo = w[(2 * p) * seq:(2 * p + 1) * seq]
            w_hi = w[(2 * p + 1) * seq:(2 * p + 2) * seq]
            out.append(acc[p] + pv(w_lo, v_lo) + pv(w_hi, v_hi))
        return carry, out

    carry = jnp.zeros((SB_HEADS * seq, 1), F32)
    acc = [jnp.zeros((seq, pair_w), F32) for _ in pairs]
    carry, acc = visit([kn_ref[:, ps] for ps in pairs], [vn_ref[:, ps] for ps in pairs], False,
                       carry, acc, _upper_ones(seq), causal)
    upper_past = _upper_ones(blk)
    for j in reversed(range(past // blk)):
        cols = slice(j * blk, (j + 1) * blk)
        carry, acc = visit([kc_ref[0, 0, ps, cols].astype(BF) for ps in pairs],
                           [vc_ref[0, 0, ps, cols].astype(BF) for ps in pairs], True,
                           carry, acc, upper_past, None)
    for p, ps in enumerate(pairs):
        o_ref[:, ps] = acc[p].astype(BF)


def _sb_sample(q, kb, vb, cache_k, cache_v, layer, batch, seq, blk):
    past = cache_k.shape[3]
    new = pl.BlockSpec((seq, SB_W), lambda b: (b, 0))
    cache = pl.BlockSpec((1, 1, SB_W, past), lambda b: (layer, b, 0, 0))
    return pl.pallas_call(
        functools.partial(_sb_sample_kernel, seq=seq, past=past, blk=blk),
        grid=(batch,),
        in_specs=[new, new, new, cache, cache],
        out_specs=new,
        out_shape=jax.ShapeDtypeStruct((batch * seq, SB_W), BF),
        compiler_params=_cparams(("parallel",)),
        name="sb_sample",
    )(q, kb, vb, cache_k, cache_v)


def _pool_kernel(u_ref, hist_ref, w_ref, scale_ref, o_ref, buf_ref, *, tm, pos0):
    i = pl.program_id(1)

    @pl.when(i == 0)
    def _():
        buf_ref[0:HIST_PAD, :] = hist_ref[0]

    @pl.when(i > 0)
    def _():
        buf_ref[0:HIST_PAD, :] = buf_ref[tm:tm + HIST_PAD, :]

    u = u_ref[...]
    buf_ref[HIST_PAD:HIST_PAD + tm, :] = u

    rows = tm + HIST_PAD
    low = lax.broadcasted_iota(jnp.int32, (tm, LANE), 1) < POOL_GDIM
    pos1 = pos0 + i * tm + lax.broadcasted_iota(jnp.int32, (tm, LANE), 0) + 1
    pooled = []
    for tile in range(POOL_W // LANE):
        cols = slice(tile * LANE, (tile + 1) * LANE)
        w_lo, w_hi = POOL_WINDOWS[2 * tile], POOL_WINDOWS[2 * tile + 1]
        s = buf_ref[0:rows, cols]
        sums, w = {}, 1
        while w < w_hi:
            s = s + pltpu.roll(s, w, 0)
            w *= 2
            sums[w] = s
        win_sum = jnp.where(low, sums[w_lo][HIST_PAD:], sums[w_hi][HIST_PAD:])
        cnt = jnp.minimum(pos1, jnp.where(low, w_lo, w_hi)).astype(F32)
        pooled.append(win_sum / cnt - u[:, cols])
    pooled = jnp.concatenate(pooled, axis=1)
    o_ref[...] = (_dot(pooled.astype(BF), w_ref[...]) * scale_ref[...]).astype(BF)


def _pool(u, hist, w_bd, scale, batch, seq, tm, pos0):
    nt = seq // tm
    return pl.pallas_call(
        functools.partial(_pool_kernel, tm=tm, pos0=pos0),
        grid=(batch, nt),
        in_specs=[
            pl.BlockSpec((tm, POOL_W), lambda b, i: (b * nt + i, 0)),
            pl.BlockSpec((1, HIST_PAD, POOL_W), lambda b, i: (b, 0, 0)),
            _const_spec((POOL_W, POOL_W)),
            _const_spec((1, POOL_W)),
        ],
        out_specs=pl.BlockSpec((tm, POOL_W), lambda b, i: (b * nt + i, 0)),
        out_shape=jax.ShapeDtypeStruct((batch * seq, POOL_W), BF),
        scratch_shapes=[pltpu.VMEM((tm + HIST_PAD, POOL_W), F32)],
        compiler_params=_cparams(("parallel", "arbitrary")),
        name="pool_mix",
    )(u, hist, w_bd, scale)


def _gla_levels(rows):
    out, s = [], 1
    while s < rows:
        out.append(s)
        s *= 2
    return out


def _gla_exponent_matrix(rows):
    t = np.arange(rows)[:, None]
    j = np.arange(rows)[None, :]
    blocks = [j <= t]
    for s in _gla_levels(rows)[1:]:
        start = (t // s) * s
        later = (t // s) % 2 == 1
        blocks.append(np.where(later, (j >= start) & (j <= t), (j > t) & (j <= start + s - 1)))
    return np.concatenate(blocks, axis=0).astype(np.float32)


def _gla_later_mask(rows):
    t = np.arange(rows)[:, None]
    blocks = [np.broadcast_to((t // s) % 2 == 1, (rows, GLA_WP)) for s in _gla_levels(rows)]
    return np.concatenate(blocks, axis=0).astype(np.float32)


def _gla_kernel(qkvg_ref, la_ref, em_ref, lm_ref, gn_ref, s0_ref, o_ref, st_ref, *, rows):
    @pl.when(pl.program_id(1) == 0)
    def _():
        st_ref[...] = s0_ref[...]

    expo = _dot(em_ref[...], la_ref[...])

    qb = qkvg_ref[:, 0:GLA_WP]
    kb = qkvg_ref[:, GLA_WP:2 * GLA_WP]
    vb = qkvg_ref[:, 2 * GLA_WP:3 * GLA_WP]
    gate = qkvg_ref[:, 3 * GLA_WP:4 * GLA_WP].astype(F32)

    b = expo[0:rows]
    b_last = b[rows - 1:rows]
    q_dec = qb * jnp.exp2(b).astype(BF)
    k_dec = kb * jnp.exp2(b_last - b).astype(BF)

    row = lax.broadcasted_iota(jnp.int32, (rows, GLA_WP), 0)
    r = lax.broadcasted_iota(jnp.int32, (rows, rows), 0)
    c = lax.broadcasted_iota(jnp.int32, (rows, rows), 1)
    heads = [slice(h * GLA_DP, (h + 1) * GLA_DP) for h in range(GLA_HEADS)]

    att = [jnp.where(r == c, _dot_nt(qb[:, hs], kb[:, hs]), 0.0) for hs in heads]
    for n, s in enumerate(_gla_levels(rows)):
        shift = s.bit_length() - 1
        rb = r >> shift
        pair = ((rb & 1) == 1) & ((c >> shift) == rb - 1)
        if s == 1:
            e = jnp.where((row & 1) == 1, la_ref[...].astype(F32), 0.0)
        else:
            e = expo[n * rows:(n + 1) * rows]
        later = lm_ref[n * rows:(n + 1) * rows] > 0
        x_lvl = jnp.where(later, qb, kb) * jnp.exp2(e).astype(BF)
        for h, hs in enumerate(heads):
            att[h] = jnp.where(pair, _dot_nt(x_lvl[:, hs], x_lvl[:, hs]), att[h])

    for h, hs in enumerate(heads):
        st = st_ref[0, h]
        o = (_dot(att[h].astype(BF), vb[:, hs]) + _dot_nt(q_dec[:, hs], st.astype(BF))) * (GLA_D ** -0.5)
        ms = jnp.sum(o * o, axis=-1, keepdims=True) * (1.0 / GLA_D)
        y = o * lax.rsqrt(ms + EPS) * gn_ref[:, hs]
        gh = gate[:, hs]
        o_ref[:, hs] = (y * (gh * _sigmoid(gh))).astype(BF)
        st_ref[0, h] = st * jnp.exp2(b_last[:, hs]) + _dot_tn(vb[:, hs], k_dec[:, hs])


def _gla(qkvg, log_a, gn, s0t, batch, seq, rows):
    nt = seq // rows
    em = jnp.asarray(_gla_exponent_matrix(rows), BF)
    lm = jnp.asarray(_gla_later_mask(rows), BF)
    state = pl.BlockSpec((1, GLA_HEADS, GLA_DP, GLA_DP), lambda b, i: (b, 0, 0, 0))
    return pl.pallas_call(
        functools.partial(_gla_kernel, rows=rows),
        grid=(batch, nt),
        in_specs=[
            pl.BlockSpec((rows, 4 * GLA_WP), lambda b, i: (b * nt + i, 0)),
            pl.BlockSpec((rows, GLA_WP), lambda b, i: (b * nt + i, 0)),
            _const_spec(em.shape),
            _const_spec(lm.shape),
            _const_spec((1, GLA_WP)),
            state,
        ],
        out_specs=[pl.BlockSpec((rows, GLA_WP), lambda b, i: (b * nt + i, 0)), state],
        out_shape=[jax.ShapeDtypeStruct((batch * seq, GLA_WP), BF),
                   jax.ShapeDtypeStruct((batch, GLA_HEADS, GLA_DP, GLA_DP), F32)],
        compiler_params=_cparams(("parallel", "arbitrary")),
        name="gla",
    )(qkvg, log_a, em, lm, gn, s0t)


def _merge_kernel(x_ref, oa_ref, ob_ref, oc_ref, wg_ref, wa_ref, wb_ref, wc_ref, wo_ref,
                  gpre_ref, gpost_ref, y_ref, *, d_model):
    x = x_ref[...]
    h = _rms(x, gpre_ref[...]).astype(BF)
    merged = None
    for j, (o_ref, w_ref) in enumerate(((oa_ref, wa_ref), (ob_ref, wb_ref), (oc_ref, wc_ref))):
        gate = _sigmoid(_dot_nt(h, wg_ref[j * d_model:(j + 1) * d_model, :]))
        term = gate * _dot(o_ref[...], w_ref[...])
        merged = term if merged is None else merged + term
    mix = _dot(merged.astype(BF), wo_ref[...])
    y_ref[...] = x + _rms(mix, gpost_ref[...])


def _merge(x, oa, ob, oc, wg, wa, wb, wc, wo, gpre, gpost, tm):
    n, d = x.shape
    row = lambda width: pl.BlockSpec((tm, width), lambda i: (i, 0))
    return pl.pallas_call(
        functools.partial(_merge_kernel, d_model=d),
        grid=(n // tm,),
        in_specs=[row(d), row(SB_W), row(POOL_W), row(GLA_WP),
                  _const_spec(wg.shape), _const_spec(wa.shape), _const_spec(wb.shape),
                  _const_spec(wc.shape), _const_spec(wo.shape),
                  _const_spec((1, d)), _const_spec((1, d))],
        out_specs=row(d),
        out_shape=jax.ShapeDtypeStruct((n, d), F32),
        compiler_params=_cparams(("parallel",)),
        name="merge_out",
    )(x, oa, ob, oc, wg, wa, wb, wc, wo, gpre, gpost)


def _xattn_kernel(x_ref, mk_ref, mv_ref, wq_ref, wo_ref, gpre_ref, gpost_ref, y_ref, *, d_model):
    x = x_ref[...]
    q = _dot(_rms(x, gpre_ref[...]).astype(BF), wq_ref[...]).astype(BF)
    hd = d_model // X_HEADS
    heads = [slice(h * hd, (h + 1) * hd) for h in range(X_HEADS)]
    s, o = {}, {}
    for i in range(X_HEADS + 1):
        if i < X_HEADS:
            s[i] = _dot_nt(q[:, heads[i]], mk_ref[0, :, heads[i]].astype(BF)) * (hd ** -0.5)
        j = i - 1
        if 0 <= j < X_HEADS:
            p = jnp.exp(s[j] - jnp.max(s[j], axis=-1, keepdims=True))
            p = p / jnp.sum(p, axis=-1, keepdims=True)
            o[j] = _dot(p.astype(BF), mv_ref[0, :, heads[j]].astype(BF)).astype(BF)
    attn = jnp.concatenate([o[h] for h in range(X_HEADS)], axis=-1)
    y_ref[...] = x + _rms(_dot(attn, wo_ref[...]), gpost_ref[...])


def _xattn_short_kernel(x_ref, mk_ref, mv_ref, wq_ref, wo_ref, gpre_ref, gpost_ref, y_ref,
                        q_scr, o_scr, *, d_model, seq):
    b = pl.program_id(0)

    @pl.when(b == 0)
    def _():
        q_scr[...] = _dot(_rms(x_ref[...], gpre_ref[...]).astype(BF), wq_ref[...]).astype(BF)

    rows = pl.ds(pl.multiple_of(b * seq, seq), seq)
    hd = d_model // X_HEADS
    for h in range(X_HEADS):
        hs = slice(h * hd, (h + 1) * hd)
        s = _dot_nt(q_scr[rows, hs], mk_ref[0, :, hs].astype(BF)) * (hd ** -0.5)
        p = jnp.exp(s - jnp.max(s, axis=-1, keepdims=True))
        p = p / jnp.sum(p, axis=-1, keepdims=True)
        o_scr[rows, hs] = _dot(p.astype(BF), mv_ref[0, :, hs].astype(BF)).astype(BF)

    @pl.when(b == pl.num_programs(0) - 1)
    def _():
        y_ref[...] = x_ref[...] + _rms(_dot(o_scr[...], wo_ref[...]), gpost_ref[...])


def _xattn_short(x, mk, mv, mem_index, wq, wo, gpre, gpost, batch, seq):
    n, d = x.shape
    mem_len = mk.shape[-2]
    mem = pl.BlockSpec((1, mem_len, d), lambda b: (mem_index(b), 0, 0))
    return pl.pallas_call(
        functools.partial(_xattn_short_kernel, d_model=d, seq=seq),
        grid=(batch,),
        in_specs=[_const_spec((n, d)), mem, mem, _const_spec((d, d)), _const_spec((d, d)),
                  _const_spec((1, d)), _const_spec((1, d))],
        out_specs=pl.BlockSpec((n, d), lambda b: (0, 0)),
        out_shape=jax.ShapeDtypeStruct((n, d), F32),
        scratch_shapes=[pltpu.VMEM((n, d), BF), pltpu.VMEM((n, d), BF)],
        compiler_params=_cparams(("arbitrary",)),
        name="cross_attn_short",
    )(x, mk, mv, wq, wo, gpre, gpost)


def _xattn(x, mk, mv, mem_index, wq, wo, gpre, gpost, batch, seq, tm):
    n, d = x.shape
    if seq < XATTN_SHORT_SEQ:
        return _xattn_short(x, mk, mv, mem_index, wq, wo, gpre, gpost, batch, seq)
    nt = seq // tm
    mem_len = mk.shape[-2]
    row = pl.BlockSpec((tm, d), lambda b, i: (b * nt + i, 0))
    mem = pl.BlockSpec((1, mem_len, d), lambda b, i: (mem_index(b), 0, 0))
    return pl.pallas_call(
        functools.partial(_xattn_kernel, d_model=d),
        grid=(batch, nt),
        in_specs=[row, mem, mem, _const_spec((d, d)), _const_spec((d, d)),
                  _const_spec((1, d)), _const_spec((1, d))],
        out_specs=row,
        out_shape=jax.ShapeDtypeStruct((n, d), F32),
        compiler_params=_cparams(("parallel", "parallel")),
        name="cross_attn",
    )(x, mk, mv, wq, wo, gpre, gpost)


def _ffn_kernel(x_ref, wg_ref, wu_ref, wo_ref, gpre_ref, gpost_ref, y_ref, *, chunk):
    x = x_ref[...]
    h = _rms(x, gpre_ref[...]).astype(BF)
    ff = wg_ref.shape[1]
    acc = jnp.zeros(x.shape, F32)
    for j in range(ff // chunk):
        cs = slice(j * chunk, (j + 1) * chunk)
        g = _dot(h, wg_ref[:, cs])
        u = _dot(h, wu_ref[:, cs])
        act = (g * _sigmoid(g) * u).astype(BF)
        acc = acc + _dot(act, wo_ref[cs, :])
    y_ref[...] = x + _rms(acc, gpost_ref[...])


def _ffn(x, wg, wu, wo, gpre, gpost, tm, chunk):
    n, d = x.shape
    row = pl.BlockSpec((tm, d), lambda i: (i, 0))
    return pl.pallas_call(
        functools.partial(_ffn_kernel, chunk=chunk),
        grid=(n // tm,),
        in_specs=[row, _const_spec(wg.shape), _const_spec(wu.shape), _const_spec(wo.shape),
                  _const_spec((1, d)), _const_spec((1, d))],
        out_specs=row,
        out_shape=jax.ShapeDtypeStruct((n, d), F32),
        compiler_params=_cparams(("parallel",)),
        name="ffn",
    )(x, wg, wu, wo, gpre, gpost)


def _memkv_kernel(m_ref, g_ref, wk_ref, wv_ref, k_ref, v_ref):
    m = _rms(m_ref[...], g_ref[0]).astype(BF)
    k_ref[0] = _dot(m, wk_ref[0].astype(BF))
    v_ref[0] = _dot(m, wv_ref[0].astype(BF))


def _memkv(mem, g, wk, wv, tm):
    depth, d, _ = wk.shape
    n = mem.shape[0]
    w = pl.BlockSpec((1, d, d), lambda l, i: (l, 0, 0))
    out = pl.BlockSpec((1, tm, d), lambda l, i: (l, i, 0))
    return pl.pallas_call(
        _memkv_kernel,
        grid=(depth, n // tm),
        in_specs=[pl.BlockSpec((tm, d), lambda l, i: (i, 0)),
                  pl.BlockSpec((1, 1, d), lambda l, i: (l, 0, 0)), w, w],
        out_specs=[out, out],
        out_shape=[jax.ShapeDtypeStruct((depth, n, d), F32)] * 2,
        compiler_params=_cparams(("parallel", "parallel")),
        name="mem_kv",
    )(mem, g, wk, wv)


def _pad_heads(w, axis):
    shape = w.shape
    w = w.reshape(shape[:axis] + (GLA_HEADS, GLA_D) + shape[axis + 1:])
    pad = [(0, 0)] * w.ndim
    pad[axis + 1] = (0, GLA_DP - GLA_D)
    w = jnp.pad(w, pad)
    return w.reshape(shape[:axis] + (GLA_WP,) + shape[axis + 1:])


def _pack_layer(p, l):
    w_in = jnp.swapaxes(p['w_in'][l], 0, 1)
    o = 3 * SB_W + POOL_W
    sb_u = w_in[:o]
    gla = [_pad_heads(w_in[o + j * GLA_W:o + (j + 1) * GLA_W], 0) for j in range(4)]
    o += 4 * GLA_W
    lr = jnp.pad(w_in[o:o + GLA_RANK], ((0, LANE - GLA_RANK), (0, 0)))
    gates = w_in[o + GLA_RANK:]
    eye = jnp.eye(len(POOL_WINDOWS), dtype=F32)
    w_pool_bd = (eye[:, None, :, None] * p['w_pool'][l][:, :, None, :]).reshape(POOL_W, POOL_W)
    ff = p['w_ffn_out'].shape[1]
    row = lambda v: v.reshape(1, -1)
    return dict(
        w_in=jnp.concatenate([sb_u] + gla + [lr], axis=0).astype(BF),
        w_gate=gates.astype(BF),
        w_a2=jnp.pad(_pad_heads(p['w_gla_a2'][l], 1), ((0, LANE - GLA_RANK), (0, 0))).astype(BF),
        b_a=row(_pad_heads(p['b_gla_a'][l], 0)),
        gla_norm=row(_pad_heads(p['gla_norm'][l], 0)),
        w_pool=w_pool_bd.astype(BF),
        pool_scale=row(p['pool_scale'][l]),
        w_a=p['w_branch_a'][l].astype(BF),
        w_b=p['w_branch_b'][l].astype(BF),
        w_c=_pad_heads(p['w_branch_c'][l], 0).astype(BF),
        w_mix=p['w_mix_out'][l].astype(BF),
        w_xq=p['w_xq'][l].astype(BF),
        w_xo=p['w_xo'][l].astype(BF),
        w_g=p['w_ffn_in'][l][:, :ff].astype(BF),
        w_u=p['w_ffn_in'][l][:, ff:].astype(BF),
        w_o=p['w_ffn_out'][l].astype(BF),
        n_mix_pre=row(p['norm_mix_pre'][l]), n_mix_post=row(p['norm_mix_post'][l]),
        n_x_pre=row(p['norm_x_pre'][l]), n_x_post=row(p['norm_x_post'][l]),
        n_ffn_pre=row(p['norm_ffn_pre'][l]), n_ffn_post=row(p['norm_ffn_post'][l]),
    )


def _pad_state(s):
    s = jnp.swapaxes(s, -1, -2)
    return jnp.pad(s, ((0, 0), (0, 0), (0, GLA_DP - GLA_D), (0, GLA_DP - GLA_D)))


def _unpad_state(st):
    return jnp.swapaxes(st[:, :, :GLA_D, :GLA_D], -1, -2)


def _tiles(batch, seq):
    n = batch * seq
    return dict(
        proj=min(512, n),
        sb=min(256, seq),
        pool=min(1024, seq),
        gla=min(128, seq),
        merge=min(512, n),
        xattn=min(512, seq),
        ffn=min(512, n),
    )


def _layer(x, w, batch, seq, sb_attend, pool_hist, pos0, s0t, mk, mv, mem_index, kv_stack=None):
    t = _tiles(batch, seq)
    q, k, v, kb, vb, u, qkvg, log_a = _proj(x, w['n_mix_pre'], w['w_in'], w['w_a2'], w['b_a'],
                                            t['proj'], kv_stack)
    oa = sb_attend(q, kb, vb)
    ob = _pool(u, pool_hist, w['w_pool'], w['pool_scale'], batch, seq, t['pool'], pos0)
    oc, st = _gla(qkvg, log_a, w['gla_norm'], s0t, batch, seq, t['gla'])
    x = _merge(x, oa, ob, oc, w['w_gate'], w['w_a'], w['w_b'], w['w_c'], w['w_mix'],
               w['n_mix_pre'], w['n_mix_post'], t['merge'])
    x = _xattn(x, mk, mv, mem_index, w['w_xq'], w['w_xo'], w['n_x_pre'], w['n_x_post'],
               batch, seq, t['xattn'])
    x = _ffn(x, w['w_g'], w['w_u'], w['w_o'], w['n_ffn_pre'], w['n_ffn_post'], t['ffn'], 256)
    return x, k, v, u, st


def kernel(x_prompt, x_sample, mem_prompt, cache_sb_k, cache_sb_v, state_pool, state_gla, cache_mem_k, cache_mem_v, w_in, w_gla_a2, b_gla_a, gla_norm, w_pool, pool_scale, w_branch_a, w_branch_b, w_branch_c, w_mix_out, mem_norm, w_xq, w_xk, w_xv, w_xo, w_ffn_in, w_ffn_out, norm_mix_pre, norm_mix_post, norm_x_pre, norm_x_post, norm_ffn_pre, norm_ffn_post):
    p = dict(w_in=w_in, w_gla_a2=w_gla_a2, b_gla_a=b_gla_a, gla_norm=gla_norm, w_pool=w_pool,
             pool_scale=pool_scale, w_branch_a=w_branch_a, w_branch_b=w_branch_b,
             w_branch_c=w_branch_c, w_mix_out=w_mix_out, w_xq=w_xq, w_xo=w_xo,
             w_ffn_in=w_ffn_in, w_ffn_out=w_ffn_out, norm_mix_pre=norm_mix_pre,
             norm_mix_post=norm_mix_post, norm_x_pre=norm_x_pre, norm_x_post=norm_x_post,
             norm_ffn_pre=norm_ffn_pre, norm_ffn_post=norm_ffn_post)
    depth = w_in.shape[0]
    bp, tp, d = x_prompt.shape
    bs, ts, _ = x_sample.shape
    mem_len = mem_prompt.shape[1]
    past = cache_sb_k.shape[2]

    mk_all, mv_all = _memkv(mem_prompt.reshape(bp * mem_len, d), mem_norm.reshape(depth, 1, d),
                            w_xk, w_xv, min(512, bp * mem_len))
    mk_p = mk_all.reshape(depth * bp, mem_len, d)
    mv_p = mv_all.reshape(depth * bp, mem_len, d)
    mk_s = cache_mem_k.reshape(depth * bs, mem_len, d)
    mv_s = cache_mem_v.reshape(depth * bs, mem_len, d)
    ck = jnp.transpose(cache_sb_k, (0, 1, 3, 4, 2)).reshape(depth, bs, SB_W, past)
    cv = jnp.transpose(cache_sb_v, (0, 1, 3, 4, 2)).reshape(depth, bs, SB_W, past)

    yp = x_prompt.reshape(bp * tp, d)
    ys = x_sample.reshape(bs * ts, d)
    hist_zero = jnp.zeros((bp, HIST_PAD, POOL_W), F32)
    state_zero = jnp.zeros((bp, GLA_HEADS, GLA_DP, GLA_DP), F32)
    tiles_p = _tiles(bp, tp)
    kv_t = None
    poolp, glap, kss, vss, pools, glas = ([] for _ in range(6))
    for l in range(depth):
        w = _pack_layer(p, l)
        yp, k, v, u, st = _layer(
            yp, w, bp, tp,
            functools.partial(_sb_prompt, batch=bp, seq=tp, blk=tiles_p['sb'],
                              nq=2 if tp % (2 * tiles_p['sb']) == 0 else 1),
            hist_zero, 0, state_zero, mk_p, mv_p, lambda b, l=l: l * bp + b,
            kv_stack=(l, depth, bp, tp, kv_t))
        kv_t = (k, v)
        poolp.append(u.reshape(bp, tp, POOL_W)[:, tp - POOL_HIST:])
        glap.append(_unpad_state(st))

        hist = jnp.pad(state_pool[l], ((0, 0), (HIST_PAD - POOL_HIST, 0), (0, 0)))
        ys, k, v, u, st = _layer(
            ys, w, bs, ts,
            functools.partial(_sb_sample, cache_k=ck, cache_v=cv, layer=l, batch=bs, seq=ts,
                              blk=min(256, past)),
            hist, past, _pad_state(state_gla[l]), mk_s, mv_s, lambda b, l=l: l * bs + b)
        kss.append(k.reshape(bs, ts, SB_HEADS, SB_DIM))
        vss.append(v.reshape(bs, ts, SB_HEADS, SB_DIM))
        full = jnp.concatenate([state_pool[l], u.reshape(bs, ts, POOL_W)], axis=1)
        pools.append(full[:, -POOL_HIST:])
        glas.append(_unpad_state(st))

    kp, vp = (jnp.transpose(a.reshape(depth, bp, SB_HEADS, SB_DIM, tp), (0, 1, 4, 2, 3)) for a in kv_t)
    return (yp.reshape(bp, tp, d), ys.reshape(bs, ts, d),
            kp, vp, jnp.stack(poolp), jnp.stack(glap),
            mk_all.reshape(depth, bp, mem_len, X_HEADS, d // X_HEADS),
            mv_all.reshape(depth, bp, mem_len, X_HEADS, d // X_HEADS),
            jnp.stack(kss), jnp.stack(vss), jnp.stack(pools), jnp.stack(glas))
```

```python
import functools

import numpy as np
import jax
import jax.numpy as jnp
from jax import lax
from jax.experimental import pallas as pl
from jax.experimental.pallas import tpu as pltpu

BF = jnp.bfloat16
F32 = jnp.float32
EPS = 1e-6
LOG2_E = 1.4426950408889634

LANE = 128
VMEM_LIMIT = 56 * 1024 * 1024

SB_HEADS = 6
SB_DIM = 64
SB_W = SB_HEADS * SB_DIM
POOL_WINDOWS = (2, 4, 8, 16)
POOL_GDIM = 64
POOL_W = len(POOL_WINDOWS) * POOL_GDIM
POOL_HIST = max(POOL_WINDOWS) - 1
HIST_PAD = 16
assert all(w & (w - 1) == 0 for w in POOL_WINDOWS) and 2 * POOL_GDIM == LANE and POOL_HIST <= HIST_PAD
GLA_HEADS = 4
GLA_D = 96
GLA_DP = LANE
GLA_W = GLA_HEADS * GLA_D
GLA_WP = GLA_HEADS * GLA_DP
GLA_RANK = 16
GLA_TAU = 16.0
X_HEADS = 4
XATTN_SHORT_SEQ = 128

C_Q = 0
C_K = C_Q + SB_W
C_V = C_K + SB_W
C_U = C_V + SB_W
C_GLA = C_U + POOL_W
C_LR = C_GLA + 4 * GLA_WP
C_END = C_LR + LANE


def _cparams(sem):
    return pltpu.CompilerParams(dimension_semantics=sem, vmem_limit_bytes=VMEM_LIMIT)


def _rms(x, g):
    return x * lax.rsqrt(jnp.mean(x * x, axis=-1, keepdims=True) + EPS) * g


def _softplus_neg_abs(z):
    return jnp.log(1.0 + jnp.exp(-jnp.abs(z)))


def _sigmoid(z):
    return 0.5 * jnp.tanh(0.5 * z) + 0.5


def _dot(a, b):
    return jnp.dot(a, b, preferred_element_type=F32)


def _dot_nt(a, b):
    return lax.dot_general(a, b, (((1,), (1,)), ((), ())), preferred_element_type=F32)


def _dot_tn(a, b):
    return lax.dot_general(a, b, (((0,), (0,)), ((), ())), preferred_element_type=F32)


def _const_spec(shape):
    zeros = (0,) * len(shape)
    return pl.BlockSpec(shape, lambda *_: zeros, pipeline_mode=pl.Buffered(1))


def _proj_kernel(*refs, n_alias, transposed_kv):
    x_ref, g_ref, w_ref, wa_ref, ba_ref = refs[:N_PROJ_IN]
    q_ref, k_ref, v_ref, kb_ref, vb_ref, u_ref, gla_ref, la_ref = refs[N_PROJ_IN + n_alias:]
    h = _rms(x_ref[...], g_ref[...]).astype(BF)

    def mm(lo, width):
        return _dot_nt(h, w_ref[lo:lo + width, :])

    q_ref[...] = (mm(C_Q, SB_W) * (SB_DIM ** -0.5)).astype(BF)
    for lo, f32_ref, bf_ref in ((C_K, k_ref, kb_ref), (C_V, v_ref, vb_ref)):
        kv = mm(lo, SB_W)
        bf_ref[...] = kv.astype(BF)
        if transposed_kv is None:
            f32_ref[...] = kv
        else:
            for l in range(f32_ref.shape[0]):
                f32_ref[l, 0] = kv.T if l == transposed_kv else jnp.zeros((SB_W, kv.shape[0]), F32)
    u_ref[...] = mm(C_U, POOL_W)
    for j in range(4):
        gla_ref[:, j * GLA_WP:(j + 1) * GLA_WP] = mm(C_GLA + j * GLA_WP, GLA_WP).astype(BF)
    xa = _dot(mm(C_LR, LANE).astype(BF), wa_ref[...]) + ba_ref[...]
    la_ref[...] = ((jnp.minimum(xa, 0.0) - _softplus_neg_abs(xa)) * (LOG2_E / GLA_TAU)).astype(BF)


N_PROJ_IN = 5


def _proj(x, g, w, wa, ba, tm, kv_stack=None):
    n, d = x.shape
    wtot = w.shape[0]
    row = lambda width: pl.BlockSpec((tm, width), lambda i: (i, 0))
    if kv_stack is None:
        kv_spec, kv_shape, extra, aliases = row(SB_W), (n, SB_W), [], {}
    else:
        layer, depth, batch, seq, bufs = kv_stack
        nt = seq // tm
        kv_shape = (depth, batch, SB_W, seq)
        if bufs is None:
            kv_spec = pl.BlockSpec((depth, 1, SB_W, tm), lambda i: (0, i // nt, 0, i % nt))
            kv_slot = layer
        else:
            kv_spec = pl.BlockSpec((1, 1, SB_W, tm), lambda i: (layer, i // nt, 0, i % nt))
            kv_slot = 0
        extra = [] if bufs is None else list(bufs)
        aliases = {N_PROJ_IN + j: 1 + j for j in range(len(extra))}
    outs = [
        (row(SB_W), (n, SB_W), BF), (kv_spec, kv_shape, F32), (kv_spec, kv_shape, F32),
        (row(SB_W), (n, SB_W), BF), (row(SB_W), (n, SB_W), BF), (row(POOL_W), (n, POOL_W), F32),
        (row(4 * GLA_WP), (n, 4 * GLA_WP), BF), (row(GLA_WP), (n, GLA_WP), BF),
    ]
    return pl.pallas_call(
        functools.partial(_proj_kernel, n_alias=len(extra),
                          transposed_kv=None if kv_stack is None else kv_slot),
        grid=(n // tm,),
        in_specs=[row(d), _const_spec((1, d)), _const_spec((wtot, d)),
                  _const_spec((LANE, GLA_WP)), _const_spec((1, GLA_WP))]
                 + [pl.BlockSpec(memory_space=pl.ANY)] * len(extra),
        out_specs=[spec for spec, _, _ in outs],
        out_shape=[jax.ShapeDtypeStruct(shape, dt) for _, shape, dt in outs],
        input_output_aliases=aliases,
        compiler_params=_cparams(("parallel",)),
        name="in_proj",
    )(x, g, w, wa, ba, *extra)


def _upper_ones(n):
    r = lax.broadcasted_iota(jnp.int32, (n, n), 0)
    c = lax.broadcasted_iota(jnp.int32, (n, n), 1)
    return jnp.where(r >= c, 1.0, 0.0).astype(BF)


def _strict_causal(n):
    r = lax.broadcasted_iota(jnp.int32, (n, n), 0)
    c = lax.broadcasted_iota(jnp.int32, (n, n), 1)
    return c < r


def _sb_weights(z, carry, upper, mask):
    return _sb_finish(z, _dot(_sb_softplus(z, mask), upper), carry, mask)


def _sb_softplus(z, mask):
    zb = z.astype(BF)
    t = jnp.maximum(zb, 0.0) + _softplus_neg_abs(zb)
    return t if mask is None else jnp.where(mask, t, jnp.zeros_like(t))


def _sb_finish(z, cum, carry, mask):
    w = jnp.exp(z - cum - carry)
    if mask is not None:
        w = jnp.where(mask, w, 0.0)
    return w.astype(BF), carry + cum[:, 0:1]


SB_SKIP = "skip"
SB_LAG = 1


def _sb_prompt_kernel(q_ref, k_ref, v_ref, o_ref, acc_ref, *, blk, nq):
    first = pl.program_id(1) * nq
    pair_w = 2 * SB_DIM
    n_pairs = SB_HEADS // 2
    upper = _upper_ones(blk)
    causal = _strict_causal(blk)
    low = lax.broadcasted_iota(jnp.int32, (blk, pair_w), 1) < SB_DIM
    zero = jnp.zeros((blk, pair_w), BF)

    def split(x):
        return jnp.where(low, x, zero), jnp.where(low, zero, x)

    q_heads = []
    for s in range(nq):
        heads = []
        for p in range(n_pairs):
            heads.extend(split(q_ref[s * blk:(s + 1) * blk, p * pair_w:(p + 1) * pair_w]))
        q_heads.append(heads)
    acc_ref[...] = jnp.zeros(acc_ref.shape, F32)

    def visit(kb, carries, masks):
        start = pl.multiple_of(kb * blk, blk)
        out = list(carries)
        pairs = [slice(p * pair_w, (p + 1) * pair_w) for p in range(n_pairs)]
        live = [(s, 2 * p + e) for p in range(n_pairs) for s in range(nq)
                if masks[s] is not SB_SKIP for e in range(2)]
        k_pairs = [k_ref[pl.ds(start, blk), ps] for ps in pairs]
        v_pairs = [jnp.concatenate(split(v_ref[pl.ds(start, blk), ps]), axis=0) for ps in pairs]
        z, cum, w = {}, {}, {}
        for i in range(len(live) + 2 * SB_LAG):
            if i < len(live):
                s, h = live[i]
                z[i] = _dot_nt(q_heads[s][h], k_pairs[h // 2])
            j = i - SB_LAG
            if 0 <= j < len(live):
                cum[j] = _dot(_sb_softplus(z[j], masks[live[j][0]]), upper)
            j = i - 2 * SB_LAG
            if 0 <= j < len(live):
                s, h = live[j]
                c = s * SB_HEADS + h
                w[j], out[c] = _sb_finish(z[j], cum[j], carries[c], masks[s])
                if h % 2 == 1:
                    w_pair = jnp.concatenate([w[j - 1], w[j]], axis=1)
                    ps = pairs[h // 2]
                    acc_ref[s * blk:(s + 1) * blk, ps] += _dot(w_pair, v_pairs[h // 2])
        return tuple(out)

    carries = tuple(jnp.zeros((blk, 1), F32) for _ in range(nq * SB_HEADS))
    for d in reversed(range(nq)):
        masks = tuple(SB_SKIP if d > s else (causal if d == s else None) for s in range(nq))
        carries = visit(first + d, carries, masks)
    lax.fori_loop(0, first, lambda j, c: visit(first - 1 - j, c, (None,) * nq), carries)
    o_ref[...] = acc_ref[...].astype(BF)


def _sb_prompt(q, kb, vb, batch, seq, blk, nq):
    nblk = seq // (blk * nq)
    return pl.pallas_call(
        functools.partial(_sb_prompt_kernel, blk=blk, nq=nq),
        grid=(batch, nblk),
        in_specs=[
            pl.BlockSpec((nq * blk, SB_W), lambda b, i: (b * nblk + i, 0)),
            pl.BlockSpec((seq, SB_W), lambda b, i: (b, 0)),
            pl.BlockSpec((seq, SB_W), lambda b, i: (b, 0)),
        ],
        out_specs=pl.BlockSpec((nq * blk, SB_W), lambda b, i: (b * nblk + i, 0)),
        out_shape=jax.ShapeDtypeStruct((batch * seq, SB_W), BF),
        scratch_shapes=[pltpu.VMEM((nq * blk, SB_W), F32)],
        compiler_params=_cparams(("parallel", "arbitrary")),
        name="sb_prompt",
    )(q, kb, vb)


def _sb_sample_kernel(q_ref, kn_ref, vn_ref, kc_ref, vc_ref, o_ref, *, seq, past, blk):
    pair_w = 2 * SB_DIM
    n_pairs = SB_HEADS // 2
    pairs = [slice(p * pair_w, (p + 1) * pair_w) for p in range(n_pairs)]

    def split(x, axis):
        low = lax.broadcasted_iota(jnp.int32, x.shape, axis) < SB_DIM
        zero = jnp.zeros(x.shape, x.dtype)
        return jnp.where(low, x, zero), jnp.where(low, zero, x)

    q_heads = []
    for ps in pairs:
        q_heads.extend(split(q_ref[:, ps], 1))
    r = lax.broadcasted_iota(jnp.int32, (SB_HEADS * seq, seq), 0) % seq
    causal = lax.broadcasted_iota(jnp.int32, (SB_HEADS * seq, seq), 1) < r

    def visit(k_pairs, v_pairs, transposed, carry, acc, upper, mask):
        qk = _dot if transposed else _dot_nt
        pv = _dot_nt if transposed else _dot
        z = jnp.concatenate([qk(q_heads[h], k_pairs[h // 2]) for h in range(SB_HEADS)], axis=0)
        w, carry = _sb_weights(z, carry, upper, mask)
        out = []
        for p in range(n_pairs):
            v_lo, v_hi = split(v_pairs[p], 0 if transposed else 1)
            w_lo = w[(2 * p) * seq:(2 * p + 1) * seq]
            w_hi = w[(2 * p + 1) * seq:(2 * p + 2) * seq]
            out.append(acc[p] + pv(w_lo, v_lo) + pv(w_hi, v_hi))
        return carry, out

    carry = jnp.zeros((SB_HEADS * seq, 1), F32)
    acc = [jnp.zeros((seq, pair_w), F32) for _ in pairs]
    carry, acc = visit([kn_ref[:, ps] for ps in pairs], [vn_ref[:, ps] for ps in pairs], False,
                       carry, acc, _upper_ones(seq), causal)
    upper_past = _upper_ones(blk)
    for j in reversed(range(past // blk)):
        cols = slice(j * blk, (j + 1) * blk)
        carry, acc = visit([kc_ref[0, 0, ps, cols].astype(BF) for ps in pairs],
                           [vc_ref[0, 0, ps, cols].astype(BF) for ps in pairs], True,
                           carry, acc, upper_past, None)
    for p, ps in enumerate(pairs):
        o_ref[:, ps] = acc[p].astype(BF)


def _sb_sample(q, kb, vb, cache_k, cache_v, layer, batch, seq, blk):
    past = cache_k.shape[3]
    new = pl.BlockSpec((seq, SB_W), lambda b: (b, 0))
    cache = pl.BlockSpec((1, 1, SB_W, past), lambda b: (layer, b, 0, 0))
    return pl.pallas_call(
        functools.partial(_sb_sample_kernel, seq=seq, past=past, blk=blk),
        grid=(batch,),
        in_specs=[new, new, new, cache, cache],
        out_specs=new,
        out_shape=jax.ShapeDtypeStruct((batch * seq, SB_W), BF),
        compiler_params=_cparams(("parallel",)),
        name="sb_sample",
    )(q, kb, vb, cache_k, cache_v)


def _pool_kernel(u_ref, hist_ref, w_ref, scale_ref, o_ref, buf_ref, *, tm, pos0):
    i = pl.program_id(1)

    @pl.when(i == 0)
    def _():
        buf_ref[0:HIST_PAD, :] = hist_ref[0]

    @pl.when(i > 0)
    def _():
        buf_ref[0:HIST_PAD, :] = buf_ref[tm:tm + HIST_PAD, :]

    u = u_ref[...]
    buf_ref[HIST_PAD:HIST_PAD + tm, :] = u

    rows = tm + HIST_PAD
    low = lax.broadcasted_iota(jnp.int32, (tm, LANE), 1) < POOL_GDIM
    pos1 = pos0 + i * tm + lax.broadcasted_iota(jnp.int32, (tm, LANE), 0) + 1
    pooled = []
    for tile in range(POOL_W // LANE):
        cols = slice(tile * LANE, (tile + 1) * LANE)
        w_lo, w_hi = POOL_WINDOWS[2 * tile], POOL_WINDOWS[2 * tile + 1]
        s = buf_ref[0:rows, cols]
        sums, w = {}, 1
        while w < w_hi:
            s = s + pltpu.roll(s, w, 0)
            w *= 2
            sums[w] = s
        win_sum = jnp.where(low, sums[w_lo][HIST_PAD:], sums[w_hi][HIST_PAD:])
        cnt = jnp.minimum(pos1, jnp.where(low, w_lo, w_hi)).astype(F32)
        pooled.append(win_sum / cnt - u[:, cols])
    pooled = jnp.concatenate(pooled, axis=1)
    o_ref[...] = (_dot(pooled.astype(BF), w_ref[...]) * scale_ref[...]).astype(BF)


def _pool(u, hist, w_bd, scale, batch, seq, tm, pos0):
    nt = seq // tm
    return pl.pallas_call(
        functools.partial(_pool_kernel, tm=tm, pos0=pos0),
        grid=(batch, nt),
        in_specs=[
            pl.BlockSpec((tm, POOL_W), lambda b, i: (b * nt + i, 0)),
            pl.BlockSpec((1, HIST_PAD, POOL_W), lambda b, i: (b, 0, 0)),
            _const_spec((POOL_W, POOL_W)),
            _const_spec((1, POOL_W)),
        ],
        out_specs=pl.BlockSpec((tm, POOL_W), lambda b, i: (b * nt + i, 0)),
        out_shape=jax.ShapeDtypeStruct((batch * seq, POOL_W), BF),
        scratch_shapes=[pltpu.VMEM((tm + HIST_PAD, POOL_W), F32)],
        compiler_params=_cparams(("parallel", "arbitrary")),
        name="pool_mix",
    )(u, hist, w_bd, scale)


def _gla_levels(rows):
    out, s = [], 1
    while s < rows:
        out.append(s)
        s *= 2
    return out


def _gla_exponent_matrix(rows):
    t = np.arange(rows)[:, None]
    j = np.arange(rows)[None, :]
    blocks = [j <= t]
    for s in _gla_levels(rows)[1:]:
        start = (t // s) * s
        later = (t // s) % 2 == 1
        blocks.append(np.where(later, (j >= start) & (j <= t), (j > t) & (j <= start + s - 1)))
    return np.concatenate(blocks, axis=0).astype(np.float32)


def _gla_later_mask(rows):
    t = np.arange(rows)[:, None]
    blocks = [np.broadcast_to((t // s) % 2 == 1, (rows, GLA_WP)) for s in _gla_levels(rows)]
    return np.concatenate(blocks, axis=0).astype(np.float32)


def _gla_kernel(qkvg_ref, la_ref, em_ref, lm_ref, gn_ref, s0_ref, o_ref, st_ref, *, rows, sub):
    @pl.when(pl.program_id(1) == 0)
    def _():
        st_ref[...] = s0_ref[...]

    for j in range(sub):
        _gla_tile(qkvg_ref, la_ref, em_ref, lm_ref, gn_ref, o_ref, st_ref,
                  slice(j * rows, (j + 1) * rows), rows)


def _gla_tile(qkvg_ref, la_ref, em_ref, lm_ref, gn_ref, o_ref, st_ref, rs, rows):
    expo = _dot(em_ref[...], la_ref[rs, :])

    qb = qkvg_ref[rs, 0:GLA_WP]
    kb = qkvg_ref[rs, GLA_WP:2 * GLA_WP]
    vb = qkvg_ref[rs, 2 * GLA_WP:3 * GLA_WP]
    gate = qkvg_ref[rs, 3 * GLA_WP:4 * GLA_WP].astype(F32)

    b = expo[0:rows]
    b_last = b[rows - 1:rows]
    q_dec = qb * jnp.exp2(b).astype(BF)
    k_dec = kb * jnp.exp2(b_last - b).astype(BF)

    row = lax.broadcasted_iota(jnp.int32, (rows, GLA_WP), 0)
    r = lax.broadcasted_iota(jnp.int32, (rows, rows), 0)
    c = lax.broadcasted_iota(jnp.int32, (rows, rows), 1)
    heads = [slice(h * GLA_DP, (h + 1) * GLA_DP) for h in range(GLA_HEADS)]

    att = [jnp.where(r == c, _dot_nt(qb[:, hs], kb[:, hs]), 0.0) for hs in heads]
    for n, s in enumerate(_gla_levels(rows)):
        shift = s.bit_length() - 1
        rb = r >> shift
        pair = ((rb & 1) == 1) & ((c >> shift) == rb - 1)
        if s == 1:
            e = jnp.where((row & 1) == 1, la_ref[rs, :].astype(F32), 0.0)
        else:
            e = expo[n * rows:(n + 1) * rows]
        later = lm_ref[n * rows:(n + 1) * rows] > 0
        x_lvl = jnp.where(later, qb, kb) * jnp.exp2(e).astype(BF)
        for h, hs in enumerate(heads):
            att[h] = jnp.where(pair, _dot_nt(x_lvl[:, hs], x_lvl[:, hs]), att[h])

    for h, hs in enumerate(heads):
        st = st_ref[0, h]
        o = (_dot(att[h].astype(BF), vb[:, hs]) + _dot_nt(q_dec[:, hs], st.astype(BF))) * (GLA_D ** -0.5)
        ms = jnp.sum(o * o, axis=-1, keepdims=True) * (1.0 / GLA_D)
        y = o * lax.rsqrt(ms + EPS) * gn_ref[:, hs]
        gh = gate[:, hs]
        o_ref[rs, hs] = (y * (gh * _sigmoid(gh))).astype(BF)
        st_ref[0, h] = st * jnp.exp2(b_last[:, hs]) + _dot_tn(vb[:, hs], k_dec[:, hs])


def _gla(qkvg, log_a, gn, s0t, batch, seq, rows, sub):
    nt = seq // (rows * sub)
    em = jnp.asarray(_gla_exponent_matrix(rows), BF)
    lm = jnp.asarray(_gla_later_mask(rows), BF)
    state = pl.BlockSpec((1, GLA_HEADS, GLA_DP, GLA_DP), lambda b, i: (b, 0, 0, 0))
    return pl.pallas_call(
        functools.partial(_gla_kernel, rows=rows, sub=sub),
        grid=(batch, nt),
        in_specs=[
            pl.BlockSpec((rows * sub, 4 * GLA_WP), lambda b, i: (b * nt + i, 0)),
            pl.BlockSpec((rows * sub, GLA_WP), lambda b, i: (b * nt + i, 0)),
            _const_spec(em.shape),
            _const_spec(lm.shape),
            _const_spec((1, GLA_WP)),
            state,
        ],
        out_specs=[pl.BlockSpec((rows * sub, GLA_WP), lambda b, i: (b * nt + i, 0)), state],
        out_shape=[jax.ShapeDtypeStruct((batch * seq, GLA_WP), BF),
                   jax.ShapeDtypeStruct((batch, GLA_HEADS, GLA_DP, GLA_DP), F32)],
        compiler_params=_cparams(("parallel", "arbitrary")),
        name="gla",
    )(qkvg, log_a, em, lm, gn, s0t)


def _merge_kernel(x_ref, oa_ref, ob_ref, oc_ref, wg_ref, wa_ref, wb_ref, wc_ref, wo_ref,
                  gpre_ref, gpost_ref, y_ref, *, d_model):
    x = x_ref[...]
    h = _rms(x, gpre_ref[...]).astype(BF)
    merged = None
    for j, (o_ref, w_ref) in enumerate(((oa_ref, wa_ref), (ob_ref, wb_ref), (oc_ref, wc_ref))):
        gate = _sigmoid(_dot_nt(h, wg_ref[j * d_model:(j + 1) * d_model, :]))
        term = gate * _dot(o_ref[...], w_ref[...])
        merged = term if merged is None else merged + term
    mix = _dot(merged.astype(BF), wo_ref[...])
    y_ref[...] = x + _rms(mix, gpost_ref[...])


def _merge(x, oa, ob, oc, wg, wa, wb, wc, wo, gpre, gpost, tm):
    n, d = x.shape
    row = lambda width: pl.BlockSpec((tm, width), lambda i: (i, 0))
    return pl.pallas_call(
        functools.partial(_merge_kernel, d_model=d),
        grid=(n // tm,),
        in_specs=[row(d), row(SB_W), row(POOL_W), row(GLA_WP),
                  _const_spec(wg.shape), _const_spec(wa.shape), _const_spec(wb.shape),
                  _const_spec(wc.shape), _const_spec(wo.shape),
                  _const_spec((1, d)), _const_spec((1, d))],
        out_specs=row(d),
        out_shape=jax.ShapeDtypeStruct((n, d), F32),
        compiler_params=_cparams(("parallel",)),
        name="merge_out",
    )(x, oa, ob, oc, wg, wa, wb, wc, wo, gpre, gpost)


def _xattn_kernel(x_ref, mk_ref, mv_ref, wq_ref, wo_ref, gpre_ref, gpost_ref, y_ref, *, d_model):
    x = x_ref[...]
    q = _dot(_rms(x, gpre_ref[...]).astype(BF), wq_ref[...]).astype(BF)
    hd = d_model // X_HEADS
    heads = [slice(h * hd, (h + 1) * hd) for h in range(X_HEADS)]
    s, o = {}, {}
    for i in range(X_HEADS + 1):
        if i < X_HEADS:
            s[i] = _dot_nt(q[:, heads[i]], mk_ref[0, :, heads[i]].astype(BF)) * (hd ** -0.5)
        j = i - 1
        if 0 <= j < X_HEADS:
            p = jnp.exp(s[j] - jnp.max(s[j], axis=-1, keepdims=True))
            p = p / jnp.sum(p, axis=-1, keepdims=True)
            o[j] = _dot(p.astype(BF), mv_ref[0, :, heads[j]].astype(BF)).astype(BF)
    attn = jnp.concatenate([o[h] for h in range(X_HEADS)], axis=-1)
    y_ref[...] = x + _rms(_dot(attn, wo_ref[...]), gpost_ref[...])


def _xattn_short_kernel(x_ref, mk_ref, mv_ref, wq_ref, wo_ref, gpre_ref, gpost_ref, y_ref,
                        q_scr, o_scr, *, d_model, seq):
    b = pl.program_id(0)

    @pl.when(b == 0)
    def _():
        q_scr[...] = _dot(_rms(x_ref[...], gpre_ref[...]).astype(BF), wq_ref[...]).astype(BF)

    rows = pl.ds(pl.multiple_of(b * seq, seq), seq)
    hd = d_model // X_HEADS
    for h in range(X_HEADS):
        hs = slice(h * hd, (h + 1) * hd)
        s = _dot_nt(q_scr[rows, hs], mk_ref[0, :, hs].astype(BF)) * (hd ** -0.5)
        p = jnp.exp(s - jnp.max(s, axis=-1, keepdims=True))
        p = p / jnp.sum(p, axis=-1, keepdims=True)
        o_scr[rows, hs] = _dot(p.astype(BF), mv_ref[0, :, hs].astype(BF)).astype(BF)

    @pl.when(b == pl.num_programs(0) - 1)
    def _():
        y_ref[...] = x_ref[...] + _rms(_dot(o_scr[...], wo_ref[...]), gpost_ref[...])


def _xattn_short(x, mk, mv, mem_index, wq, wo, gpre, gpost, batch, seq):
    n, d = x.shape
    mem_len = mk.shape[-2]
    mem = pl.BlockSpec((1, mem_len, d), lambda b: (mem_index(b), 0, 0))
    return pl.pallas_call(
        functools.partial(_xattn_short_kernel, d_model=d, seq=seq),
        grid=(batch,),
        in_specs=[_const_spec((n, d)), mem, mem, _const_spec((d, d)), _const_spec((d, d)),
                  _const_spec((1, d)), _const_spec((1, d))],
        out_specs=pl.BlockSpec((n, d), lambda b: (0, 0)),
        out_shape=jax.ShapeDtypeStruct((n, d), F32),
        scratch_shapes=[pltpu.VMEM((n, d), BF), pltpu.VMEM((n, d), BF)],
        compiler_params=_cparams(("arbitrary",)),
        name="cross_attn_short",
    )(x, mk, mv, wq, wo, gpre, gpost)


def _xattn(x, mk, mv, mem_index, wq, wo, gpre, gpost, batch, seq, tm):
    n, d = x.shape
    if seq < XATTN_SHORT_SEQ:
        return _xattn_short(x, mk, mv, mem_index, wq, wo, gpre, gpost, batch, seq)
    nt = seq // tm
    mem_len = mk.shape[-2]
    row = pl.BlockSpec((tm, d), lambda b, i: (b * nt + i, 0))
    mem = pl.BlockSpec((1, mem_len, d), lambda b, i: (mem_index(b), 0, 0))
    return pl.pallas_call(
        functools.partial(_xattn_kernel, d_model=d),
        grid=(batch, nt),
        in_specs=[row, mem, mem, _const_spec((d, d)), _const_spec((d, d)),
                  _const_spec((1, d)), _const_spec((1, d))],
        out_specs=row,
        out_shape=jax.ShapeDtypeStruct((n, d), F32),
        compiler_params=_cparams(("parallel", "parallel")),
        name="cross_attn",
    )(x, mk, mv, wq, wo, gpre, gpost)


def _ffn_kernel(x_ref, wg_ref, wu_ref, wo_ref, gpre_ref, gpost_ref, y_ref, *, chunk):
    x = x_ref[...]
    h = _rms(x, gpre_ref[...]).astype(BF)
    ff = wg_ref.shape[1]
    acc = jnp.zeros(x.shape, F32)
    for j in range(ff // chunk):
        cs = slice(j * chunk, (j + 1) * chunk)
        g = _dot(h, wg_ref[:, cs])
        u = _dot(h, wu_ref[:, cs])
        act = (g * _sigmoid(g) * u).astype(BF)
        acc = acc + _dot(act, wo_ref[cs, :])
    y_ref[...] = x + _rms(acc, gpost_ref[...])


def _ffn(x, wg, wu, wo, gpre, gpost, tm, chunk):
    n, d = x.shape
    row = pl.BlockSpec((tm, d), lambda i: (i, 0))
    return pl.pallas_call(
        functools.partial(_ffn_kernel, chunk=chunk),
        grid=(n // tm,),
        in_specs=[row, _const_spec(wg.shape), _const_spec(wu.shape), _const_spec(wo.shape),
                  _const_spec((1, d)), _const_spec((1, d))],
        out_specs=row,
        out_shape=jax.ShapeDtypeStruct((n, d), F32),
        compiler_params=_cparams(("parallel",)),
        name="ffn",
    )(x, wg, wu, wo, gpre, gpost)


def _memkv_kernel(m_ref, g_ref, wk_ref, wv_ref, k_ref, v_ref):
    m = _rms(m_ref[...], g_ref[0]).astype(BF)
    k_ref[0] = _dot(m, wk_ref[0].astype(BF))
    v_ref[0] = _dot(m, wv_ref[0].astype(BF))


def _memkv(mem, g, wk, wv, tm):
    depth, d, _ = wk.shape
    n = mem.shape[0]
    w = pl.BlockSpec((1, d, d), lambda l, i: (l, 0, 0))
    out = pl.BlockSpec((1, tm, d), lambda l, i: (l, i, 0))
    return pl.pallas_call(
        _memkv_kernel,
        grid=(depth, n // tm),
        in_specs=[pl.BlockSpec((tm, d), lambda l, i: (i, 0)),
                  pl.BlockSpec((1, 1, d), lambda l, i: (l, 0, 0)), w, w],
        out_specs=[out, out],
        out_shape=[jax.ShapeDtypeStruct((depth, n, d), F32)] * 2,
        compiler_params=_cparams(("parallel", "parallel")),
        name="mem_kv",
    )(mem, g, wk, wv)


def _pad_heads(w, axis):
    shape = w.shape
    w = w.reshape(shape[:axis] + (GLA_HEADS, GLA_D) + shape[axis + 1:])
    pad = [(0, 0)] * w.ndim
    pad[axis + 1] = (0, GLA_DP - GLA_D)
    w = jnp.pad(w, pad)
    return w.reshape(shape[:axis] + (GLA_WP,) + shape[axis + 1:])


def _pack_layer(p, l):
    w_in = jnp.swapaxes(p['w_in'][l], 0, 1)
    o = 3 * SB_W + POOL_W
    sb_u = w_in[:o]
    gla = [_pad_heads(w_in[o + j * GLA_W:o + (j + 1) * GLA_W], 0) for j in range(4)]
    o += 4 * GLA_W
    lr = jnp.pad(w_in[o:o + GLA_RANK], ((0, LANE - GLA_RANK), (0, 0)))
    gates = w_in[o + GLA_RANK:]
    eye = jnp.eye(len(POOL_WINDOWS), dtype=F32)
    w_pool_bd = (eye[:, None, :, None] * p['w_pool'][l][:, :, None, :]).reshape(POOL_W, POOL_W)
    ff = p['w_ffn_out'].shape[1]
    row = lambda v: v.reshape(1, -1)
    return dict(
        w_in=jnp.concatenate([sb_u] + gla + [lr], axis=0).astype(BF),
        w_gate=gates.astype(BF),
        w_a2=jnp.pad(_pad_heads(p['w_gla_a2'][l], 1), ((0, LANE - GLA_RANK), (0, 0))).astype(BF),
        b_a=row(_pad_heads(p['b_gla_a'][l], 0)),
        gla_norm=row(_pad_heads(p['gla_norm'][l], 0)),
        w_pool=w_pool_bd.astype(BF),
        pool_scale=row(p['pool_scale'][l]),
        w_a=p['w_branch_a'][l].astype(BF),
        w_b=p['w_branch_b'][l].astype(BF),
        w_c=_pad_heads(p['w_branch_c'][l], 0).astype(BF),
        w_mix=p['w_mix_out'][l].astype(BF),
        w_xq=p['w_xq'][l].astype(BF),
        w_xo=p['w_xo'][l].astype(BF),
        w_g=p['w_ffn_in'][l][:, :ff].astype(BF),
        w_u=p['w_ffn_in'][l][:, ff:].astype(BF),
        w_o=p['w_ffn_out'][l].astype(BF),
        n_mix_pre=row(p['norm_mix_pre'][l]), n_mix_post=row(p['norm_mix_post'][l]),
        n_x_pre=row(p['norm_x_pre'][l]), n_x_post=row(p['norm_x_post'][l]),
        n_ffn_pre=row(p['norm_ffn_pre'][l]), n_ffn_post=row(p['norm_ffn_post'][l]),
    )


def _pad_state(s):
    s = jnp.swapaxes(s, -1, -2)
    return jnp.pad(s, ((0, 0), (0, 0), (0, GLA_DP - GLA_D), (0, GLA_DP - GLA_D)))


def _unpad_state(st):
    return jnp.swapaxes(st[:, :, :GLA_D, :GLA_D], -1, -2)


def _tiles(batch, seq):
    n = batch * seq
    return dict(
        proj=min(512, n),
        sb=min(256, seq),
        pool=min(1024, seq),
        gla=min(128, seq),
        merge=min(1024, n),
        xattn=min(1024, seq),
        ffn=min(1024, n),
    )


def _layer(x, w, batch, seq, sb_attend, pool_hist, pos0, s0t, mk, mv, mem_index, kv_stack=None):
    t = _tiles(batch, seq)
    q, k, v, kb, vb, u, qkvg, log_a = _proj(x, w['n_mix_pre'], w['w_in'], w['w_a2'], w['b_a'],
                                            t['proj'], kv_stack)
    oa = sb_attend(q, kb, vb)
    ob = _pool(u, pool_hist, w['w_pool'], w['pool_scale'], batch, seq, t['pool'], pos0)
    oc, st = _gla(qkvg, log_a, w['gla_norm'], s0t, batch, seq, t['gla'],
                  2 if seq % (2 * t['gla']) == 0 else 1)
    x = _merge(x, oa, ob, oc, w['w_gate'], w['w_a'], w['w_b'], w['w_c'], w['w_mix'],
               w['n_mix_pre'], w['n_mix_post'], t['merge'])
    x = _xattn(x, mk, mv, mem_index, w['w_xq'], w['w_xo'], w['n_x_pre'], w['n_x_post'],
               batch, seq, t['xattn'])
    x = _ffn(x, w['w_g'], w['w_u'], w['w_o'], w['n_ffn_pre'], w['n_ffn_post'], t['ffn'], 256)
    return x, k, v, u, st


def kernel(x_prompt, x_sample, mem_prompt, cache_sb_k, cache_sb_v, state_pool, state_gla, cache_mem_k, cache_mem_v, w_in, w_gla_a2, b_gla_a, gla_norm, w_pool, pool_scale, w_branch_a, w_branch_b, w_branch_c, w_mix_out, mem_norm, w_xq, w_xk, w_xv, w_xo, w_ffn_in, w_ffn_out, norm_mix_pre, norm_mix_post, norm_x_pre, norm_x_post, norm_ffn_pre, norm_ffn_post):
    p = dict(w_in=w_in, w_gla_a2=w_gla_a2, b_gla_a=b_gla_a, gla_norm=gla_norm, w_pool=w_pool,
             pool_scale=pool_scale, w_branch_a=w_branch_a, w_branch_b=w_branch_b,
             w_branch_c=w_branch_c, w_mix_out=w_mix_out, w_xq=w_xq, w_xo=w_xo,
             w_ffn_in=w_ffn_in, w_ffn_out=w_ffn_out, norm_mix_pre=norm_mix_pre,
             norm_mix_post=norm_mix_post, norm_x_pre=norm_x_pre, norm_x_post=norm_x_post,
             norm_ffn_pre=norm_ffn_pre, norm_ffn_post=norm_ffn_post)
    depth = w_in.shape[0]
    bp, tp, d = x_prompt.shape
    bs, ts, _ = x_sample.shape
    mem_len = mem_prompt.shape[1]
    past = cache_sb_k.shape[2]

    mk_all, mv_all = _memkv(mem_prompt.reshape(bp * mem_len, d), mem_norm.reshape(depth, 1, d),
                            w_xk, w_xv, min(512, bp * mem_len))
    mk_p = mk_all.reshape(depth * bp, mem_len, d)
    mv_p = mv_all.reshape(depth * bp, mem_len, d)
    mk_s = cache_mem_k.reshape(depth * bs, mem_len, d)
    mv_s = cache_mem_v.reshape(depth * bs, mem_len, d)
    ck = jnp.transpose(cache_sb_k, (0, 1, 3, 4, 2)).reshape(depth, bs, SB_W, past)
    cv = jnp.transpose(cache_sb_v, (0, 1, 3, 4, 2)).reshape(depth, bs, SB_W, past)

    yp = x_prompt.reshape(bp * tp, d)
    ys = x_sample.reshape(bs * ts, d)
    hist_zero = jnp.zeros((bp, HIST_PAD, POOL_W), F32)
    state_zero = jnp.zeros((bp, GLA_HEADS, GLA_DP, GLA_DP), F32)
    tiles_p = _tiles(bp, tp)
    kv_t = None
    poolp, glap, kss, vss, pools, glas = ([] for _ in range(6))
    for l in range(depth):
        w = _pack_layer(p, l)
        yp, k, v, u, st = _layer(
            yp, w, bp, tp,
            functools.partial(_sb_prompt, batch=bp, seq=tp, blk=tiles_p['sb'],
                              nq=2 if tp % (2 * tiles_p['sb']) == 0 else 1),
            hist_zero, 0, state_zero, mk_p, mv_p, lambda b, l=l: l * bp + b,
            kv_stack=(l, depth, bp, tp, kv_t))
        kv_t = (k, v)
        poolp.append(u.reshape(bp, tp, POOL_W)[:, tp - POOL_HIST:])
        glap.append(_unpad_state(st))

        hist = jnp.pad(state_pool[l], ((0, 0), (HIST_PAD - POOL_HIST, 0), (0, 0)))
        ys, k, v, u, st = _layer(
            ys, w, bs, ts,
            functools.partial(_sb_sample, cache_k=ck, cache_v=cv, layer=l, batch=bs, seq=ts,
                              blk=min(256, past)),
            hist, past, _pad_state(state_gla[l]), mk_s, mv_s, lambda b, l=l: l * bs + b)
        kss.append(k.reshape(bs, ts, SB_HEADS, SB_DIM))
        vss.append(v.reshape(bs, ts, SB_HEADS, SB_DIM))
        full = jnp.concatenate([state_pool[l], u.reshape(bs, ts, POOL_W)], axis=1)
        pools.append(full[:, -POOL_HIST:])
        glas.append(_unpad_state(st))

    kp, vp = (jnp.transpose(a.reshape(depth, bp, SB_HEADS, SB_DIM, tp), (0, 1, 4, 2, 3)) for a in kv_t)
    return (yp.reshape(bp, tp, d), ys.reshape(bs, ts, d),
            kp, vp, jnp.stack(poolp), jnp.stack(glap),
            mk_all.reshape(depth, bp, mem_len, X_HEADS, d // X_HEADS),
            mv_all.reshape(depth, bp, mem_len, X_HEADS, d // X_HEADS),
            jnp.stack(kss), jnp.stack(vss), jnp.stack(pools), jnp.stack(glas))
```

```python
import functools

import numpy as np
import jax
import jax.numpy as jnp
from jax import lax
from jax.experimental import pallas as pl
from jax.experimental.pallas import tpu as pltpu

BF = jnp.bfloat16
F32 = jnp.float32
EPS = 1e-6
LOG2_E = 1.4426950408889634

LANE = 128
VMEM_LIMIT = 56 * 1024 * 1024

SB_HEADS = 6
SB_DIM = 64
SB_W = SB_HEADS * SB_DIM
POOL_WINDOWS = (2, 4, 8, 16)
POOL_GDIM = 64
POOL_W = len(POOL_WINDOWS) * POOL_GDIM
POOL_HIST = max(POOL_WINDOWS) - 1
HIST_PAD = 16
assert all(w & (w - 1) == 0 for w in POOL_WINDOWS) and 2 * POOL_GDIM == LANE and POOL_HIST <= HIST_PAD
GLA_HEADS = 4
GLA_D = 96
GLA_DP = LANE
GLA_W = GLA_HEADS * GLA_D
GLA_WP = GLA_HEADS * GLA_DP
GLA_RANK = 16
GLA_TAU = 16.0
X_HEADS = 4
XATTN_SHORT_SEQ = 128

C_Q = 0
C_K = C_Q + SB_W
C_V = C_K + SB_W
C_U = C_V + SB_W
C_GLA = C_U + POOL_W
C_LR = C_GLA + 4 * GLA_WP
C_END = C_LR + LANE


def _cparams(sem):
    return pltpu.CompilerParams(dimension_semantics=sem, vmem_limit_bytes=VMEM_LIMIT)


def _rms(x, g):
    return x * lax.rsqrt(jnp.mean(x * x, axis=-1, keepdims=True) + EPS) * g


def _softplus_neg_abs(z):
    return jnp.log(1.0 + jnp.exp(-jnp.abs(z)))


def _sigmoid(z):
    return 0.5 * jnp.tanh(0.5 * z) + 0.5


def _dot(a, b):
    return jnp.dot(a, b, preferred_element_type=F32)


def _dot_nt(a, b):
    return lax.dot_general(a, b, (((1,), (1,)), ((), ())), preferred_element_type=F32)


def _dot_tn(a, b):
    return lax.dot_general(a, b, (((0,), (0,)), ((), ())), preferred_element_type=F32)


def _const_spec(shape):
    zeros = (0,) * len(shape)
    return pl.BlockSpec(shape, lambda *_: zeros, pipeline_mode=pl.Buffered(1))


def _proj_kernel(*refs, n_alias, transposed_kv):
    x_ref, g_ref, w_ref, wa_ref, ba_ref = refs[:N_PROJ_IN]
    q_ref, k_ref, v_ref, kb_ref, vb_ref, u_ref, gla_ref, la_ref = refs[N_PROJ_IN + n_alias:]
    h = _rms(x_ref[...], g_ref[...]).astype(BF)

    def mm(lo, width):
        return _dot_nt(h, w_ref[lo:lo + width, :])

    q_ref[...] = (mm(C_Q, SB_W) * (SB_DIM ** -0.5)).astype(BF)
    for lo, f32_ref, bf_ref in ((C_K, k_ref, kb_ref), (C_V, v_ref, vb_ref)):
        kv = mm(lo, SB_W)
        bf_ref[...] = kv.astype(BF)
        if transposed_kv is None:
            f32_ref[...] = kv
        else:
            for l in range(f32_ref.shape[0]):
                f32_ref[l, 0] = kv.T if l == transposed_kv else jnp.zeros((SB_W, kv.shape[0]), F32)
    u_ref[...] = mm(C_U, POOL_W)
    for j in range(4):
        gla_ref[:, j * GLA_WP:(j + 1) * GLA_WP] = mm(C_GLA + j * GLA_WP, GLA_WP).astype(BF)
    xa = _dot(mm(C_LR, LANE).astype(BF), wa_ref[...]) + ba_ref[...]
    la_ref[...] = ((jnp.minimum(xa, 0.0) - _softplus_neg_abs(xa)) * (LOG2_E / GLA_TAU)).astype(BF)


N_PROJ_IN = 5


def _proj(x, g, w, wa, ba, tm, kv_stack=None):
    n, d = x.shape
    wtot = w.shape[0]
    row = lambda width: pl.BlockSpec((tm, width), lambda i: (i, 0))
    if kv_stack is None:
        kv_spec, kv_shape, extra, aliases = row(SB_W), (n, SB_W), [], {}
    else:
        layer, depth, batch, seq, bufs = kv_stack
        nt = seq // tm
        kv_shape = (depth, batch, SB_W, seq)
        if bufs is None:
            kv_spec = pl.BlockSpec((depth, 1, SB_W, tm), lambda i: (0, i // nt, 0, i % nt))
            kv_slot = layer
        else:
            kv_spec = pl.BlockSpec((1, 1, SB_W, tm), lambda i: (layer, i // nt, 0, i % nt))
            kv_slot = 0
        extra = [] if bufs is None else list(bufs)
        aliases = {N_PROJ_IN + j: 1 + j for j in range(len(extra))}
    outs = [
        (row(SB_W), (n, SB_W), BF), (kv_spec, kv_shape, F32), (kv_spec, kv_shape, F32),
        (row(SB_W), (n, SB_W), BF), (row(SB_W), (n, SB_W), BF), (row(POOL_W), (n, POOL_W), F32),
        (row(4 * GLA_WP), (n, 4 * GLA_WP), BF), (row(GLA_WP), (n, GLA_WP), BF),
    ]
    return pl.pallas_call(
        functools.partial(_proj_kernel, n_alias=len(extra),
                          transposed_kv=None if kv_stack is None else kv_slot),
        grid=(n // tm,),
        in_specs=[row(d), _const_spec((1, d)), _const_spec((wtot, d)),
                  _const_spec((LANE, GLA_WP)), _const_spec((1, GLA_WP))]
                 + [pl.BlockSpec(memory_space=pl.ANY)] * len(extra),
        out_specs=[spec for spec, _, _ in outs],
        out_shape=[jax.ShapeDtypeStruct(shape, dt) for _, shape, dt in outs],
        input_output_aliases=aliases,
        compiler_params=_cparams(("parallel",)),
        name="in_proj",
    )(x, g, w, wa, ba, *extra)


def _upper_ones(n):
    r = lax.broadcasted_iota(jnp.int32, (n, n), 0)
    c = lax.broadcasted_iota(jnp.int32, (n, n), 1)
    return jnp.where(r >= c, 1.0, 0.0).astype(BF)


def _strict_causal(n):
    r = lax.broadcasted_iota(jnp.int32, (n, n), 0)
    c = lax.broadcasted_iota(jnp.int32, (n, n), 1)
    return c < r


def _sb_weights(z, carry, upper, mask):
    return _sb_finish(z, _dot(_sb_softplus(z, mask), upper), carry, mask)


def _sb_softplus(z, mask):
    zb = z.astype(BF)
    t = jnp.maximum(zb, 0.0) + _softplus_neg_abs(zb)
    return t if mask is None else jnp.where(mask, t, jnp.zeros_like(t))


def _sb_finish(z, cum, carry, mask):
    w = jnp.exp(z - cum - carry)
    if mask is not None:
        w = jnp.where(mask, w, 0.0)
    return w.astype(BF), carry + cum[:, 0:1]


SB_SKIP = "skip"
SB_LAG = 1


def _sb_prompt_kernel(q_ref, k_ref, v_ref, o_ref, acc_ref, *, blk, nq):
    first = pl.program_id(1) * nq
    pair_w = 2 * SB_DIM
    n_pairs = SB_HEADS // 2
    upper = _upper_ones(blk)
    causal = _strict_causal(blk)
    low = lax.broadcasted_iota(jnp.int32, (blk, pair_w), 1) < SB_DIM
    zero = jnp.zeros((blk, pair_w), BF)

    def split(x):
        return jnp.where(low, x, zero), jnp.where(low, zero, x)

    q_heads = []
    for s in range(nq):
        heads = []
        for p in range(n_pairs):
            heads.extend(split(q_ref[s * blk:(s + 1) * blk, p * pair_w:(p + 1) * pair_w]))
        q_heads.append(heads)
    acc_ref[...] = jnp.zeros(acc_ref.shape, F32)

    def visit(kb, carries, masks):
        start = pl.multiple_of(kb * blk, blk)
        out = list(carries)
        pairs = [slice(p * pair_w, (p + 1) * pair_w) for p in range(n_pairs)]
        live = [(s, 2 * p + e) for p in range(n_pairs) for s in range(nq)
                if masks[s] is not SB_SKIP for e in range(2)]
        k_pairs = [k_ref[pl.ds(start, blk), ps] for ps in pairs]
        v_pairs = [jnp.concatenate(split(v_ref[pl.ds(start, blk), ps]), axis=0) for ps in pairs]
        z, cum, w = {}, {}, {}
        for i in range(len(live) + 2 * SB_LAG):
            if i < len(live):
                s, h = live[i]
                z[i] = _dot_nt(q_heads[s][h], k_pairs[h // 2])
            j = i - SB_LAG
            if 0 <= j < len(live):
                cum[j] = _dot(_sb_softplus(z[j], masks[live[j][0]]), upper)
            j = i - 2 * SB_LAG
            if 0 <= j < len(live):
                s, h = live[j]
                c = s * SB_HEADS + h
                w[j], out[c] = _sb_finish(z[j], cum[j], carries[c], masks[s])
                if h % 2 == 1:
                    w_pair = jnp.concatenate([w[j - 1], w[j]], axis=1)
                    ps = pairs[h // 2]
                    acc_ref[s * blk:(s + 1) * blk, ps] += _dot(w_pair, v_pairs[h // 2])
        return tuple(out)

    carries = tuple(jnp.zeros((blk, 1), F32) for _ in range(nq * SB_HEADS))
    for d in reversed(range(nq)):
        masks = tuple(SB_SKIP if d > s else (causal if d == s else None) for s in range(nq))
        carries = visit(first + d, carries, masks)
    lax.fori_loop(0, first, lambda j, c: visit(first - 1 - j, c, (None,) * nq), carries)
    o_ref[...] = acc_ref[...].astype(BF)


def _sb_prompt(q, kb, vb, batch, seq, blk, nq):
    nblk = seq // (blk * nq)
    return pl.pallas_call(
        functools.partial(_sb_prompt_kernel, blk=blk, nq=nq),
        grid=(batch, nblk),
        in_specs=[
            pl.BlockSpec((nq * blk, SB_W), lambda b, i: (b * nblk + i, 0)),
            pl.BlockSpec((seq, SB_W), lambda b, i: (b, 0)),
            pl.BlockSpec((seq, SB_W), lambda b, i: (b, 0)),
        ],
        out_specs=pl.BlockSpec((nq * blk, SB_W), lambda b, i: (b * nblk + i, 0)),
        out_shape=jax.ShapeDtypeStruct((batch * seq, SB_W), BF),
        scratch_shapes=[pltpu.VMEM((nq * blk, SB_W), F32)],
        compiler_params=_cparams(("parallel", "arbitrary")),
        name="sb_prompt",
    )(q, kb, vb)


def _sb_sample_kernel(q_ref, kn_ref, vn_ref, kc_ref, vc_ref, o_ref, *, seq, past, blk):
    pair_w = 2 * SB_DIM
    n_pairs = SB_HEADS // 2
    pairs = [slice(p * pair_w, (p + 1) * pair_w) for p in range(n_pairs)]

    def split(x, axis):
        low = lax.broadcasted_iota(jnp.int32, x.shape, axis) < SB_DIM
        zero = jnp.zeros(x.shape, x.dtype)
        return jnp.where(low, x, zero), jnp.where(low, zero, x)

    q_heads = []
    for ps in pairs:
        q_heads.extend(split(q_ref[:, ps], 1))
    r = lax.broadcasted_iota(jnp.int32, (SB_HEADS * seq, seq), 0) % seq
    causal = lax.broadcasted_iota(jnp.int32, (SB_HEADS * seq, seq), 1) < r

    def visit(k_pairs, v_pairs, transposed, carry, acc, upper, mask):
        qk = _dot if transposed else _dot_nt
        pv = _dot_nt if transposed else _dot
        z = jnp.concatenate([qk(q_heads[h], k_pairs[h // 2]) for h in range(SB_HEADS)], axis=0)
        w, carry = _sb_weights(z, carry, upper, mask)
        out = []
        for p in range(n_pairs):
            v_lo, v_hi = split(v_pairs[p], 0 if transposed else 1)
            w_lo = w[(2 * p) * seq:(2 * p + 1) * seq]
            w_hi = w[(2 * p + 1) * seq:(2 * p + 2) * seq]
            out.append(acc[p] + pv(w_lo, v_lo) + pv(w_hi, v_hi))
        return carry, out

    carry = jnp.zeros((SB_HEADS * seq, 1), F32)
    acc = [jnp.zeros((seq, pair_w), F32) for _ in pairs]
    carry, acc = visit([kn_ref[:, ps] for ps in pairs], [vn_ref[:, ps] for ps in pairs], False,
                       carry, acc, _upper_ones(seq), causal)
    upper_past = _upper_ones(blk)
    for j in reversed(range(past // blk)):
        cols = slice(j * blk, (j + 1) * blk)
        carry, acc = visit([kc_ref[0, 0, ps, cols].astype(BF) for ps in pairs],
                           [vc_ref[0, 0, ps, cols].astype(BF) for ps in pairs], True,
                           carry, acc, upper_past, None)
    for p, ps in enumerate(pairs):
        o_ref[:, ps] = acc[p].astype(BF)


def _sb_sample(q, kb, vb, cache_k, cache_v, layer, batch, seq, blk):
    past = cache_k.shape[3]
    new = pl.BlockSpec((seq, SB_W), lambda b: (b, 0))
    cache = pl.BlockSpec((1, 1, SB_W, past), lambda b: (layer, b, 0, 0))
    return pl.pallas_call(
        functools.partial(_sb_sample_kernel, seq=seq, past=past, blk=blk),
        grid=(batch,),
        in_specs=[new, new, new, cache, cache],
        out_specs=new,
        out_shape=jax.ShapeDtypeStruct((batch * seq, SB_W), BF),
        compiler_params=_cparams(("parallel",)),
        name="sb_sample",
    )(q, kb, vb, cache_k, cache_v)


def _pool_kernel(u_ref, hist_ref, w_ref, scale_ref, o_ref, buf_ref, *, tm, pos0):
    i = pl.program_id(1)

    @pl.when(i == 0)
    def _():
        buf_ref[0:HIST_PAD, :] = hist_ref[0]

    @pl.when(i > 0)
    def _():
        buf_ref[0:HIST_PAD, :] = buf_ref[tm:tm + HIST_PAD, :]

    u = u_ref[...]
    buf_ref[HIST_PAD:HIST_PAD + tm, :] = u

    rows = tm + HIST_PAD
    low = lax.broadcasted_iota(jnp.int32, (tm, LANE), 1) < POOL_GDIM
    pos1 = pos0 + i * tm + lax.broadcasted_iota(jnp.int32, (tm, LANE), 0) + 1
    pooled = []
    for tile in range(POOL_W // LANE):
        cols = slice(tile * LANE, (tile + 1) * LANE)
        w_lo, w_hi = POOL_WINDOWS[2 * tile], POOL_WINDOWS[2 * tile + 1]
        s = buf_ref[0:rows, cols]
        sums, w = {}, 1
        while w < w_hi:
            s = s + pltpu.roll(s, w, 0)
            w *= 2
            sums[w] = s
        win_sum = jnp.where(low, sums[w_lo][HIST_PAD:], sums[w_hi][HIST_PAD:])
        cnt = jnp.minimum(pos1, jnp.where(low, w_lo, w_hi)).astype(F32)
        pooled.append(win_sum / cnt - u[:, cols])
    pooled = jnp.concatenate(pooled, axis=1)
    o_ref[...] = (_dot(pooled.astype(BF), w_ref[...]) * scale_ref[...]).astype(BF)


def _pool(u, hist, w_bd, scale, batch, seq, tm, pos0):
    nt = seq // tm
    return pl.pallas_call(
        functools.partial(_pool_kernel, tm=tm, pos0=pos0),
        grid=(batch, nt),
        in_specs=[
            pl.BlockSpec((tm, POOL_W), lambda b, i: (b * nt + i, 0)),
            pl.BlockSpec((1, HIST_PAD, POOL_W), lambda b, i: (b, 0, 0)),
            _const_spec((POOL_W, POOL_W)),
            _const_spec((1, POOL_W)),
        ],
        out_specs=pl.BlockSpec((tm, POOL_W), lambda b, i: (b * nt + i, 0)),
        out_shape=jax.ShapeDtypeStruct((batch * seq, POOL_W), BF),
        scratch_shapes=[pltpu.VMEM((tm + HIST_PAD, POOL_W), F32)],
        compiler_params=_cparams(("parallel", "arbitrary")),
        name="pool_mix",
    )(u, hist, w_bd, scale)


def _gla_levels(rows):
    out, s = [], 1
    while s < rows:
        out.append(s)
        s *= 2
    return out


def _gla_exponent_matrix(rows):
    t = np.arange(rows)[:, None]
    j = np.arange(rows)[None, :]
    blocks = [j <= t]
    for s in _gla_levels(rows)[1:]:
        start = (t // s) * s
        later = (t // s) % 2 == 1
        blocks.append(np.where(later, (j >= start) & (j <= t), (j > t) & (j <= start + s - 1)))
    return np.concatenate(blocks, axis=0).astype(np.float32)


def _gla_later_mask(rows):
    t = np.arange(rows)[:, None]
    blocks = [np.broadcast_to((t // s) % 2 == 1, (rows, GLA_WP)) for s in _gla_levels(rows)]
    return np.concatenate(blocks, axis=0).astype(np.float32)


def _gla_kernel(qkvg_ref, la_ref, em_ref, lm_ref, gn_ref, s0_ref, o_ref, st_ref, *, rows, sub):
    @pl.when(pl.program_id(1) == 0)
    def _():
        st_ref[...] = s0_ref[...]

    for j in range(sub):
        _gla_tile(qkvg_ref, la_ref, em_ref, lm_ref, gn_ref, o_ref, st_ref,
                  slice(j * rows, (j + 1) * rows), rows)


def _gla_tile(qkvg_ref, la_ref, em_ref, lm_ref, gn_ref, o_ref, st_ref, rs, rows):
    expo = _dot(em_ref[...], la_ref[rs, :])

    qb = qkvg_ref[rs, 0:GLA_WP]
    kb = qkvg_ref[rs, GLA_WP:2 * GLA_WP]
    vb = qkvg_ref[rs, 2 * GLA_WP:3 * GLA_WP]
    gate = qkvg_ref[rs, 3 * GLA_WP:4 * GLA_WP].astype(F32)

    b = expo[0:rows]
    b_last = b[rows - 1:rows]
    q_dec = qb * jnp.exp2(b).astype(BF)
    k_dec = kb * jnp.exp2(b_last - b).astype(BF)

    row = lax.broadcasted_iota(jnp.int32, (rows, GLA_WP), 0)
    r = lax.broadcasted_iota(jnp.int32, (rows, rows), 0)
    c = lax.broadcasted_iota(jnp.int32, (rows, rows), 1)
    heads = [slice(h * GLA_DP, (h + 1) * GLA_DP) for h in range(GLA_HEADS)]

    att = [jnp.where(r == c, _dot_nt(qb[:, hs], kb[:, hs]), 0.0) for hs in heads]
    for n, s in enumerate(_gla_levels(rows)):
        shift = s.bit_length() - 1
        rb = r >> shift
        pair = ((rb & 1) == 1) & ((c >> shift) == rb - 1)
        if s == 1:
            e = jnp.where((row & 1) == 1, la_ref[rs, :].astype(F32), 0.0)
        else:
            e = expo[n * rows:(n + 1) * rows]
        later = lm_ref[n * rows:(n + 1) * rows] > 0
        x_lvl = jnp.where(later, qb, kb) * jnp.exp2(e).astype(BF)
        for h, hs in enumerate(heads):
            att[h] = jnp.where(pair, _dot_nt(x_lvl[:, hs], x_lvl[:, hs]), att[h])

    for h, hs in enumerate(heads):
        st = st_ref[0, h]
        o = (_dot(att[h].astype(BF), vb[:, hs]) + _dot_nt(q_dec[:, hs], st.astype(BF))) * (GLA_D ** -0.5)
        ms = jnp.sum(o * o, axis=-1, keepdims=True) * (1.0 / GLA_D)
        y = o * lax.rsqrt(ms + EPS) * gn_ref[:, hs]
        gh = gate[:, hs]
        o_ref[rs, hs] = (y * (gh * _sigmoid(gh))).astype(BF)
        st_ref[0, h] = st * jnp.exp2(b_last[:, hs]) + _dot_tn(vb[:, hs], k_dec[:, hs])


def _gla(qkvg, log_a, gn, s0t, batch, seq, rows, sub):
    nt = seq // (rows * sub)
    em = jnp.asarray(_gla_exponent_matrix(rows), BF)
    lm = jnp.asarray(_gla_later_mask(rows), BF)
    state = pl.BlockSpec((1, GLA_HEADS, GLA_DP, GLA_DP), lambda b, i: (b, 0, 0, 0))
    return pl.pallas_call(
        functools.partial(_gla_kernel, rows=rows, sub=sub),
        grid=(batch, nt),
        in_specs=[
            pl.BlockSpec((rows * sub, 4 * GLA_WP), lambda b, i: (b * nt + i, 0)),
            pl.BlockSpec((rows * sub, GLA_WP), lambda b, i: (b * nt + i, 0)),
            _const_spec(em.shape),
            _const_spec(lm.shape),
            _const_spec((1, GLA_WP)),
            state,
        ],
        out_specs=[pl.BlockSpec((rows * sub, GLA_WP), lambda b, i: (b * nt + i, 0)), state],
        out_shape=[jax.ShapeDtypeStruct((batch * seq, GLA_WP), BF),
                   jax.ShapeDtypeStruct((batch, GLA_HEADS, GLA_DP, GLA_DP), F32)],
        compiler_params=_cparams(("parallel", "arbitrary")),
        name="gla",
    )(qkvg, log_a, em, lm, gn, s0t)


def _merge_kernel(x_ref, oa_ref, ob_ref, oc_ref, wg_ref, wa_ref, wb_ref, wc_ref, wo_ref,
                  gpre_ref, gpost_ref, y_ref, *, d_model):
    x = x_ref[...]
    h = _rms(x, gpre_ref[...]).astype(BF)
    merged = None
    for j, (o_ref, w_ref) in enumerate(((oa_ref, wa_ref), (ob_ref, wb_ref), (oc_ref, wc_ref))):
        gate = _sigmoid(_dot_nt(h, wg_ref[j * d_model:(j + 1) * d_model, :]))
        term = gate * _dot(o_ref[...], w_ref[...])
        merged = term if merged is None else merged + term
    mix = _dot(merged.astype(BF), wo_ref[...])
    y_ref[...] = x + _rms(mix, gpost_ref[...])


def _merge(x, oa, ob, oc, wg, wa, wb, wc, wo, gpre, gpost, tm):
    n, d = x.shape
    row = lambda width: pl.BlockSpec((tm, width), lambda i: (i, 0))
    return pl.pallas_call(
        functools.partial(_merge_kernel, d_model=d),
        grid=(n // tm,),
        in_specs=[row(d), row(SB_W), row(POOL_W), row(GLA_WP),
                  _const_spec(wg.shape), _const_spec(wa.shape), _const_spec(wb.shape),
                  _const_spec(wc.shape), _const_spec(wo.shape),
                  _const_spec((1, d)), _const_spec((1, d))],
        out_specs=row(d),
        out_shape=jax.ShapeDtypeStruct((n, d), F32),
        compiler_params=_cparams(("parallel",)),
        name="merge_out",
    )(x, oa, ob, oc, wg, wa, wb, wc, wo, gpre, gpost)


def _xattn_kernel(x_ref, mk_ref, mv_ref, wq_ref, wo_ref, gpre_ref, gpost_ref, y_ref, *, d_model):
    x = x_ref[...]
    q = _dot(_rms(x, gpre_ref[...]).astype(BF), wq_ref[...]).astype(BF)
    hd = d_model // X_HEADS
    heads = [slice(h * hd, (h + 1) * hd) for h in range(X_HEADS)]
    s, o = {}, {}
    for i in range(X_HEADS + 1):
        if i < X_HEADS:
            s[i] = _dot_nt(q[:, heads[i]], mk_ref[0, :, heads[i]].astype(BF)) * (hd ** -0.5)
        j = i - 1
        if 0 <= j < X_HEADS:
            p = jnp.exp(s[j] - jnp.max(s[j], axis=-1, keepdims=True))
            p = p / jnp.sum(p, axis=-1, keepdims=True)
            o[j] = _dot(p.astype(BF), mv_ref[0, :, heads[j]].astype(BF)).astype(BF)
    attn = jnp.concatenate([o[h] for h in range(X_HEADS)], axis=-1)
    y_ref[...] = x + _rms(_dot(attn, wo_ref[...]), gpost_ref[...])


def _xattn_short_kernel(x_ref, mk_ref, mv_ref, wq_ref, wo_ref, gpre_ref, gpost_ref, y_ref,
                        q_scr, o_scr, *, d_model, seq):
    b = pl.program_id(0)

    @pl.when(b == 0)
    def _():
        q_scr[...] = _dot(_rms(x_ref[...], gpre_ref[...]).astype(BF), wq_ref[...]).astype(BF)

    rows = pl.ds(pl.multiple_of(b * seq, seq), seq)
    hd = d_model // X_HEADS
    for h in range(X_HEADS):
        hs = slice(h * hd, (h + 1) * hd)
        s = _dot_nt(q_scr[rows, hs], mk_ref[0, :, hs].astype(BF)) * (hd ** -0.5)
        p = jnp.exp(s - jnp.max(s, axis=-1, keepdims=True))
        p = p / jnp.sum(p, axis=-1, keepdims=True)
        o_scr[rows, hs] = _dot(p.astype(BF), mv_ref[0, :, hs].astype(BF)).astype(BF)

    @pl.when(b == pl.num_programs(0) - 1)
    def _():
        y_ref[...] = x_ref[...] + _rms(_dot(o_scr[...], wo_ref[...]), gpost_ref[...])


def _xattn_short(x, mk, mv, mem_index, wq, wo, gpre, gpost, batch, seq):
    n, d = x.shape
    mem_len = mk.shape[-2]
    mem = pl.BlockSpec((1, mem_len, d), lambda b: (mem_index(b), 0, 0))
    return pl.pallas_call(
        functools.partial(_xattn_short_kernel, d_model=d, seq=seq),
        grid=(batch,),
        in_specs=[_const_spec((n, d)), mem, mem, _const_spec((d, d)), _const_spec((d, d)),
                  _const_spec((1, d)), _const_spec((1, d))],
        out_specs=pl.BlockSpec((n, d), lambda b: (0, 0)),
        out_shape=jax.ShapeDtypeStruct((n, d), F32),
        scratch_shapes=[pltpu.VMEM((n, d), BF), pltpu.VMEM((n, d), BF)],
        compiler_params=_cparams(("arbitrary",)),
        name="cross_attn_short",
    )(x, mk, mv, wq, wo, gpre, gpost)


def _xattn(x, mk, mv, mem_index, wq, wo, gpre, gpost, batch, seq, tm):
    n, d = x.shape
    if seq < XATTN_SHORT_SEQ:
        return _xattn_short(x, mk, mv, mem_index, wq, wo, gpre, gpost, batch, seq)
    nt = seq // tm
    mem_len = mk.shape[-2]
    row = pl.BlockSpec((tm, d), lambda b, i: (b * nt + i, 0))
    mem = pl.BlockSpec((1, mem_len, d), lambda b, i: (mem_index(b), 0, 0))
    return pl.pallas_call(
        functools.partial(_xattn_kernel, d_model=d),
        grid=(batch, nt),
        in_specs=[row, mem, mem, _const_spec((d, d)), _const_spec((d, d)),
                  _const_spec((1, d)), _const_spec((1, d))],
        out_specs=row,
        out_shape=jax.ShapeDtypeStruct((n, d), F32),
        compiler_params=_cparams(("parallel", "parallel")),
        name="cross_attn",
    )(x, mk, mv, wq, wo, gpre, gpost)


def _ffn_kernel(x_ref, wg_ref, wu_ref, wo_ref, gpre_ref, gpost_ref, y_ref, *, chunk):
    x = x_ref[...]
    h = _rms(x, gpre_ref[...]).astype(BF)
    ff = wg_ref.shape[1]
    acc = jnp.zeros(x.shape, F32)
    for j in range(ff // chunk):
        cs = slice(j * chunk, (j + 1) * chunk)
        g = _dot(h, wg_ref[:, cs])
        u = _dot(h, wu_ref[:, cs])
        act = (g * _sigmoid(g) * u).astype(BF)
        acc = acc + _dot(act, wo_ref[cs, :])
    y_ref[...] = x + _rms(acc, gpost_ref[...])


def _ffn(x, wg, wu, wo, gpre, gpost, tm, chunk):
    n, d = x.shape
    row = pl.BlockSpec((tm, d), lambda i: (i, 0))
    return pl.pallas_call(
        functools.partial(_ffn_kernel, chunk=chunk),
        grid=(n // tm,),
        in_specs=[row, _const_spec(wg.shape), _const_spec(wu.shape), _const_spec(wo.shape),
                  _const_spec((1, d)), _const_spec((1, d))],
        out_specs=row,
        out_shape=jax.ShapeDtypeStruct((n, d), F32),
        compiler_params=_cparams(("parallel",)),
        name="ffn",
    )(x, wg, wu, wo, gpre, gpost)


def _memkv_kernel(m_ref, g_ref, wk_ref, wv_ref, k_ref, v_ref):
    m = _rms(m_ref[...], g_ref[0]).astype(BF)
    k_ref[0] = _dot(m, wk_ref[0].astype(BF))
    v_ref[0] = _dot(m, wv_ref[0].astype(BF))


def _memkv(mem, g, wk, wv, tm):
    depth, d, _ = wk.shape
    n = mem.shape[0]
    w = pl.BlockSpec((1, d, d), lambda l, i: (l, 0, 0))
    out = pl.BlockSpec((1, tm, d), lambda l, i: (l, i, 0))
    return pl.pallas_call(
        _memkv_kernel,
        grid=(depth, n // tm),
        in_specs=[pl.BlockSpec((tm, d), lambda l, i: (i, 0)),
                  pl.BlockSpec((1, 1, d), lambda l, i: (l, 0, 0)), w, w],
        out_specs=[out, out],
        out_shape=[jax.ShapeDtypeStruct((depth, n, d), F32)] * 2,
        compiler_params=_cparams(("parallel", "parallel")),
        name="mem_kv",
    )(mem, g, wk, wv)


def _pad_heads(w, axis):
    shape = w.shape
    w = w.reshape(shape[:axis] + (GLA_HEADS, GLA_D) + shape[axis + 1:])
    pad = [(0, 0)] * w.ndim
    pad[axis + 1] = (0, GLA_DP - GLA_D)
    w = jnp.pad(w, pad)
    return w.reshape(shape[:axis] + (GLA_WP,) + shape[axis + 1:])


def _pack_layer(p, l):
    w_in = jnp.swapaxes(p['w_in'][l], 0, 1)
    o = 3 * SB_W + POOL_W
    sb_u = w_in[:o]
    gla = [_pad_heads(w_in[o + j * GLA_W:o + (j + 1) * GLA_W], 0) for j in range(4)]
    o += 4 * GLA_W
    lr = jnp.pad(w_in[o:o + GLA_RANK], ((0, LANE - GLA_RANK), (0, 0)))
    gates = w_in[o + GLA_RANK:]
    eye = jnp.eye(len(POOL_WINDOWS), dtype=F32)
    w_pool_bd = (eye[:, None, :, None] * p['w_pool'][l][:, :, None, :]).reshape(POOL_W, POOL_W)
    ff = p['w_ffn_out'].shape[1]
    row = lambda v: v.reshape(1, -1)
    return dict(
        w_in=jnp.concatenate([sb_u] + gla + [lr], axis=0).astype(BF),
        w_gate=gates.astype(BF),
        w_a2=jnp.pad(_pad_heads(p['w_gla_a2'][l], 1), ((0, LANE - GLA_RANK), (0, 0))).astype(BF),
        b_a=row(_pad_heads(p['b_gla_a'][l], 0)),
        gla_norm=row(_pad_heads(p['gla_norm'][l], 0)),
        w_pool=w_pool_bd.astype(BF),
        pool_scale=row(p['pool_scale'][l]),
        w_a=p['w_branch_a'][l].astype(BF),
        w_b=p['w_branch_b'][l].astype(BF),
        w_c=_pad_heads(p['w_branch_c'][l], 0).astype(BF),
        w_mix=p['w_mix_out'][l].astype(BF),
        w_xq=p['w_xq'][l].astype(BF),
        w_xo=p['w_xo'][l].astype(BF),
        w_g=p['w_ffn_in'][l][:, :ff].astype(BF),
        w_u=p['w_ffn_in'][l][:, ff:].astype(BF),
        w_o=p['w_ffn_out'][l].astype(BF),
        n_mix_pre=row(p['norm_mix_pre'][l]), n_mix_post=row(p['norm_mix_post'][l]),
        n_x_pre=row(p['norm_x_pre'][l]), n_x_post=row(p['norm_x_post'][l]),
        n_ffn_pre=row(p['norm_ffn_pre'][l]), n_ffn_post=row(p['norm_ffn_post'][l]),
    )


def _pad_state(s):
    s = jnp.swapaxes(s, -1, -2)
    return jnp.pad(s, ((0, 0), (0, 0), (0, GLA_DP - GLA_D), (0, GLA_DP - GLA_D)))


def _unpad_state(st):
    return jnp.swapaxes(st[:, :, :GLA_D, :GLA_D], -1, -2)


def _tiles(batch, seq):
    n = batch * seq
    return dict(
        proj=min(512, n),
        sb=min(256, seq),
        pool=min(1024, seq),
        gla=min(128, seq),
        merge=min(1024, n),
        xattn=min(1024, seq),
        ffn=min(1024, n),
    )


def _layer(x, w, batch, seq, sb_attend, pool_hist, pos0, s0t, mk, mv, mem_index, kv_stack=None):
    t = _tiles(batch, seq)
    q, k, v, kb, vb, u, qkvg, log_a = _proj(x, w['n_mix_pre'], w['w_in'], w['w_a2'], w['b_a'],
                                            t['proj'], kv_stack)
    oa = sb_attend(q, kb, vb)
    ob = _pool(u, pool_hist, w['w_pool'], w['pool_scale'], batch, seq, t['pool'], pos0)
    oc, st = _gla(qkvg, log_a, w['gla_norm'], s0t, batch, seq, t['gla'],
                  4 if seq % (4 * t['gla']) == 0 else 1)
    x = _merge(x, oa, ob, oc, w['w_gate'], w['w_a'], w['w_b'], w['w_c'], w['w_mix'],
               w['n_mix_pre'], w['n_mix_post'], t['merge'])
    x = _xattn(x, mk, mv, mem_index, w['w_xq'], w['w_xo'], w['n_x_pre'], w['n_x_post'],
               batch, seq, t['xattn'])
    x = _ffn(x, w['w_g'], w['w_u'], w['w_o'], w['n_ffn_pre'], w['n_ffn_post'], t['ffn'], 256)
    return x, k, v, u, st


def kernel(x_prompt, x_sample, mem_prompt, cache_sb_k, cache_sb_v, state_pool, state_gla, cache_mem_k, cache_mem_v, w_in, w_gla_a2, b_gla_a, gla_norm, w_pool, pool_scale, w_branch_a, w_branch_b, w_branch_c, w_mix_out, mem_norm, w_xq, w_xk, w_xv, w_xo, w_ffn_in, w_ffn_out, norm_mix_pre, norm_mix_post, norm_x_pre, norm_x_post, norm_ffn_pre, norm_ffn_post):
    p = dict(w_in=w_in, w_gla_a2=w_gla_a2, b_gla_a=b_gla_a, gla_norm=gla_norm, w_pool=w_pool,
             pool_scale=pool_scale, w_branch_a=w_branch_a, w_branch_b=w_branch_b,
             w_branch_c=w_branch_c, w_mix_out=w_mix_out, w_xq=w_xq, w_xo=w_xo,
             w_ffn_in=w_ffn_in, w_ffn_out=w_ffn_out, norm_mix_pre=norm_mix_pre,
             norm_mix_post=norm_mix_post, norm_x_pre=norm_x_pre, norm_x_post=norm_x_post,
             norm_ffn_pre=norm_ffn_pre, norm_ffn_post=norm_ffn_post)
    depth = w_in.shape[0]
    bp, tp, d = x_prompt.shape
    bs, ts, _ = x_sample.shape
    mem_len = mem_prompt.shape[1]
    past = cache_sb_k.shape[2]

    mk_all, mv_all = _memkv(mem_prompt.reshape(bp * mem_len, d), mem_norm.reshape(depth, 1, d),
                            w_xk, w_xv, min(512, bp * mem_len))
    mk_p = mk_all.reshape(depth * bp, mem_len, d)
    mv_p = mv_all.reshape(depth * bp, mem_len, d)
    mk_s = cache_mem_k.reshape(depth * bs, mem_len, d)
    mv_s = cache_mem_v.reshape(depth * bs, mem_len, d)
    ck = jnp.transpose(cache_sb_k, (0, 1, 3, 4, 2)).reshape(depth, bs, SB_W, past)
    cv = jnp.transpose(cache_sb_v, (0, 1, 3, 4, 2)).reshape(depth, bs, SB_W, past)

    yp = x_prompt.reshape(bp * tp, d)
    ys = x_sample.reshape(bs * ts, d)
    hist_zero = jnp.zeros((bp, HIST_PAD, POOL_W), F32)
    state_zero = jnp.zeros((bp, GLA_HEADS, GLA_DP, GLA_DP), F32)
    tiles_p = _tiles(bp, tp)
    kv_t = None
    poolp, glap, kss, vss, pools, glas = ([] for _ in range(6))
    for l in range(depth):
        w = _pack_layer(p, l)
        yp, k, v, u, st = _layer(
            yp, w, bp, tp,
            functools.partial(_sb_prompt, batch=bp, seq=tp, blk=tiles_p['sb'],
                              nq=4 if tp % (4 * tiles_p['sb']) == 0 else 1),
            hist_zero, 0, state_zero, mk_p, mv_p, lambda b, l=l: l * bp + b,
            kv_stack=(l, depth, bp, tp, kv_t))
        kv_t = (k, v)
        poolp.append(u.reshape(bp, tp, POOL_W)[:, tp - POOL_HIST:])
        glap.append(_unpad_state(st))

        hist = jnp.pad(state_pool[l], ((0, 0), (HIST_PAD - POOL_HIST, 0), (0, 0)))
        ys, k, v, u, st = _layer(
            ys, w, bs, ts,
            functools.partial(_sb_sample, cache_k=ck, cache_v=cv, layer=l, batch=bs, seq=ts,
                              blk=min(256, past)),
            hist, past, _pad_state(state_gla[l]), mk_s, mv_s, lambda b, l=l: l * bs + b)
        kss.append(k.reshape(bs, ts, SB_HEADS, SB_DIM))
        vss.append(v.reshape(bs, ts, SB_HEADS, SB_DIM))
        full = jnp.concatenate([state_pool[l], u.reshape(bs, ts, POOL_W)], axis=1)
        pools.append(full[:, -POOL_HIST:])
        glas.append(_unpad_state(st))

    kp, vp = (jnp.transpose(a.reshape(depth, bp, SB_HEADS, SB_DIM, tp), (0, 1, 4, 2, 3)) for a in kv_t)
    return (yp.reshape(bp, tp, d), ys.reshape(bs, ts, d),
            kp, vp, jnp.stack(poolp), jnp.stack(glap),
            mk_all.reshape(depth, bp, mem_len, X_HEADS, d // X_HEADS),
            mv_all.reshape(depth, bp, mem_len, X_HEADS, d // X_HEADS),
            jnp.stack(kss), jnp.stack(vss), jnp.stack(pools), jnp.stack(glas))
```

```python
import functools

import numpy as np
import jax
import jax.numpy as jnp
from jax import lax
from jax.experimental import pallas as pl
from jax.experimental.pallas import tpu as pltpu

BF = jnp.bfloat16
F32 = jnp.float32
EPS = 1e-6
LOG2_E = 1.4426950408889634

LANE = 128
VMEM_LIMIT = 56 * 1024 * 1024

SB_HEADS = 6
SB_DIM = 64
SB_W = SB_HEADS * SB_DIM
POOL_WINDOWS = (2, 4, 8, 16)
POOL_GDIM = 64
POOL_W = len(POOL_WINDOWS) * POOL_GDIM
POOL_HIST = max(POOL_WINDOWS) - 1
HIST_PAD = 16
assert all(w & (w - 1) == 0 for w in POOL_WINDOWS) and 2 * POOL_GDIM == LANE and POOL_HIST <= HIST_PAD
GLA_HEADS = 4
GLA_D = 96
GLA_DP = LANE
GLA_W = GLA_HEADS * GLA_D
GLA_WP = GLA_HEADS * GLA_DP
GLA_RANK = 16
GLA_TAU = 16.0
X_HEADS = 4
XATTN_SHORT_SEQ = 128

C_Q = 0
C_K = C_Q + SB_W
C_V = C_K + SB_W
C_U = C_V + SB_W
C_GLA = C_U + POOL_W
C_LR = C_GLA + 4 * GLA_WP
C_END = C_LR + LANE


def _layer_spec(stacked, layer):
    return pl.BlockSpec((None,) + stacked.shape[1:], lambda *_: (layer, 0, 0),
                        pipeline_mode=pl.Buffered(1))


def _cparams(sem):
    return pltpu.CompilerParams(dimension_semantics=sem, vmem_limit_bytes=VMEM_LIMIT)


def _rms(x, g):
    return x * lax.rsqrt(jnp.mean(x * x, axis=-1, keepdims=True) + EPS) * g


def _softplus_neg_abs(z):
    return jnp.log(1.0 + jnp.exp(-jnp.abs(z)))


def _sigmoid(z):
    return 0.5 * jnp.tanh(0.5 * z) + 0.5


def _dot(a, b):
    return jnp.dot(a, b, preferred_element_type=F32)


def _dot_nt(a, b):
    return lax.dot_general(a, b, (((1,), (1,)), ((), ())), preferred_element_type=F32)


def _dot_tn(a, b):
    return lax.dot_general(a, b, (((0,), (0,)), ((), ())), preferred_element_type=F32)


def _const_spec(shape):
    zeros = (0,) * len(shape)
    return pl.BlockSpec(shape, lambda *_: zeros, pipeline_mode=pl.Buffered(1))


def _proj_kernel(*refs, n_alias, transposed_kv):
    x_ref, g_ref, w_ref, wa_ref, ba_ref = refs[:N_PROJ_IN]
    q_ref, k_ref, v_ref, kb_ref, vb_ref, u_ref, gla_ref, la_ref = refs[N_PROJ_IN + n_alias:]
    h = _rms(x_ref[...], g_ref[...]).astype(BF)

    def mm(lo, width):
        return _dot_nt(h, w_ref[lo:lo + width, :])

    q_ref[...] = (mm(C_Q, SB_W) * (SB_DIM ** -0.5)).astype(BF)
    for lo, f32_ref, bf_ref in ((C_K, k_ref, kb_ref), (C_V, v_ref, vb_ref)):
        kv = mm(lo, SB_W)
        bf_ref[...] = kv.astype(BF)
        if transposed_kv is None:
            f32_ref[...] = kv
        else:
            for l in range(f32_ref.shape[0]):
                f32_ref[l, 0] = kv.T if l == transposed_kv else jnp.zeros((SB_W, kv.shape[0]), F32)
    u_ref[...] = mm(C_U, POOL_W)
    for j in range(4):
        gla_ref[:, j * GLA_WP:(j + 1) * GLA_WP] = mm(C_GLA + j * GLA_WP, GLA_WP).astype(BF)
    xa = _dot(mm(C_LR, LANE).astype(BF), wa_ref[...]) + ba_ref[...]
    la_ref[...] = ((jnp.minimum(xa, 0.0) - _softplus_neg_abs(xa)) * (LOG2_E / GLA_TAU)).astype(BF)


N_PROJ_IN = 5


def _proj(x, g, w, wa, ba, tm, kv_stack=None):
    n, d = x.shape
    wtot = w.shape[0]
    row = lambda width: pl.BlockSpec((tm, width), lambda i: (i, 0))
    if kv_stack is None:
        kv_spec, kv_shape, extra, aliases = row(SB_W), (n, SB_W), [], {}
    else:
        layer, depth, batch, seq, bufs = kv_stack
        nt = seq // tm
        kv_shape = (depth, batch, SB_W, seq)
        if bufs is None:
            kv_spec = pl.BlockSpec((depth, 1, SB_W, tm), lambda i: (0, i // nt, 0, i % nt))
            kv_slot = layer
        else:
            kv_spec = pl.BlockSpec((1, 1, SB_W, tm), lambda i: (layer, i // nt, 0, i % nt))
            kv_slot = 0
        extra = [] if bufs is None else list(bufs)
        aliases = {N_PROJ_IN + j: 1 + j for j in range(len(extra))}
    outs = [
        (row(SB_W), (n, SB_W), BF), (kv_spec, kv_shape, F32), (kv_spec, kv_shape, F32),
        (row(SB_W), (n, SB_W), BF), (row(SB_W), (n, SB_W), BF), (row(POOL_W), (n, POOL_W), F32),
        (row(4 * GLA_WP), (n, 4 * GLA_WP), BF), (row(GLA_WP), (n, GLA_WP), BF),
    ]
    return pl.pallas_call(
        functools.partial(_proj_kernel, n_alias=len(extra),
                          transposed_kv=None if kv_stack is None else kv_slot),
        grid=(n // tm,),
        in_specs=[row(d), _const_spec((1, d)), _const_spec((wtot, d)),
                  _const_spec((LANE, GLA_WP)), _const_spec((1, GLA_WP))]
                 + [pl.BlockSpec(memory_space=pl.ANY)] * len(extra),
        out_specs=[spec for spec, _, _ in outs],
        out_shape=[jax.ShapeDtypeStruct(shape, dt) for _, shape, dt in outs],
        input_output_aliases=aliases,
        compiler_params=_cparams(("parallel",)),
        name="in_proj",
    )(x, g, w, wa, ba, *extra)


def _upper_ones(n):
    r = lax.broadcasted_iota(jnp.int32, (n, n), 0)
    c = lax.broadcasted_iota(jnp.int32, (n, n), 1)
    return jnp.where(r >= c, 1.0, 0.0).astype(BF)


def _strict_causal(n):
    r = lax.broadcasted_iota(jnp.int32, (n, n), 0)
    c = lax.broadcasted_iota(jnp.int32, (n, n), 1)
    return c < r


def _sb_weights(z, carry, upper, mask):
    return _sb_finish(z, _dot(_sb_softplus(z, mask), upper), carry, mask)


def _sb_softplus(z, mask):
    zb = z.astype(BF)
    t = jnp.maximum(zb, 0.0) + _softplus_neg_abs(zb)
    return t if mask is None else jnp.where(mask, t, jnp.zeros_like(t))


def _sb_finish(z, cum, carry, mask):
    w = jnp.exp(z - cum - carry)
    if mask is not None:
        w = jnp.where(mask, w, 0.0)
    return w.astype(BF), carry + cum[:, 0:1]


SB_SKIP = "skip"
SB_LAG = 1


def _sb_prompt_kernel(q_ref, k_ref, v_ref, o_ref, acc_ref, *, blk, nq):
    first = pl.program_id(1) * nq
    pair_w = 2 * SB_DIM
    n_pairs = SB_HEADS // 2
    upper = _upper_ones(blk)
    causal = _strict_causal(blk)
    low = lax.broadcasted_iota(jnp.int32, (blk, pair_w), 1) < SB_DIM
    zero = jnp.zeros((blk, pair_w), BF)

    def split(x):
        return jnp.where(low, x, zero), jnp.where(low, zero, x)

    q_heads = []
    for s in range(nq):
        heads = []
        for p in range(n_pairs):
            heads.extend(split(q_ref[s * blk:(s + 1) * blk, p * pair_w:(p + 1) * pair_w]))
        q_heads.append(heads)
    acc_ref[...] = jnp.zeros(acc_ref.shape, F32)

    def visit(kb, carries, masks):
        start = pl.multiple_of(kb * blk, blk)
        out = list(carries)
        pairs = [slice(p * pair_w, (p + 1) * pair_w) for p in range(n_pairs)]
        live = [(s, 2 * p + e) for p in range(n_pairs) for s in range(nq)
                if masks[s] is not SB_SKIP for e in range(2)]
        k_pairs = [k_ref[pl.ds(start, blk), ps] for ps in pairs]
        v_pairs = [jnp.concatenate(split(v_ref[pl.ds(start, blk), ps]), axis=0) for ps in pairs]
        z, cum, w = {}, {}, {}
        for i in range(len(live) + 2 * SB_LAG):
            if i < len(live):
                s, h = live[i]
                z[i] = _dot_nt(q_heads[s][h], k_pairs[h // 2])
            j = i - SB_LAG
            if 0 <= j < len(live):
                cum[j] = _dot(_sb_softplus(z[j], masks[live[j][0]]), upper)
            j = i - 2 * SB_LAG
            if 0 <= j < len(live):
                s, h = live[j]
                c = s * SB_HEADS + h
                w[j], out[c] = _sb_finish(z[j], cum[j], carries[c], masks[s])
                if h % 2 == 1:
                    w_pair = jnp.concatenate([w[j - 1], w[j]], axis=1)
                    ps = pairs[h // 2]
                    acc_ref[s * blk:(s + 1) * blk, ps] += _dot(w_pair, v_pairs[h // 2])
        return tuple(out)

    carries = tuple(jnp.zeros((blk, 1), F32) for _ in range(nq * SB_HEADS))
    for d in reversed(range(nq)):
        masks = tuple(SB_SKIP if d > s else (causal if d == s else None) for s in range(nq))
        carries = visit(first + d, carries, masks)
    lax.fori_loop(0, first, lambda j, c: visit(first - 1 - j, c, (None,) * nq), carries)
    o_ref[...] = acc_ref[...].astype(BF)


def _sb_prompt(q, kb, vb, batch, seq, blk, nq):
    nblk = seq // (blk * nq)
    return pl.pallas_call(
        functools.partial(_sb_prompt_kernel, blk=blk, nq=nq),
        grid=(batch, nblk),
        in_specs=[
            pl.BlockSpec((nq * blk, SB_W), lambda b, i: (b * nblk + i, 0)),
            pl.BlockSpec((seq, SB_W), lambda b, i: (b, 0)),
            pl.BlockSpec((seq, SB_W), lambda b, i: (b, 0)),
        ],
        out_specs=pl.BlockSpec((nq * blk, SB_W), lambda b, i: (b * nblk + i, 0)),
        out_shape=jax.ShapeDtypeStruct((batch * seq, SB_W), BF),
        scratch_shapes=[pltpu.VMEM((nq * blk, SB_W), F32)],
        compiler_params=_cparams(("parallel", "arbitrary")),
        name="sb_prompt",
    )(q, kb, vb)


def _sb_sample_kernel(q_ref, kn_ref, vn_ref, kc_ref, vc_ref, o_ref, *, seq, past, blk):
    pair_w = 2 * SB_DIM
    n_pairs = SB_HEADS // 2
    pairs = [slice(p * pair_w, (p + 1) * pair_w) for p in range(n_pairs)]

    def split(x, axis):
        low = lax.broadcasted_iota(jnp.int32, x.shape, axis) < SB_DIM
        zero = jnp.zeros(x.shape, x.dtype)
        return jnp.where(low, x, zero), jnp.where(low, zero, x)

    q_heads = []
    for ps in pairs:
        q_heads.extend(split(q_ref[:, ps], 1))
    r = lax.broadcasted_iota(jnp.int32, (SB_HEADS * seq, seq), 0) % seq
    causal = lax.broadcasted_iota(jnp.int32, (SB_HEADS * seq, seq), 1) < r

    def visit(k_pairs, v_pairs, transposed, carry, acc, upper, mask):
        qk = _dot if transposed else _dot_nt
        pv = _dot_nt if transposed else _dot
        z = jnp.concatenate([qk(q_heads[h], k_pairs[h // 2]) for h in range(SB_HEADS)], axis=0)
        w, carry = _sb_weights(z, carry, upper, mask)
        out = []
        for p in range(n_pairs):
            v_lo, v_hi = split(v_pairs[p], 0 if transposed else 1)
            w_lo = w[(2 * p) * seq:(2 * p + 1) * seq]
            w_hi = w[(2 * p + 1) * seq:(2 * p + 2) * seq]
            out.append(acc[p] + pv(w_lo, v_lo) + pv(w_hi, v_hi))
        return carry, out

    carry = jnp.zeros((SB_HEADS * seq, 1), F32)
    acc = [jnp.zeros((seq, pair_w), F32) for _ in pairs]
    carry, acc = visit([kn_ref[:, ps] for ps in pairs], [vn_ref[:, ps] for ps in pairs], False,
                       carry, acc, _upper_ones(seq), causal)
    upper_past = _upper_ones(blk)
    for j in reversed(range(past // blk)):
        cols = slice(j * blk, (j + 1) * blk)
        carry, acc = visit([kc_ref[0, 0, ps, cols].astype(BF) for ps in pairs],
                           [vc_ref[0, 0, ps, cols].astype(BF) for ps in pairs], True,
                           carry, acc, upper_past, None)
    for p, ps in enumerate(pairs):
        o_ref[:, ps] = acc[p].astype(BF)


def _sb_sample(q, kb, vb, cache_k, cache_v, layer, batch, seq, blk):
    past = cache_k.shape[3]
    new = pl.BlockSpec((seq, SB_W), lambda b: (b, 0))
    cache = pl.BlockSpec((1, 1, SB_W, past), lambda b: (layer, b, 0, 0))
    return pl.pallas_call(
        functools.partial(_sb_sample_kernel, seq=seq, past=past, blk=blk),
        grid=(batch,),
        in_specs=[new, new, new, cache, cache],
        out_specs=new,
        out_shape=jax.ShapeDtypeStruct((batch * seq, SB_W), BF),
        compiler_params=_cparams(("parallel",)),
        name="sb_sample",
    )(q, kb, vb, cache_k, cache_v)


def _pool_kernel(u_ref, hist_ref, w_ref, scale_ref, o_ref, buf_ref, *, tm, pos0):
    i = pl.program_id(1)

    @pl.when(i == 0)
    def _():
        buf_ref[0:HIST_PAD, :] = hist_ref[0]

    @pl.when(i > 0)
    def _():
        buf_ref[0:HIST_PAD, :] = buf_ref[tm:tm + HIST_PAD, :]

    u = u_ref[...]
    buf_ref[HIST_PAD:HIST_PAD + tm, :] = u

    rows = tm + HIST_PAD
    low = lax.broadcasted_iota(jnp.int32, (tm, LANE), 1) < POOL_GDIM
    pos1 = pos0 + i * tm + lax.broadcasted_iota(jnp.int32, (tm, LANE), 0) + 1
    pooled = []
    for tile in range(POOL_W // LANE):
        cols = slice(tile * LANE, (tile + 1) * LANE)
        w_lo, w_hi = POOL_WINDOWS[2 * tile], POOL_WINDOWS[2 * tile + 1]
        s = buf_ref[0:rows, cols]
        sums, w = {}, 1
        while w < w_hi:
            s = s + pltpu.roll(s, w, 0)
            w *= 2
            sums[w] = s
        win_sum = jnp.where(low, sums[w_lo][HIST_PAD:], sums[w_hi][HIST_PAD:])
        cnt = jnp.minimum(pos1, jnp.where(low, w_lo, w_hi)).astype(F32)
        pooled.append(win_sum / cnt - u[:, cols])
    pooled = jnp.concatenate(pooled, axis=1)
    o_ref[...] = (_dot(pooled.astype(BF), w_ref[...]) * scale_ref[...]).astype(BF)


def _pool(u, hist, w_bd, scale, batch, seq, tm, pos0):
    nt = seq // tm
    return pl.pallas_call(
        functools.partial(_pool_kernel, tm=tm, pos0=pos0),
        grid=(batch, nt),
        in_specs=[
            pl.BlockSpec((tm, POOL_W), lambda b, i: (b * nt + i, 0)),
            pl.BlockSpec((1, HIST_PAD, POOL_W), lambda b, i: (b, 0, 0)),
            _const_spec((POOL_W, POOL_W)),
            _const_spec((1, POOL_W)),
        ],
        out_specs=pl.BlockSpec((tm, POOL_W), lambda b, i: (b * nt + i, 0)),
        out_shape=jax.ShapeDtypeStruct((batch * seq, POOL_W), BF),
        scratch_shapes=[pltpu.VMEM((tm + HIST_PAD, POOL_W), F32)],
        compiler_params=_cparams(("parallel", "arbitrary")),
        name="pool_mix",
    )(u, hist, w_bd, scale)


def _gla_levels(rows):
    out, s = [], 1
    while s < rows:
        out.append(s)
        s *= 2
    return out


def _gla_exponent_matrix(rows):
    t = np.arange(rows)[:, None]
    j = np.arange(rows)[None, :]
    blocks = [j <= t]
    for s in _gla_levels(rows)[1:]:
        start = (t // s) * s
        later = (t // s) % 2 == 1
        blocks.append(np.where(later, (j >= start) & (j <= t), (j > t) & (j <= start + s - 1)))
    return np.concatenate(blocks, axis=0).astype(np.float32)


def _gla_later_mask(rows):
    t = np.arange(rows)[:, None]
    blocks = [np.broadcast_to((t // s) % 2 == 1, (rows, GLA_WP)) for s in _gla_levels(rows)]
    return np.concatenate(blocks, axis=0).astype(np.float32)


def _gla_kernel(qkvg_ref, la_ref, em_ref, lm_ref, gn_ref, s0_ref, o_ref, st_ref, *, rows, sub):
    @pl.when(pl.program_id(1) == 0)
    def _():
        st_ref[...] = s0_ref[...]

    for j in range(sub):
        _gla_tile(qkvg_ref, la_ref, em_ref, lm_ref, gn_ref, o_ref, st_ref,
                  slice(j * rows, (j + 1) * rows), rows)


def _gla_tile(qkvg_ref, la_ref, em_ref, lm_ref, gn_ref, o_ref, st_ref, rs, rows):
    expo = _dot(em_ref[...], la_ref[rs, :])

    qb = qkvg_ref[rs, 0:GLA_WP]
    kb = qkvg_ref[rs, GLA_WP:2 * GLA_WP]
    vb = qkvg_ref[rs, 2 * GLA_WP:3 * GLA_WP]
    gate = qkvg_ref[rs, 3 * GLA_WP:4 * GLA_WP].astype(F32)

    b = expo[0:rows]
    b_last = b[rows - 1:rows]
    q_dec = qb * jnp.exp2(b).astype(BF)
    k_dec = kb * jnp.exp2(b_last - b).astype(BF)

    row = lax.broadcasted_iota(jnp.int32, (rows, GLA_WP), 0)
    r = lax.broadcasted_iota(jnp.int32, (rows, rows), 0)
    c = lax.broadcasted_iota(jnp.int32, (rows, rows), 1)
    heads = [slice(h * GLA_DP, (h + 1) * GLA_DP) for h in range(GLA_HEADS)]

    att = [jnp.where(r == c, _dot_nt(qb[:, hs], kb[:, hs]), 0.0) for hs in heads]
    for n, s in enumerate(_gla_levels(rows)):
        shift = s.bit_length() - 1
        rb = r >> shift
        pair = ((rb & 1) == 1) & ((c >> shift) == rb - 1)
        if s == 1:
            e = jnp.where((row & 1) == 1, la_ref[rs, :].astype(F32), 0.0)
        else:
            e = expo[n * rows:(n + 1) * rows]
        later = lm_ref[n * rows:(n + 1) * rows] > 0
        x_lvl = jnp.where(later, qb, kb) * jnp.exp2(e).astype(BF)
        for h, hs in enumerate(heads):
            att[h] = jnp.where(pair, _dot_nt(x_lvl[:, hs], x_lvl[:, hs]), att[h])

    for h, hs in enumerate(heads):
        st = st_ref[0, h]
        o = (_dot(att[h].astype(BF), vb[:, hs]) + _dot_nt(q_dec[:, hs], st.astype(BF))) * (GLA_D ** -0.5)
        ms = jnp.sum(o * o, axis=-1, keepdims=True) * (1.0 / GLA_D)
        y = o * lax.rsqrt(ms + EPS) * gn_ref[:, hs]
        gh = gate[:, hs]
        o_ref[rs, hs] = (y * (gh * _sigmoid(gh))).astype(BF)
        st_ref[0, h] = st * jnp.exp2(b_last[:, hs]) + _dot_tn(vb[:, hs], k_dec[:, hs])


def _gla(qkvg, log_a, gn, s0t, batch, seq, rows, sub):
    nt = seq // (rows * sub)
    em = jnp.asarray(_gla_exponent_matrix(rows), BF)
    lm = jnp.asarray(_gla_later_mask(rows), BF)
    state = pl.BlockSpec((1, GLA_HEADS, GLA_DP, GLA_DP), lambda b, i: (b, 0, 0, 0))
    return pl.pallas_call(
        functools.partial(_gla_kernel, rows=rows, sub=sub),
        grid=(batch, nt),
        in_specs=[
            pl.BlockSpec((rows * sub, 4 * GLA_WP), lambda b, i: (b * nt + i, 0)),
            pl.BlockSpec((rows * sub, GLA_WP), lambda b, i: (b * nt + i, 0)),
            _const_spec(em.shape),
            _const_spec(lm.shape),
            _const_spec((1, GLA_WP)),
            state,
        ],
        out_specs=[pl.BlockSpec((rows * sub, GLA_WP), lambda b, i: (b * nt + i, 0)), state],
        out_shape=[jax.ShapeDtypeStruct((batch * seq, GLA_WP), BF),
                   jax.ShapeDtypeStruct((batch, GLA_HEADS, GLA_DP, GLA_DP), F32)],
        compiler_params=_cparams(("parallel", "arbitrary")),
        name="gla",
    )(qkvg, log_a, em, lm, gn, s0t)


def _merge_kernel(x_ref, oa_ref, ob_ref, oc_ref, win_ref, wa_ref, wb_ref, wc_ref, wo_ref,
                  gpre_ref, gpost_ref, y_ref, *, d_model, gate_row0):
    x = x_ref[...]
    h = _rms(x, gpre_ref[...]).astype(BF)
    merged = None
    for j, (o_ref, w_ref) in enumerate(((oa_ref, wa_ref), (ob_ref, wb_ref), (oc_ref, wc_ref))):
        rows = slice(gate_row0 + j * d_model, gate_row0 + (j + 1) * d_model)
        gate = _sigmoid(_dot_nt(h, win_ref[rows, :]))
        term = gate * _dot(o_ref[...], w_ref[...])
        merged = term if merged is None else merged + term
    mix = _dot(merged.astype(BF), wo_ref[...])
    y_ref[...] = x + _rms(mix, gpost_ref[...])


def _merge(x, oa, ob, oc, w_in_t, wa, wb, wc, wo, gpre, gpost, tm, layer):
    n, d = x.shape
    row = lambda width: pl.BlockSpec((tm, width), lambda i: (i, 0))
    return pl.pallas_call(
        functools.partial(_merge_kernel, d_model=d, gate_row0=w_in_t.shape[1] - 3 * d),
        grid=(n // tm,),
        in_specs=[row(d), row(SB_W), row(POOL_W), row(GLA_WP)]
                 + [_layer_spec(w, layer) for w in (w_in_t, wa, wb, wc, wo)]
                 + [_const_spec((1, d)), _const_spec((1, d))],
        out_specs=row(d),
        out_shape=jax.ShapeDtypeStruct((n, d), F32),
        compiler_params=_cparams(("parallel",)),
        name="merge_out",
    )(x, oa, ob, oc, w_in_t, wa, wb, wc, wo, gpre, gpost)


def _xattn_kernel(x_ref, mk_ref, mv_ref, wq_ref, wo_ref, gpre_ref, gpost_ref, y_ref, *, d_model):
    x = x_ref[...]
    q = _dot(_rms(x, gpre_ref[...]).astype(BF), wq_ref[...]).astype(BF)
    hd = d_model // X_HEADS
    heads = [slice(h * hd, (h + 1) * hd) for h in range(X_HEADS)]
    s, o = {}, {}
    for i in range(X_HEADS + 1):
        if i < X_HEADS:
            s[i] = _dot_nt(q[:, heads[i]], mk_ref[0, :, heads[i]].astype(BF)) * (hd ** -0.5)
        j = i - 1
        if 0 <= j < X_HEADS:
            p = jnp.exp(s[j] - jnp.max(s[j], axis=-1, keepdims=True))
            p = p / jnp.sum(p, axis=-1, keepdims=True)
            o[j] = _dot(p.astype(BF), mv_ref[0, :, heads[j]].astype(BF)).astype(BF)
    attn = jnp.concatenate([o[h] for h in range(X_HEADS)], axis=-1)
    y_ref[...] = x + _rms(_dot(attn, wo_ref[...]), gpost_ref[...])


def _xattn_short_kernel(x_ref, mk_ref, mv_ref, wq_ref, wo_ref, gpre_ref, gpost_ref, y_ref,
                        q_scr, o_scr, *, d_model, seq):
    b = pl.program_id(0)

    @pl.when(b == 0)
    def _():
        q_scr[...] = _dot(_rms(x_ref[...], gpre_ref[...]).astype(BF), wq_ref[...]).astype(BF)

    rows = pl.ds(pl.multiple_of(b * seq, seq), seq)
    hd = d_model // X_HEADS
    for h in range(X_HEADS):
        hs = slice(h * hd, (h + 1) * hd)
        s = _dot_nt(q_scr[rows, hs], mk_ref[0, :, hs].astype(BF)) * (hd ** -0.5)
        p = jnp.exp(s - jnp.max(s, axis=-1, keepdims=True))
        p = p / jnp.sum(p, axis=-1, keepdims=True)
        o_scr[rows, hs] = _dot(p.astype(BF), mv_ref[0, :, hs].astype(BF)).astype(BF)

    @pl.when(b == pl.num_programs(0) - 1)
    def _():
        y_ref[...] = x_ref[...] + _rms(_dot(o_scr[...], wo_ref[...]), gpost_ref[...])


def _xattn_short(x, mk, mv, mem_index, wq, wo, gpre, gpost, batch, seq, layer):
    n, d = x.shape
    mem_len = mk.shape[-2]
    mem = pl.BlockSpec((1, mem_len, d), lambda b: (mem_index(b), 0, 0))
    return pl.pallas_call(
        functools.partial(_xattn_short_kernel, d_model=d, seq=seq),
        grid=(batch,),
        in_specs=[_const_spec((n, d)), mem, mem, _layer_spec(wq, layer), _layer_spec(wo, layer),
                  _const_spec((1, d)), _const_spec((1, d))],
        out_specs=pl.BlockSpec((n, d), lambda b: (0, 0)),
        out_shape=jax.ShapeDtypeStruct((n, d), F32),
        scratch_shapes=[pltpu.VMEM((n, d), BF), pltpu.VMEM((n, d), BF)],
        compiler_params=_cparams(("arbitrary",)),
        name="cross_attn_short",
    )(x, mk, mv, wq, wo, gpre, gpost)


def _xattn(x, mk, mv, mem_index, wq, wo, gpre, gpost, batch, seq, tm, layer):
    n, d = x.shape
    if seq < XATTN_SHORT_SEQ:
        return _xattn_short(x, mk, mv, mem_index, wq, wo, gpre, gpost, batch, seq, layer)
    nt = seq // tm
    mem_len = mk.shape[-2]
    row = pl.BlockSpec((tm, d), lambda b, i: (b * nt + i, 0))
    mem = pl.BlockSpec((1, mem_len, d), lambda b, i: (mem_index(b), 0, 0))
    return pl.pallas_call(
        functools.partial(_xattn_kernel, d_model=d),
        grid=(batch, nt),
        in_specs=[row, mem, mem, _layer_spec(wq, layer), _layer_spec(wo, layer),
                  _const_spec((1, d)), _const_spec((1, d))],
        out_specs=row,
        out_shape=jax.ShapeDtypeStruct((n, d), F32),
        compiler_params=_cparams(("parallel", "parallel")),
        name="cross_attn",
    )(x, mk, mv, wq, wo, gpre, gpost)


def _ffn_kernel(x_ref, wgu_ref, wo_ref, gpre_ref, gpost_ref, y_ref, *, chunk):
    x = x_ref[...]
    h = _rms(x, gpre_ref[...]).astype(BF)
    ff = wo_ref.shape[0]
    acc = jnp.zeros(x.shape, F32)
    for j in range(ff // chunk):
        cs = slice(j * chunk, (j + 1) * chunk)
        g = _dot(h, wgu_ref[:, cs])
        u = _dot(h, wgu_ref[:, ff + j * chunk:ff + (j + 1) * chunk])
        act = (g * _sigmoid(g) * u).astype(BF)
        acc = acc + _dot(act, wo_ref[cs, :])
    y_ref[...] = x + _rms(acc, gpost_ref[...])


def _ffn(x, wgu, wo, gpre, gpost, tm, chunk, layer):
    n, d = x.shape
    row = pl.BlockSpec((tm, d), lambda i: (i, 0))
    return pl.pallas_call(
        functools.partial(_ffn_kernel, chunk=chunk),
        grid=(n // tm,),
        in_specs=[row, _layer_spec(wgu, layer), _layer_spec(wo, layer),
                  _const_spec((1, d)), _const_spec((1, d))],
        out_specs=row,
        out_shape=jax.ShapeDtypeStruct((n, d), F32),
        compiler_params=_cparams(("parallel",)),
        name="ffn",
    )(x, wgu, wo, gpre, gpost)


def _memkv_kernel(m_ref, g_ref, wk_ref, wv_ref, k_ref, v_ref):
    m = _rms(m_ref[...], g_ref[0]).astype(BF)
    k_ref[0] = _dot(m, wk_ref[0].astype(BF))
    v_ref[0] = _dot(m, wv_ref[0].astype(BF))


def _memkv(mem, g, wk, wv, tm):
    depth, d, _ = wk.shape
    n = mem.shape[0]
    w = pl.BlockSpec((1, d, d), lambda l, i: (l, 0, 0))
    out = pl.BlockSpec((1, tm, d), lambda l, i: (l, i, 0))
    return pl.pallas_call(
        _memkv_kernel,
        grid=(depth, n // tm),
        in_specs=[pl.BlockSpec((tm, d), lambda l, i: (i, 0)),
                  pl.BlockSpec((1, 1, d), lambda l, i: (l, 0, 0)), w, w],
        out_specs=[out, out],
        out_shape=[jax.ShapeDtypeStruct((depth, n, d), F32)] * 2,
        compiler_params=_cparams(("parallel", "parallel")),
        name="mem_kv",
    )(mem, g, wk, wv)


def _pad_heads(w, axis):
    shape = w.shape
    w = w.reshape(shape[:axis] + (GLA_HEADS, GLA_D) + shape[axis + 1:])
    pad = [(0, 0)] * w.ndim
    pad[axis + 1] = (0, GLA_DP - GLA_D)
    w = jnp.pad(w, pad)
    return w.reshape(shape[:axis] + (GLA_WP,) + shape[axis + 1:])


def _cast_stacks(p):
    return dict(
        w_in_t=jnp.swapaxes(p['w_in'], 1, 2).astype(BF),
        w_a=p['w_branch_a'].astype(BF),
        w_b=p['w_branch_b'].astype(BF),
        w_c=_pad_heads(p['w_branch_c'], 1).astype(BF),
        w_mix=p['w_mix_out'].astype(BF),
        w_xq=p['w_xq'].astype(BF),
        w_xo=p['w_xo'].astype(BF),
        w_ffn_in=p['w_ffn_in'].astype(BF),
        w_ffn_out=p['w_ffn_out'].astype(BF),
    )


def _pack_layer(p, stacks, l):
    w_in = stacks['w_in_t'][l]
    o = 3 * SB_W + POOL_W
    sb_u = w_in[:o]
    gla = [_pad_heads(w_in[o + j * GLA_W:o + (j + 1) * GLA_W], 0) for j in range(4)]
    o += 4 * GLA_W
    lr = jnp.pad(w_in[o:o + GLA_RANK], ((0, LANE - GLA_RANK), (0, 0)))
    eye = jnp.eye(len(POOL_WINDOWS), dtype=F32)
    w_pool_bd = (eye[:, None, :, None] * p['w_pool'][l][:, :, None, :]).reshape(POOL_W, POOL_W)
    row = lambda v: v.reshape(1, -1)
    return dict(
        stacks, layer=l,
        w_in=jnp.concatenate([sb_u] + gla + [lr], axis=0),
        w_a2=jnp.pad(_pad_heads(p['w_gla_a2'][l], 1), ((0, LANE - GLA_RANK), (0, 0))).astype(BF),
        b_a=row(_pad_heads(p['b_gla_a'][l], 0)),
        gla_norm=row(_pad_heads(p['gla_norm'][l], 0)),
        w_pool=w_pool_bd.astype(BF),
        pool_scale=row(p['pool_scale'][l]),
        n_mix_pre=row(p['norm_mix_pre'][l]), n_mix_post=row(p['norm_mix_post'][l]),
        n_x_pre=row(p['norm_x_pre'][l]), n_x_post=row(p['norm_x_post'][l]),
        n_ffn_pre=row(p['norm_ffn_pre'][l]), n_ffn_post=row(p['norm_ffn_post'][l]),
    )


def _pad_state(s):
    s = jnp.swapaxes(s, -1, -2)
    return jnp.pad(s, ((0, 0), (0, 0), (0, GLA_DP - GLA_D), (0, GLA_DP - GLA_D)))


def _unpad_state(st):
    return jnp.swapaxes(st[:, :, :GLA_D, :GLA_D], -1, -2)


def _tiles(batch, seq):
    n = batch * seq
    return dict(
        proj=min(512, n),
        sb=min(256, seq),
        pool=min(1024, seq),
        gla=min(128, seq),
        merge=min(1024, n),
        xattn=min(1024, seq),
        ffn=min(1024, n),
    )


def _layer(x, w, batch, seq, sb_attend, pool_hist, pos0, s0t, mk, mv, mem_index, kv_stack=None):
    t = _tiles(batch, seq)
    q, k, v, kb, vb, u, qkvg, log_a = _proj(x, w['n_mix_pre'], w['w_in'], w['w_a2'], w['b_a'],
                                            t['proj'], kv_stack)
    oa = sb_attend(q, kb, vb)
    ob = _pool(u, pool_hist, w['w_pool'], w['pool_scale'], batch, seq, t['pool'], pos0)
    oc, st = _gla(qkvg, log_a, w['gla_norm'], s0t, batch, seq, t['gla'],
                  4 if seq % (4 * t['gla']) == 0 else 1)
    x = _merge(x, oa, ob, oc, w['w_in_t'], w['w_a'], w['w_b'], w['w_c'], w['w_mix'],
               w['n_mix_pre'], w['n_mix_post'], t['merge'], w['layer'])
    x = _xattn(x, mk, mv, mem_index, w['w_xq'], w['w_xo'], w['n_x_pre'], w['n_x_post'],
               batch, seq, t['xattn'], w['layer'])
    x = _ffn(x, w['w_ffn_in'], w['w_ffn_out'], w['n_ffn_pre'], w['n_ffn_post'], t['ffn'], 256,
             w['layer'])
    return x, k, v, u, st


def kernel(x_prompt, x_sample, mem_prompt, cache_sb_k, cache_sb_v, state_pool, state_gla, cache_mem_k, cache_mem_v, w_in, w_gla_a2, b_gla_a, gla_norm, w_pool, pool_scale, w_branch_a, w_branch_b, w_branch_c, w_mix_out, mem_norm, w_xq, w_xk, w_xv, w_xo, w_ffn_in, w_ffn_out, norm_mix_pre, norm_mix_post, norm_x_pre, norm_x_post, norm_ffn_pre, norm_ffn_post):
    p = dict(w_in=w_in, w_gla_a2=w_gla_a2, b_gla_a=b_gla_a, gla_norm=gla_norm, w_pool=w_pool,
             pool_scale=pool_scale, w_branch_a=w_branch_a, w_branch_b=w_branch_b,
             w_branch_c=w_branch_c, w_mix_out=w_mix_out, w_xq=w_xq, w_xo=w_xo,
             w_ffn_in=w_ffn_in, w_ffn_out=w_ffn_out, norm_mix_pre=norm_mix_pre,
             norm_mix_post=norm_mix_post, norm_x_pre=norm_x_pre, norm_x_post=norm_x_post,
             norm_ffn_pre=norm_ffn_pre, norm_ffn_post=norm_ffn_post)
    depth = w_in.shape[0]
    bp, tp, d = x_prompt.shape
    bs, ts, _ = x_sample.shape
    mem_len = mem_prompt.shape[1]
    past = cache_sb_k.shape[2]

    mk_all, mv_all = _memkv(mem_prompt.reshape(bp * mem_len, d), mem_norm.reshape(depth, 1, d),
                            w_xk, w_xv, min(512, bp * mem_len))
    mk_p = mk_all.reshape(depth * bp, mem_len, d)
    mv_p = mv_all.reshape(depth * bp, mem_len, d)
    mk_s = cache_mem_k.reshape(depth * bs, mem_len, d)
    mv_s = cache_mem_v.reshape(depth * bs, mem_len, d)
    ck = jnp.transpose(cache_sb_k, (0, 1, 3, 4, 2)).reshape(depth, bs, SB_W, past)
    cv = jnp.transpose(cache_sb_v, (0, 1, 3, 4, 2)).reshape(depth, bs, SB_W, past)

    yp = x_prompt.reshape(bp * tp, d)
    ys = x_sample.reshape(bs * ts, d)
    hist_zero = jnp.zeros((bp, HIST_PAD, POOL_W), F32)
    state_zero = jnp.zeros((bp, GLA_HEADS, GLA_DP, GLA_DP), F32)
    tiles_p = _tiles(bp, tp)
    kv_t = None
    poolp, glap, kss, vss, pools, glas = ([] for _ in range(6))
    stacks = _cast_stacks(p)
    for l in range(depth):
        w = _pack_layer(p, stacks, l)
        yp, k, v, u, st = _layer(
            yp, w, bp, tp,
            functools.partial(_sb_prompt, batch=bp, seq=tp, blk=tiles_p['sb'],
                              nq=4 if tp % (4 * tiles_p['sb']) == 0 else 1),
            hist_zero, 0, state_zero, mk_p, mv_p, lambda b, l=l: l * bp + b,
            kv_stack=(l, depth, bp, tp, kv_t))
        kv_t = (k, v)
        poolp.append(u.reshape(bp, tp, POOL_W)[:, tp - POOL_HIST:])
        glap.append(_unpad_state(st))

        hist = jnp.pad(state_pool[l], ((0, 0), (HIST_PAD - POOL_HIST, 0), (0, 0)))
        ys, k, v, u, st = _layer(
            ys, w, bs, ts,
            functools.partial(_sb_sample, cache_k=ck, cache_v=cv, layer=l, batch=bs, seq=ts,
                              blk=min(256, past)),
            hist, past, _pad_state(state_gla[l]), mk_s, mv_s, lambda b, l=l: l * bs + b)
        kss.append(k.reshape(bs, ts, SB_HEADS, SB_DIM))
        vss.append(v.reshape(bs, ts, SB_HEADS, SB_DIM))
        full = jnp.concatenate([state_pool[l], u.reshape(bs, ts, POOL_W)], axis=1)
        pools.append(full[:, -POOL_HIST:])
        glas.append(_unpad_state(st))

    kp, vp = (jnp.transpose(a.reshape(depth, bp, SB_HEADS, SB_DIM, tp), (0, 1, 4, 2, 3)) for a in kv_t)
    return (yp.reshape(bp, tp, d), ys.reshape(bs, ts, d),
            kp, vp, jnp.stack(poolp), jnp.stack(glap),
            mk_all.reshape(depth, bp, mem_len, X_HEADS, d // X_HEADS),
            mv_all.reshape(depth, bp, mem_len, X_HEADS, d // X_HEADS),
            jnp.stack(kss), jnp.stack(vss), jnp.stack(pools), jnp.stack(glas))
```

```python
import functools

import numpy as np
import jax
import jax.numpy as jnp
from jax import lax
from jax.experimental import pallas as pl
from jax.experimental.pallas import tpu as pltpu

BF = jnp.bfloat16
F32 = jnp.float32
EPS = 1e-6
LOG2_E = 1.4426950408889634

LANE = 128
VMEM_LIMIT = 56 * 1024 * 1024

SB_HEADS = 6
SB_DIM = 64
SB_W = SB_HEADS * SB_DIM
POOL_WINDOWS = (2, 4, 8, 16)
POOL_GDIM = 64
POOL_W = len(POOL_WINDOWS) * POOL_GDIM
POOL_HIST = max(POOL_WINDOWS) - 1
HIST_PAD = 16
assert all(w & (w - 1) == 0 for w in POOL_WINDOWS) and 2 * POOL_GDIM == LANE and POOL_HIST <= HIST_PAD
GLA_HEADS = 4
GLA_D = 96
GLA_DP = LANE
GLA_W = GLA_HEADS * GLA_D
GLA_WP = GLA_HEADS * GLA_DP
GLA_RANK = 16
GLA_TAU = 16.0
X_HEADS = 4
XATTN_SHORT_SEQ = 128

C_Q = 0
C_K = C_Q + SB_W
C_V = C_K + SB_W
C_U = C_V + SB_W
C_GLA = C_U + POOL_W
C_LR = C_GLA + 4 * GLA_WP
C_END = C_LR + LANE


def _layer_spec(stacked, layer):
    return pl.BlockSpec((None,) + stacked.shape[1:], lambda *_: (layer, 0, 0),
                        pipeline_mode=pl.Buffered(1))


def _cparams(sem):
    return pltpu.CompilerParams(dimension_semantics=sem, vmem_limit_bytes=VMEM_LIMIT)


def _rms(x, g):
    return x * lax.rsqrt(jnp.mean(x * x, axis=-1, keepdims=True) + EPS) * g


def _softplus_neg_abs(z):
    return jnp.log(1.0 + jnp.exp(-jnp.abs(z)))


def _sigmoid(z):
    return 0.5 * jnp.tanh(0.5 * z) + 0.5


def _dot(a, b):
    return jnp.dot(a, b, preferred_element_type=F32)


def _dot_nt(a, b):
    return lax.dot_general(a, b, (((1,), (1,)), ((), ())), preferred_element_type=F32)


def _dot_tn(a, b):
    return lax.dot_general(a, b, (((0,), (0,)), ((), ())), preferred_element_type=F32)


def _const_spec(shape):
    zeros = (0,) * len(shape)
    return pl.BlockSpec(shape, lambda *_: zeros, pipeline_mode=pl.Buffered(1))


def _proj_kernel(*refs, n_alias, transposed_kv):
    x_ref, g_ref, w_ref, wa_ref, ba_ref = refs[:N_PROJ_IN]
    q_ref, k_ref, v_ref, kb_ref, vb_ref, u_ref, gla_ref, la_ref = refs[N_PROJ_IN + n_alias:]
    h = _rms(x_ref[...], g_ref[...]).astype(BF)

    def mm(lo, width):
        return _dot_nt(h, w_ref[lo:lo + width, :])

    q_ref[...] = (mm(C_Q, SB_W) * (SB_DIM ** -0.5)).astype(BF)
    for lo, f32_ref, bf_ref in ((C_K, k_ref, kb_ref), (C_V, v_ref, vb_ref)):
        kv = mm(lo, SB_W)
        bf_ref[...] = kv.astype(BF)
        if transposed_kv is None:
            f32_ref[...] = kv
        else:
            for l in range(f32_ref.shape[0]):
                f32_ref[l, 0] = kv.T if l == transposed_kv else jnp.zeros((SB_W, kv.shape[0]), F32)
    u_ref[...] = mm(C_U, POOL_W)
    for j in range(4):
        gla_ref[:, j * GLA_WP:(j + 1) * GLA_WP] = mm(C_GLA + j * GLA_WP, GLA_WP).astype(BF)
    xa = _dot(mm(C_LR, LANE).astype(BF), wa_ref[...]) + ba_ref[...]
    la_ref[...] = ((jnp.minimum(xa, 0.0) - _softplus_neg_abs(xa)) * (LOG2_E / GLA_TAU)).astype(BF)


N_PROJ_IN = 5


def _proj(x, g, w, w_layer, wa, ba, tm, kv_stack=None):
    n, d = x.shape
    row = lambda width: pl.BlockSpec((tm, width), lambda i: (i, 0))
    if kv_stack is None:
        kv_spec, kv_shape, extra, aliases = row(SB_W), (n, SB_W), [], {}
    else:
        layer, depth, batch, seq, bufs = kv_stack
        nt = seq // tm
        kv_shape = (depth, batch, SB_W, seq)
        if bufs is None:
            kv_spec = pl.BlockSpec((depth, 1, SB_W, tm), lambda i: (0, i // nt, 0, i % nt))
            kv_slot = layer
        else:
            kv_spec = pl.BlockSpec((1, 1, SB_W, tm), lambda i: (layer, i // nt, 0, i % nt))
            kv_slot = 0
        extra = [] if bufs is None else list(bufs)
        aliases = {N_PROJ_IN + j: 1 + j for j in range(len(extra))}
    outs = [
        (row(SB_W), (n, SB_W), BF), (kv_spec, kv_shape, F32), (kv_spec, kv_shape, F32),
        (row(SB_W), (n, SB_W), BF), (row(SB_W), (n, SB_W), BF), (row(POOL_W), (n, POOL_W), F32),
        (row(4 * GLA_WP), (n, 4 * GLA_WP), BF), (row(GLA_WP), (n, GLA_WP), BF),
    ]
    return pl.pallas_call(
        functools.partial(_proj_kernel, n_alias=len(extra),
                          transposed_kv=None if kv_stack is None else kv_slot),
        grid=(n // tm,),
        in_specs=[row(d), _const_spec((1, d)), _layer_spec(w, w_layer),
                  _const_spec((LANE, GLA_WP)), _const_spec((1, GLA_WP))]
                 + [pl.BlockSpec(memory_space=pl.ANY)] * len(extra),
        out_specs=[spec for spec, _, _ in outs],
        out_shape=[jax.ShapeDtypeStruct(shape, dt) for _, shape, dt in outs],
        input_output_aliases=aliases,
        compiler_params=_cparams(("parallel",)),
        name="in_proj",
    )(x, g, w, wa, ba, *extra)


def _upper_ones(n):
    r = lax.broadcasted_iota(jnp.int32, (n, n), 0)
    c = lax.broadcasted_iota(jnp.int32, (n, n), 1)
    return jnp.where(r >= c, 1.0, 0.0).astype(BF)


def _strict_causal(n):
    r = lax.broadcasted_iota(jnp.int32, (n, n), 0)
    c = lax.broadcasted_iota(jnp.int32, (n, n), 1)
    return c < r


def _sb_weights(z, carry, upper, mask):
    return _sb_finish(z, _dot(_sb_softplus(z, mask), upper), carry, mask)


def _sb_softplus(z, mask):
    zb = z.astype(BF)
    t = jnp.maximum(zb, 0.0) + _softplus_neg_abs(zb)
    return t if mask is None else jnp.where(mask, t, jnp.zeros_like(t))


def _sb_finish(z, cum, carry, mask):
    w = jnp.exp(z - cum - carry)
    if mask is not None:
        w = jnp.where(mask, w, 0.0)
    return w.astype(BF), carry + cum[:, 0:1]


SB_SKIP = "skip"
SB_LAG = 1


def _sb_prompt_kernel(q_ref, k_ref, v_ref, o_ref, acc_ref, *, blk, nq):
    first = pl.program_id(1) * nq
    pair_w = 2 * SB_DIM
    n_pairs = SB_HEADS // 2
    upper = _upper_ones(blk)
    causal = _strict_causal(blk)
    low = lax.broadcasted_iota(jnp.int32, (blk, pair_w), 1) < SB_DIM
    zero = jnp.zeros((blk, pair_w), BF)

    def split(x):
        return jnp.where(low, x, zero), jnp.where(low, zero, x)

    q_heads = []
    for s in range(nq):
        heads = []
        for p in range(n_pairs):
            heads.extend(split(q_ref[s * blk:(s + 1) * blk, p * pair_w:(p + 1) * pair_w]))
        q_heads.append(heads)
    acc_ref[...] = jnp.zeros(acc_ref.shape, F32)

    def visit(kb, carries, masks):
        start = pl.multiple_of(kb * blk, blk)
        out = list(carries)
        pairs = [slice(p * pair_w, (p + 1) * pair_w) for p in range(n_pairs)]
        live = [(s, 2 * p + e) for p in range(n_pairs) for s in range(nq)
                if masks[s] is not SB_SKIP for e in range(2)]
        k_pairs = [k_ref[pl.ds(start, blk), ps] for ps in pairs]
        v_pairs = [jnp.concatenate(split(v_ref[pl.ds(start, blk), ps]), axis=0) for ps in pairs]
        z, cum, w = {}, {}, {}
        for i in range(len(live) + 2 * SB_LAG):
            if i < len(live):
                s, h = live[i]
                z[i] = _dot_nt(q_heads[s][h], k_pairs[h // 2])
            j = i - SB_LAG
            if 0 <= j < len(live):
                cum[j] = _dot(_sb_softplus(z[j], masks[live[j][0]]), upper)
            j = i - 2 * SB_LAG
            if 0 <= j < len(live):
                s, h = live[j]
                c = s * SB_HEADS + h
                w[j], out[c] = _sb_finish(z[j], cum[j], carries[c], masks[s])
                if h % 2 == 1:
                    w_pair = jnp.concatenate([w[j - 1], w[j]], axis=1)
                    ps = pairs[h // 2]
                    acc_ref[s * blk:(s + 1) * blk, ps] += _dot(w_pair, v_pairs[h // 2])
        return tuple(out)

    carries = tuple(jnp.zeros((blk, 1), F32) for _ in range(nq * SB_HEADS))
    for d in reversed(range(nq)):
        masks = tuple(SB_SKIP if d > s else (causal if d == s else None) for s in range(nq))
        carries = visit(first + d, carries, masks)
    lax.fori_loop(0, first, lambda j, c: visit(first - 1 - j, c, (None,) * nq), carries)
    o_ref[...] = acc_ref[...].astype(BF)


def _sb_prompt(q, kb, vb, batch, seq, blk, nq):
    nblk = seq // (blk * nq)
    return pl.pallas_call(
        functools.partial(_sb_prompt_kernel, blk=blk, nq=nq),
        grid=(batch, nblk),
        in_specs=[
            pl.BlockSpec((nq * blk, SB_W), lambda b, i: (b * nblk + i, 0)),
            pl.BlockSpec((seq, SB_W), lambda b, i: (b, 0)),
            pl.BlockSpec((seq, SB_W), lambda b, i: (b, 0)),
        ],
        out_specs=pl.BlockSpec((nq * blk, SB_W), lambda b, i: (b * nblk + i, 0)),
        out_shape=jax.ShapeDtypeStruct((batch * seq, SB_W), BF),
        scratch_shapes=[pltpu.VMEM((nq * blk, SB_W), F32)],
        compiler_params=_cparams(("parallel", "arbitrary")),
        name="sb_prompt",
    )(q, kb, vb)


def _sb_sample_kernel(q_ref, kn_ref, vn_ref, kc_ref, vc_ref, o_ref, *, seq, past, blk):
    pair_w = 2 * SB_DIM
    n_pairs = SB_HEADS // 2
    pairs = [slice(p * pair_w, (p + 1) * pair_w) for p in range(n_pairs)]

    def split(x, axis):
        low = lax.broadcasted_iota(jnp.int32, x.shape, axis) < SB_DIM
        zero = jnp.zeros(x.shape, x.dtype)
        return jnp.where(low, x, zero), jnp.where(low, zero, x)

    q_heads = []
    for ps in pairs:
        q_heads.extend(split(q_ref[:, ps], 1))
    r = lax.broadcasted_iota(jnp.int32, (SB_HEADS * seq, seq), 0) % seq
    causal = lax.broadcasted_iota(jnp.int32, (SB_HEADS * seq, seq), 1) < r

    def visit(k_pairs, v_pairs, transposed, carry, acc, upper, mask):
        qk = _dot if transposed else _dot_nt
        pv = _dot_nt if transposed else _dot
        z = jnp.concatenate([qk(q_heads[h], k_pairs[h // 2]) for h in range(SB_HEADS)], axis=0)
        w, carry = _sb_weights(z, carry, upper, mask)
        out = []
        for p in range(n_pairs):
            v_lo, v_hi = split(v_pairs[p], 0 if transposed else 1)
            w_lo = w[(2 * p) * seq:(2 * p + 1) * seq]
            w_hi = w[(2 * p + 1) * seq:(2 * p + 2) * seq]
            out.append(acc[p] + pv(w_lo, v_lo) + pv(w_hi, v_hi))
        return carry, out

    carry = jnp.zeros((SB_HEADS * seq, 1), F32)
    acc = [jnp.zeros((seq, pair_w), F32) for _ in pairs]
    carry, acc = visit([kn_ref[:, ps] for ps in pairs], [vn_ref[:, ps] for ps in pairs], False,
                       carry, acc, _upper_ones(seq), causal)
    upper_past = _upper_ones(blk)
    for j in reversed(range(past // blk)):
        cols = slice(j * blk, (j + 1) * blk)
        carry, acc = visit([kc_ref[0, 0, ps, cols].astype(BF) for ps in pairs],
                           [vc_ref[0, 0, ps, cols].astype(BF) for ps in pairs], True,
                           carry, acc, upper_past, None)
    for p, ps in enumerate(pairs):
        o_ref[:, ps] = acc[p].astype(BF)


def _sb_sample(q, kb, vb, cache_k, cache_v, layer, batch, seq, blk):
    past = cache_k.shape[3]
    new = pl.BlockSpec((seq, SB_W), lambda b: (b, 0))
    cache = pl.BlockSpec((1, 1, SB_W, past), lambda b: (layer, b, 0, 0))
    return pl.pallas_call(
        functools.partial(_sb_sample_kernel, seq=seq, past=past, blk=blk),
        grid=(batch,),
        in_specs=[new, new, new, cache, cache],
        out_specs=new,
        out_shape=jax.ShapeDtypeStruct((batch * seq, SB_W), BF),
        compiler_params=_cparams(("parallel",)),
        name="sb_sample",
    )(q, kb, vb, cache_k, cache_v)


def _pool_kernel(u_ref, hist_ref, w_ref, scale_ref, o_ref, buf_ref, *, tm, pos0):
    i = pl.program_id(1)

    @pl.when(i == 0)
    def _():
        buf_ref[0:HIST_PAD, :] = hist_ref[0]

    @pl.when(i > 0)
    def _():
        buf_ref[0:HIST_PAD, :] = buf_ref[tm:tm + HIST_PAD, :]

    u = u_ref[...]
    buf_ref[HIST_PAD:HIST_PAD + tm, :] = u

    rows = tm + HIST_PAD
    low = lax.broadcasted_iota(jnp.int32, (tm, LANE), 1) < POOL_GDIM
    pos1 = pos0 + i * tm + lax.broadcasted_iota(jnp.int32, (tm, LANE), 0) + 1
    pooled = []
    for tile in range(POOL_W // LANE):
        cols = slice(tile * LANE, (tile + 1) * LANE)
        w_lo, w_hi = POOL_WINDOWS[2 * tile], POOL_WINDOWS[2 * tile + 1]
        s = buf_ref[0:rows, cols]
        sums, w = {}, 1
        while w < w_hi:
            s = s + pltpu.roll(s, w, 0)
            w *= 2
            sums[w] = s
        win_sum = jnp.where(low, sums[w_lo][HIST_PAD:], sums[w_hi][HIST_PAD:])
        cnt = jnp.minimum(pos1, jnp.where(low, w_lo, w_hi)).astype(F32)
        pooled.append(win_sum / cnt - u[:, cols])
    pooled = jnp.concatenate(pooled, axis=1)
    o_ref[...] = (_dot(pooled.astype(BF), w_ref[...]) * scale_ref[...]).astype(BF)


def _pool(u, hist, w_bd, scale, batch, seq, tm, pos0):
    nt = seq // tm
    return pl.pallas_call(
        functools.partial(_pool_kernel, tm=tm, pos0=pos0),
        grid=(batch, nt),
        in_specs=[
            pl.BlockSpec((tm, POOL_W), lambda b, i: (b * nt + i, 0)),
            pl.BlockSpec((1, HIST_PAD, POOL_W), lambda b, i: (b, 0, 0)),
            _const_spec((POOL_W, POOL_W)),
            _const_spec((1, POOL_W)),
        ],
        out_specs=pl.BlockSpec((tm, POOL_W), lambda b, i: (b * nt + i, 0)),
        out_shape=jax.ShapeDtypeStruct((batch * seq, POOL_W), BF),
        scratch_shapes=[pltpu.VMEM((tm + HIST_PAD, POOL_W), F32)],
        compiler_params=_cparams(("parallel", "arbitrary")),
        name="pool_mix",
    )(u, hist, w_bd, scale)


def _gla_levels(rows):
    out, s = [], 1
    while s < rows:
        out.append(s)
        s *= 2
    return out


def _gla_exponent_matrix(rows):
    t = np.arange(rows)[:, None]
    j = np.arange(rows)[None, :]
    blocks = [j <= t]
    for s in _gla_levels(rows)[1:]:
        start = (t // s) * s
        later = (t // s) % 2 == 1
        blocks.append(np.where(later, (j >= start) & (j <= t), (j > t) & (j <= start + s - 1)))
    return np.concatenate(blocks, axis=0).astype(np.float32)


def _gla_later_mask(rows):
    t = np.arange(rows)[:, None]
    blocks = [np.broadcast_to((t // s) % 2 == 1, (rows, GLA_WP)) for s in _gla_levels(rows)]
    return np.concatenate(blocks, axis=0).astype(np.float32)


def _gla_kernel(qkvg_ref, la_ref, em_ref, lm_ref, gn_ref, s0_ref, o_ref, st_ref, *, rows, sub):
    @pl.when(pl.program_id(1) == 0)
    def _():
        st_ref[...] = s0_ref[...]

    for j in range(sub):
        _gla_tile(qkvg_ref, la_ref, em_ref, lm_ref, gn_ref, o_ref, st_ref,
                  slice(j * rows, (j + 1) * rows), rows)


def _gla_tile(qkvg_ref, la_ref, em_ref, lm_ref, gn_ref, o_ref, st_ref, rs, rows):
    expo = _dot(em_ref[...], la_ref[rs, :])

    qb = qkvg_ref[rs, 0:GLA_WP]
    kb = qkvg_ref[rs, GLA_WP:2 * GLA_WP]
    vb = qkvg_ref[rs, 2 * GLA_WP:3 * GLA_WP]
    gate = qkvg_ref[rs, 3 * GLA_WP:4 * GLA_WP].astype(F32)

    b = expo[0:rows]
    b_last = b[rows - 1:rows]
    q_dec = qb * jnp.exp2(b).astype(BF)
    k_dec = kb * jnp.exp2(b_last - b).astype(BF)

    row = lax.broadcasted_iota(jnp.int32, (rows, GLA_WP), 0)
    r = lax.broadcasted_iota(jnp.int32, (rows, rows), 0)
    c = lax.broadcasted_iota(jnp.int32, (rows, rows), 1)
    heads = [slice(h * GLA_DP, (h + 1) * GLA_DP) for h in range(GLA_HEADS)]

    att = [jnp.where(r == c, _dot_nt(qb[:, hs], kb[:, hs]), 0.0) for hs in heads]
    for n, s in enumerate(_gla_levels(rows)):
        shift = s.bit_length() - 1
        rb = r >> shift
        pair = ((rb & 1) == 1) & ((c >> shift) == rb - 1)
        if s == 1:
            e = jnp.where((row & 1) == 1, la_ref[rs, :].astype(F32), 0.0)
        else:
            e = expo[n * rows:(n + 1) * rows]
        later = lm_ref[n * rows:(n + 1) * rows] > 0
        x_lvl = jnp.where(later, qb, kb) * jnp.exp2(e).astype(BF)
        for h, hs in enumerate(heads):
            att[h] = jnp.where(pair, _dot_nt(x_lvl[:, hs], x_lvl[:, hs]), att[h])

    for h, hs in enumerate(heads):
        st = st_ref[0, h]
        o = (_dot(att[h].astype(BF), vb[:, hs]) + _dot_nt(q_dec[:, hs], st.astype(BF))) * (GLA_D ** -0.5)
        ms = jnp.sum(o * o, axis=-1, keepdims=True) * (1.0 / GLA_D)
        y = o * lax.rsqrt(ms + EPS) * gn_ref[:, hs]
        gh = gate[:, hs]
        o_ref[rs, hs] = (y * (gh * _sigmoid(gh))).astype(BF)
        st_ref[0, h] = st * jnp.exp2(b_last[:, hs]) + _dot_tn(vb[:, hs], k_dec[:, hs])


def _gla(qkvg, log_a, gn, s0t, batch, seq, rows, sub):
    nt = seq // (rows * sub)
    em = jnp.asarray(_gla_exponent_matrix(rows), BF)
    lm = jnp.asarray(_gla_later_mask(rows), BF)
    state = pl.BlockSpec((1, GLA_HEADS, GLA_DP, GLA_DP), lambda b, i: (b, 0, 0, 0))
    return pl.pallas_call(
        functools.partial(_gla_kernel, rows=rows, sub=sub),
        grid=(batch, nt),
        in_specs=[
            pl.BlockSpec((rows * sub, 4 * GLA_WP), lambda b, i: (b * nt + i, 0)),
            pl.BlockSpec((rows * sub, GLA_WP), lambda b, i: (b * nt + i, 0)),
            _const_spec(em.shape),
            _const_spec(lm.shape),
            _const_spec((1, GLA_WP)),
            state,
        ],
        out_specs=[pl.BlockSpec((rows * sub, GLA_WP), lambda b, i: (b * nt + i, 0)), state],
        out_shape=[jax.ShapeDtypeStruct((batch * seq, GLA_WP), BF),
                   jax.ShapeDtypeStruct((batch, GLA_HEADS, GLA_DP, GLA_DP), F32)],
        compiler_params=_cparams(("parallel", "arbitrary")),
        name="gla",
    )(qkvg, log_a, em, lm, gn, s0t)


def _merge_kernel(x_ref, oa_ref, ob_ref, oc_ref, win_ref, wa_ref, wb_ref, wc_ref, wo_ref,
                  gpre_ref, gpost_ref, y_ref, *, d_model, gate_row0):
    x = x_ref[...]
    h = _rms(x, gpre_ref[...]).astype(BF)
    merged = None
    for j, (o_ref, w_ref) in enumerate(((oa_ref, wa_ref), (ob_ref, wb_ref), (oc_ref, wc_ref))):
        rows = slice(gate_row0 + j * d_model, gate_row0 + (j + 1) * d_model)
        gate = _sigmoid(_dot_nt(h, win_ref[rows, :]))
        term = gate * _dot(o_ref[...], w_ref[...])
        merged = term if merged is None else merged + term
    mix = _dot(merged.astype(BF), wo_ref[...])
    y_ref[...] = x + _rms(mix, gpost_ref[...])


def _merge(x, oa, ob, oc, w_in_t, wa, wb, wc, wo, gpre, gpost, tm, layer):
    n, d = x.shape
    row = lambda width: pl.BlockSpec((tm, width), lambda i: (i, 0))
    return pl.pallas_call(
        functools.partial(_merge_kernel, d_model=d, gate_row0=w_in_t.shape[1] - 3 * d),
        grid=(n // tm,),
        in_specs=[row(d), row(SB_W), row(POOL_W), row(GLA_WP)]
                 + [_layer_spec(w, layer) for w in (w_in_t, wa, wb, wc, wo)]
                 + [_const_spec((1, d)), _const_spec((1, d))],
        out_specs=row(d),
        out_shape=jax.ShapeDtypeStruct((n, d), F32),
        compiler_params=_cparams(("parallel",)),
        name="merge_out",
    )(x, oa, ob, oc, w_in_t, wa, wb, wc, wo, gpre, gpost)


def _xattn_kernel(x_ref, mk_ref, mv_ref, wq_ref, wo_ref, gpre_ref, gpost_ref, y_ref, *, d_model):
    x = x_ref[...]
    q = _dot(_rms(x, gpre_ref[...]).astype(BF), wq_ref[...]).astype(BF)
    hd = d_model // X_HEADS
    heads = [slice(h * hd, (h + 1) * hd) for h in range(X_HEADS)]
    s, o = {}, {}
    for i in range(X_HEADS + 1):
        if i < X_HEADS:
            s[i] = _dot_nt(q[:, heads[i]], mk_ref[0, :, heads[i]].astype(BF)) * (hd ** -0.5)
        j = i - 1
        if 0 <= j < X_HEADS:
            p = jnp.exp(s[j] - jnp.max(s[j], axis=-1, keepdims=True))
            p = p / jnp.sum(p, axis=-1, keepdims=True)
            o[j] = _dot(p.astype(BF), mv_ref[0, :, heads[j]].astype(BF)).astype(BF)
    attn = jnp.concatenate([o[h] for h in range(X_HEADS)], axis=-1)
    y_ref[...] = x + _rms(_dot(attn, wo_ref[...]), gpost_ref[...])


def _xattn_short_kernel(x_ref, mk_ref, mv_ref, wq_ref, wo_ref, gpre_ref, gpost_ref, y_ref,
                        q_scr, o_scr, *, d_model, seq):
    b = pl.program_id(0)

    @pl.when(b == 0)
    def _():
        q_scr[...] = _dot(_rms(x_ref[...], gpre_ref[...]).astype(BF), wq_ref[...]).astype(BF)

    rows = pl.ds(pl.multiple_of(b * seq, seq), seq)
    hd = d_model // X_HEADS
    for h in range(X_HEADS):
        hs = slice(h * hd, (h + 1) * hd)
        s = _dot_nt(q_scr[rows, hs], mk_ref[0, :, hs].astype(BF)) * (hd ** -0.5)
        p = jnp.exp(s - jnp.max(s, axis=-1, keepdims=True))
        p = p / jnp.sum(p, axis=-1, keepdims=True)
        o_scr[rows, hs] = _dot(p.astype(BF), mv_ref[0, :, hs].astype(BF)).astype(BF)

    @pl.when(b == pl.num_programs(0) - 1)
    def _():
        y_ref[...] = x_ref[...] + _rms(_dot(o_scr[...], wo_ref[...]), gpost_ref[...])


def _xattn_short(x, mk, mv, mem_index, wq, wo, gpre, gpost, batch, seq, layer):
    n, d = x.shape
    mem_len = mk.shape[-2]
    mem = pl.BlockSpec((1, mem_len, d), lambda b: (mem_index(b), 0, 0))
    return pl.pallas_call(
        functools.partial(_xattn_short_kernel, d_model=d, seq=seq),
        grid=(batch,),
        in_specs=[_const_spec((n, d)), mem, mem, _layer_spec(wq, layer), _layer_spec(wo, layer),
                  _const_spec((1, d)), _const_spec((1, d))],
        out_specs=pl.BlockSpec((n, d), lambda b: (0, 0)),
        out_shape=jax.ShapeDtypeStruct((n, d), F32),
        scratch_shapes=[pltpu.VMEM((n, d), BF), pltpu.VMEM((n, d), BF)],
        compiler_params=_cparams(("arbitrary",)),
        name="cross_attn_short",
    )(x, mk, mv, wq, wo, gpre, gpost)


def _xattn(x, mk, mv, mem_index, wq, wo, gpre, gpost, batch, seq, tm, layer):
    n, d = x.shape
    if seq < XATTN_SHORT_SEQ:
        return _xattn_short(x, mk, mv, mem_index, wq, wo, gpre, gpost, batch, seq, layer)
    nt = seq // tm
    mem_len = mk.shape[-2]
    row = pl.BlockSpec((tm, d), lambda b, i: (b * nt + i, 0))
    mem = pl.BlockSpec((1, mem_len, d), lambda b, i: (mem_index(b), 0, 0))
    return pl.pallas_call(
        functools.partial(_xattn_kernel, d_model=d),
        grid=(batch, nt),
        in_specs=[row, mem, mem, _layer_spec(wq, layer), _layer_spec(wo, layer),
                  _const_spec((1, d)), _const_spec((1, d))],
        out_specs=row,
        out_shape=jax.ShapeDtypeStruct((n, d), F32),
        compiler_params=_cparams(("parallel", "parallel")),
        name="cross_attn",
    )(x, mk, mv, wq, wo, gpre, gpost)


def _ffn_kernel(x_ref, wgu_ref, wo_ref, gpre_ref, gpost_ref, y_ref, *, chunk):
    x = x_ref[...]
    h = _rms(x, gpre_ref[...]).astype(BF)
    ff = wo_ref.shape[0]
    acc = jnp.zeros(x.shape, F32)
    for j in range(ff // chunk):
        cs = slice(j * chunk, (j + 1) * chunk)
        g = _dot(h, wgu_ref[:, cs])
        u = _dot(h, wgu_ref[:, ff + j * chunk:ff + (j + 1) * chunk])
        act = (g * _sigmoid(g) * u).astype(BF)
        acc = acc + _dot(act, wo_ref[cs, :])
    y_ref[...] = x + _rms(acc, gpost_ref[...])


def _ffn(x, wgu, wo, gpre, gpost, tm, chunk, layer):
    n, d = x.shape
    row = pl.BlockSpec((tm, d), lambda i: (i, 0))
    return pl.pallas_call(
        functools.partial(_ffn_kernel, chunk=chunk),
        grid=(n // tm,),
        in_specs=[row, _layer_spec(wgu, layer), _layer_spec(wo, layer),
                  _const_spec((1, d)), _const_spec((1, d))],
        out_specs=row,
        out_shape=jax.ShapeDtypeStruct((n, d), F32),
        compiler_params=_cparams(("parallel",)),
        name="ffn",
    )(x, wgu, wo, gpre, gpost)


def _memkv_kernel(m_ref, g_ref, wk_ref, wv_ref, k_ref, v_ref):
    m = _rms(m_ref[...], g_ref[0]).astype(BF)
    k_ref[0] = _dot(m, wk_ref[0].astype(BF))
    v_ref[0] = _dot(m, wv_ref[0].astype(BF))


def _memkv(mem, g, wk, wv, tm):
    depth, d, _ = wk.shape
    n = mem.shape[0]
    w = pl.BlockSpec((1, d, d), lambda l, i: (l, 0, 0))
    out = pl.BlockSpec((1, tm, d), lambda l, i: (l, i, 0))
    return pl.pallas_call(
        _memkv_kernel,
        grid=(depth, n // tm),
        in_specs=[pl.BlockSpec((tm, d), lambda l, i: (i, 0)),
                  pl.BlockSpec((1, 1, d), lambda l, i: (l, 0, 0)), w, w],
        out_specs=[out, out],
        out_shape=[jax.ShapeDtypeStruct((depth, n, d), F32)] * 2,
        compiler_params=_cparams(("parallel", "parallel")),
        name="mem_kv",
    )(mem, g, wk, wv)


def _pad_heads(w, axis):
    shape = w.shape
    w = w.reshape(shape[:axis] + (GLA_HEADS, GLA_D) + shape[axis + 1:])
    pad = [(0, 0)] * w.ndim
    pad[axis + 1] = (0, GLA_DP - GLA_D)
    w = jnp.pad(w, pad)
    return w.reshape(shape[:axis] + (GLA_WP,) + shape[axis + 1:])


def _cast_stacks(p):
    w_in_t = jnp.swapaxes(p['w_in'], 1, 2).astype(BF)
    o = 3 * SB_W + POOL_W
    gla = [_pad_heads(w_in_t[:, o + j * GLA_W:o + (j + 1) * GLA_W], 1) for j in range(4)]
    o += 4 * GLA_W
    lr = jnp.pad(w_in_t[:, o:o + GLA_RANK], ((0, 0), (0, LANE - GLA_RANK), (0, 0)))
    return dict(
        w_in_t=w_in_t,
        w_in=jnp.concatenate([w_in_t[:, :3 * SB_W + POOL_W]] + gla + [lr], axis=1),
        w_a=p['w_branch_a'].astype(BF),
        w_b=p['w_branch_b'].astype(BF),
        w_c=_pad_heads(p['w_branch_c'], 1).astype(BF),
        w_mix=p['w_mix_out'].astype(BF),
        w_xq=p['w_xq'].astype(BF),
        w_xo=p['w_xo'].astype(BF),
        w_ffn_in=p['w_ffn_in'].astype(BF),
        w_ffn_out=p['w_ffn_out'].astype(BF),
    )


def _pack_layer(p, stacks, l):
    eye = jnp.eye(len(POOL_WINDOWS), dtype=F32)
    w_pool_bd = (eye[:, None, :, None] * p['w_pool'][l][:, :, None, :]).reshape(POOL_W, POOL_W)
    row = lambda v: v.reshape(1, -1)
    return dict(
        stacks, layer=l,
        w_a2=jnp.pad(_pad_heads(p['w_gla_a2'][l], 1), ((0, LANE - GLA_RANK), (0, 0))).astype(BF),
        b_a=row(_pad_heads(p['b_gla_a'][l], 0)),
        gla_norm=row(_pad_heads(p['gla_norm'][l], 0)),
        w_pool=w_pool_bd.astype(BF),
        pool_scale=row(p['pool_scale'][l]),
        n_mix_pre=row(p['norm_mix_pre'][l]), n_mix_post=row(p['norm_mix_post'][l]),
        n_x_pre=row(p['norm_x_pre'][l]), n_x_post=row(p['norm_x_post'][l]),
        n_ffn_pre=row(p['norm_ffn_pre'][l]), n_ffn_post=row(p['norm_ffn_post'][l]),
    )


def _pad_state(s):
    s = jnp.swapaxes(s, -1, -2)
    return jnp.pad(s, ((0, 0), (0, 0), (0, GLA_DP - GLA_D), (0, GLA_DP - GLA_D)))


def _unpad_state(st):
    return jnp.swapaxes(st[:, :, :GLA_D, :GLA_D], -1, -2)


def _tiles(batch, seq):
    n = batch * seq
    return dict(
        proj=min(512, n),
        sb=min(256, seq),
        pool=min(1024, seq),
        gla=min(128, seq),
        merge=min(1024, n),
        xattn=min(1024, seq),
        ffn=min(1024, n),
    )


def _layer(x, w, batch, seq, sb_attend, pool_hist, pos0, s0t, mk, mv, mem_index, kv_stack=None):
    t = _tiles(batch, seq)
    q, k, v, kb, vb, u, qkvg, log_a = _proj(x, w['n_mix_pre'], w['w_in'], w['layer'], w['w_a2'],
                                            w['b_a'], t['proj'], kv_stack)
    oa = sb_attend(q, kb, vb)
    ob = _pool(u, pool_hist, w['w_pool'], w['pool_scale'], batch, seq, t['pool'], pos0)
    oc, st = _gla(qkvg, log_a, w['gla_norm'], s0t, batch, seq, t['gla'],
                  4 if seq % (4 * t['gla']) == 0 else 1)
    x = _merge(x, oa, ob, oc, w['w_in_t'], w['w_a'], w['w_b'], w['w_c'], w['w_mix'],
               w['n_mix_pre'], w['n_mix_post'], t['merge'], w['layer'])
    x = _xattn(x, mk, mv, mem_index, w['w_xq'], w['w_xo'], w['n_x_pre'], w['n_x_post'],
               batch, seq, t['xattn'], w['layer'])
    x = _ffn(x, w['w_ffn_in'], w['w_ffn_out'], w['n_ffn_pre'], w['n_ffn_post'], t['ffn'], 256,
             w['layer'])
    return x, k, v, u, st


def kernel(x_prompt, x_sample, mem_prompt, cache_sb_k, cache_sb_v, state_pool, state_gla, cache_mem_k, cache_mem_v, w_in, w_gla_a2, b_gla_a, gla_norm, w_pool, pool_scale, w_branch_a, w_branch_b, w_branch_c, w_mix_out, mem_norm, w_xq, w_xk, w_xv, w_xo, w_ffn_in, w_ffn_out, norm_mix_pre, norm_mix_post, norm_x_pre, norm_x_post, norm_ffn_pre, norm_ffn_post):
    p = dict(w_in=w_in, w_gla_a2=w_gla_a2, b_gla_a=b_gla_a, gla_norm=gla_norm, w_pool=w_pool,
             pool_scale=pool_scale, w_branch_a=w_branch_a, w_branch_b=w_branch_b,
             w_branch_c=w_branch_c, w_mix_out=w_mix_out, w_xq=w_xq, w_xo=w_xo,
             w_ffn_in=w_ffn_in, w_ffn_out=w_ffn_out, norm_mix_pre=norm_mix_pre,
             norm_mix_post=norm_mix_post, norm_x_pre=norm_x_pre, norm_x_post=norm_x_post,
             norm_ffn_pre=norm_ffn_pre, norm_ffn_post=norm_ffn_post)
    depth = w_in.shape[0]
    bp, tp, d = x_prompt.shape
    bs, ts, _ = x_sample.shape
    mem_len = mem_prompt.shape[1]
    past = cache_sb_k.shape[2]

    mk_all, mv_all = _memkv(mem_prompt.reshape(bp * mem_len, d), mem_norm.reshape(depth, 1, d),
                            w_xk, w_xv, min(512, bp * mem_len))
    mk_p = mk_all.reshape(depth * bp, mem_len, d)
    mv_p = mv_all.reshape(depth * bp, mem_len, d)
    mk_s = cache_mem_k.reshape(depth * bs, mem_len, d)
    mv_s = cache_mem_v.reshape(depth * bs, mem_len, d)
    ck = jnp.transpose(cache_sb_k, (0, 1, 3, 4, 2)).reshape(depth, bs, SB_W, past)
    cv = jnp.transpose(cache_sb_v, (0, 1, 3, 4, 2)).reshape(depth, bs, SB_W, past)

    yp = x_prompt.reshape(bp * tp, d)
    ys = x_sample.reshape(bs * ts, d)
    hist_zero = jnp.zeros((bp, HIST_PAD, POOL_W), F32)
    state_zero = jnp.zeros((bp, GLA_HEADS, GLA_DP, GLA_DP), F32)
    tiles_p = _tiles(bp, tp)
    kv_t = None
    poolp, glap, kss, vss, pools, glas = ([] for _ in range(6))
    stacks = _cast_stacks(p)
    for l in range(depth):
        w = _pack_layer(p, stacks, l)
        yp, k, v, u, st = _layer(
            yp, w, bp, tp,
            functools.partial(_sb_prompt, batch=bp, seq=tp, blk=tiles_p['sb'],
                              nq=4 if tp % (4 * tiles_p['sb']) == 0 else 1),
            hist_zero, 0, state_zero, mk_p, mv_p, lambda b, l=l: l * bp + b,
            kv_stack=(l, depth, bp, tp, kv_t))
        kv_t = (k, v)
        poolp.append(u.reshape(bp, tp, POOL_W)[:, tp - POOL_HIST:])
        glap.append(_unpad_state(st))

        hist = jnp.pad(state_pool[l], ((0, 0), (HIST_PAD - POOL_HIST, 0), (0, 0)))
        ys, k, v, u, st = _layer(
            ys, w, bs, ts,
            functools.partial(_sb_sample, cache_k=ck, cache_v=cv, layer=l, batch=bs, seq=ts,
                              blk=min(256, past)),
            hist, past, _pad_state(state_gla[l]), mk_s, mv_s, lambda b, l=l: l * bs + b)
        kss.append(k.reshape(bs, ts, SB_HEADS, SB_DIM))
        vss.append(v.reshape(bs, ts, SB_HEADS, SB_DIM))
        full = jnp.concatenate([state_pool[l], u.reshape(bs, ts, POOL_W)], axis=1)
        pools.append(full[:, -POOL_HIST:])
        glas.append(_unpad_state(st))

    kp, vp = (jnp.transpose(a.reshape(depth, bp, SB_HEADS, SB_DIM, tp), (0, 1, 4, 2, 3)) for a in kv_t)
    return (yp.reshape(bp, tp, d), ys.reshape(bs, ts, d),
            kp, vp, jnp.stack(poolp), jnp.stack(glap),
            mk_all.reshape(depth, bp, mem_len, X_HEADS, d // X_HEADS),
            mv_all.reshape(depth, bp, mem_len, X_HEADS, d // X_HEADS),
            jnp.stack(kss), jnp.stack(vss), jnp.stack(pools), jnp.stack(glas))
```

```python
import functools

import numpy as np
import jax
import jax.numpy as jnp
from jax import lax
from jax.experimental import pallas as pl
from jax.experimental.pallas import tpu as pltpu

BF = jnp.bfloat16
F32 = jnp.float32
EPS = 1e-6
LOG2_E = 1.4426950408889634

LANE = 128
VMEM_LIMIT = 56 * 1024 * 1024

SB_HEADS = 6
SB_DIM = 64
SB_W = SB_HEADS * SB_DIM
POOL_WINDOWS = (2, 4, 8, 16)
POOL_GDIM = 64
POOL_W = len(POOL_WINDOWS) * POOL_GDIM
POOL_HIST = max(POOL_WINDOWS) - 1
HIST_PAD = 16
assert all(w & (w - 1) == 0 for w in POOL_WINDOWS) and 2 * POOL_GDIM == LANE and POOL_HIST <= HIST_PAD
GLA_HEADS = 4
GLA_D = 96
GLA_DP = LANE
GLA_W = GLA_HEADS * GLA_D
GLA_WP = GLA_HEADS * GLA_DP
GLA_RANK = 16
GLA_TAU = 16.0
X_HEADS = 4
XATTN_SHORT_SEQ = 128

C_Q = 0
C_K = C_Q + SB_W
C_V = C_K + SB_W
C_U = C_V + SB_W
C_GLA = C_U + POOL_W
C_LR = C_GLA + 4 * GLA_WP
C_END = C_LR + LANE


def _layer_spec(stacked, layer):
    return pl.BlockSpec((None,) + stacked.shape[1:], lambda *_: (layer, 0, 0),
                        pipeline_mode=pl.Buffered(1))


def _cparams(sem):
    return pltpu.CompilerParams(dimension_semantics=sem, vmem_limit_bytes=VMEM_LIMIT)


def _rms(x, g):
    return x * lax.rsqrt(jnp.mean(x * x, axis=-1, keepdims=True) + EPS) * g


def _softplus_neg_abs(z):
    return jnp.log(1.0 + jnp.exp(-jnp.abs(z)))


def _sigmoid(z):
    return 0.5 * jnp.tanh(0.5 * z) + 0.5


def _dot(a, b):
    return jnp.dot(a, b, preferred_element_type=F32)


def _dot_nt(a, b):
    return lax.dot_general(a, b, (((1,), (1,)), ((), ())), preferred_element_type=F32)


def _dot_tn(a, b):
    return lax.dot_general(a, b, (((0,), (0,)), ((), ())), preferred_element_type=F32)


def _const_spec(shape):
    zeros = (0,) * len(shape)
    return pl.BlockSpec(shape, lambda *_: zeros, pipeline_mode=pl.Buffered(1))


def _proj_kernel(*refs, n_alias, transposed_kv):
    x_ref, g_ref, w_ref, wa_ref, ba_ref = refs[:N_PROJ_IN]
    q_ref, k_ref, v_ref, kb_ref, vb_ref, u_ref, gla_ref, la_ref = refs[N_PROJ_IN + n_alias:]
    h = _rms(x_ref[...], g_ref[...]).astype(BF)

    def mm(lo, width):
        return _dot_nt(h, w_ref[lo:lo + width, :])

    q_ref[...] = (mm(C_Q, SB_W) * (SB_DIM ** -0.5)).astype(BF)
    for lo, f32_ref, bf_ref in ((C_K, k_ref, kb_ref), (C_V, v_ref, vb_ref)):
        kv = mm(lo, SB_W)
        bf_ref[...] = kv.astype(BF)
        if transposed_kv is None:
            f32_ref[...] = kv
        else:
            for l in range(f32_ref.shape[0]):
                f32_ref[l, 0] = kv.T if l == transposed_kv else jnp.zeros((SB_W, kv.shape[0]), F32)
    u_ref[...] = mm(C_U, POOL_W)
    for j in range(4):
        gla_ref[:, j * GLA_WP:(j + 1) * GLA_WP] = mm(C_GLA + j * GLA_WP, GLA_WP).astype(BF)
    xa = _dot(mm(C_LR, LANE).astype(BF), wa_ref[...]) + ba_ref[...]
    la_ref[...] = ((jnp.minimum(xa, 0.0) - _softplus_neg_abs(xa)) * (LOG2_E / GLA_TAU)).astype(BF)


N_PROJ_IN = 5


def _proj(x, g, w, w_layer, wa, ba, tm, kv_stack=None):
    n, d = x.shape
    row = lambda width: pl.BlockSpec((tm, width), lambda i: (i, 0))
    if kv_stack is None:
        kv_spec, kv_shape, extra, aliases = row(SB_W), (n, SB_W), [], {}
    else:
        layer, depth, batch, seq, bufs = kv_stack
        nt = seq // tm
        kv_shape = (depth, batch, SB_W, seq)
        if bufs is None:
            kv_spec = pl.BlockSpec((depth, 1, SB_W, tm), lambda i: (0, i // nt, 0, i % nt))
            kv_slot = layer
        else:
            kv_spec = pl.BlockSpec((1, 1, SB_W, tm), lambda i: (layer, i // nt, 0, i % nt))
            kv_slot = 0
        extra = [] if bufs is None else list(bufs)
        aliases = {N_PROJ_IN + j: 1 + j for j in range(len(extra))}
    outs = [
        (row(SB_W), (n, SB_W), BF), (kv_spec, kv_shape, F32), (kv_spec, kv_shape, F32),
        (row(SB_W), (n, SB_W), BF), (row(SB_W), (n, SB_W), BF), (row(POOL_W), (n, POOL_W), F32),
        (row(4 * GLA_WP), (n, 4 * GLA_WP), BF), (row(GLA_WP), (n, GLA_WP), BF),
    ]
    return pl.pallas_call(
        functools.partial(_proj_kernel, n_alias=len(extra),
                          transposed_kv=None if kv_stack is None else kv_slot),
        grid=(n // tm,),
        in_specs=[row(d), _const_spec((1, d)), _layer_spec(w, w_layer),
                  _const_spec((LANE, GLA_WP)), _const_spec((1, GLA_WP))]
                 + [pl.BlockSpec(memory_space=pl.ANY)] * len(extra),
        out_specs=[spec for spec, _, _ in outs],
        out_shape=[jax.ShapeDtypeStruct(shape, dt) for _, shape, dt in outs],
        input_output_aliases=aliases,
        compiler_params=_cparams(("parallel",)),
        name="in_proj",
    )(x, g, w, wa, ba, *extra)


def _upper_ones(n):
    r = lax.broadcasted_iota(jnp.int32, (n, n), 0)
    c = lax.broadcasted_iota(jnp.int32, (n, n), 1)
    return jnp.where(r >= c, 1.0, 0.0).astype(BF)


def _strict_causal(n):
    r = lax.broadcasted_iota(jnp.int32, (n, n), 0)
    c = lax.broadcasted_iota(jnp.int32, (n, n), 1)
    return c < r


def _sb_weights(z, carry, upper, mask):
    return _sb_finish(z, _dot(_sb_softplus(z, mask), upper), carry, mask)


def _sb_softplus(z, mask):
    zb = z.astype(BF)
    t = jnp.maximum(zb, 0.0) + _softplus_neg_abs(zb)
    return t if mask is None else jnp.where(mask, t, jnp.zeros_like(t))


def _sb_finish(z, cum, carry, mask):
    w = jnp.exp(z - cum - carry)
    if mask is not None:
        w = jnp.where(mask, w, 0.0)
    return w.astype(BF), carry + cum[:, 0:1]


SB_SKIP = "skip"
SB_LAG = 1


def _sb_prompt_kernel(q_ref, k_ref, v_ref, o_ref, acc_ref, *, blk, nq):
    first = pl.program_id(1) * nq
    pair_w = 2 * SB_DIM
    n_pairs = SB_HEADS // 2
    upper = _upper_ones(blk)
    causal = _strict_causal(blk)
    low = lax.broadcasted_iota(jnp.int32, (blk, pair_w), 1) < SB_DIM
    zero = jnp.zeros((blk, pair_w), BF)

    def split(x):
        return jnp.where(low, x, zero), jnp.where(low, zero, x)

    q_heads = []
    for s in range(nq):
        heads = []
        for p in range(n_pairs):
            heads.extend(split(q_ref[s * blk:(s + 1) * blk, p * pair_w:(p + 1) * pair_w]))
        q_heads.append(heads)
    acc_ref[...] = jnp.zeros(acc_ref.shape, F32)

    def visit(kb, carries, masks):
        start = pl.multiple_of(kb * blk, blk)
        out = list(carries)
        pairs = [slice(p * pair_w, (p + 1) * pair_w) for p in range(n_pairs)]
        live = [(s, 2 * p + e) for p in range(n_pairs) for s in range(nq)
                if masks[s] is not SB_SKIP for e in range(2)]
        k_pairs = [k_ref[pl.ds(start, blk), ps] for ps in pairs]
        v_pairs = [jnp.concatenate(split(v_ref[pl.ds(start, blk), ps]), axis=0) for ps in pairs]
        z, cum, w = {}, {}, {}
        for i in range(len(live) + 2 * SB_LAG):
            if i < len(live):
                s, h = live[i]
                z[i] = _dot_nt(q_heads[s][h], k_pairs[h // 2])
            j = i - SB_LAG
            if 0 <= j < len(live):
                cum[j] = _dot(_sb_softplus(z[j], masks[live[j][0]]), upper)
            j = i - 2 * SB_LAG
            if 0 <= j < len(live):
                s, h = live[j]
                c = s * SB_HEADS + h
                w[j], out[c] = _sb_finish(z[j], cum[j], carries[c], masks[s])
                if h % 2 == 1:
                    w_pair = jnp.concatenate([w[j - 1], w[j]], axis=1)
                    ps = pairs[h // 2]
                    acc_ref[s * blk:(s + 1) * blk, ps] += _dot(w_pair, v_pairs[h // 2])
        return tuple(out)

    carries = tuple(jnp.zeros((blk, 1), F32) for _ in range(nq * SB_HEADS))
    for d in reversed(range(nq)):
        masks = tuple(SB_SKIP if d > s else (causal if d == s else None) for s in range(nq))
        carries = visit(first + d, carries, masks)
    lax.fori_loop(0, first, lambda j, c: visit(first - 1 - j, c, (None,) * nq), carries)
    o_ref[...] = acc_ref[...].astype(BF)


def _sb_prompt(q, kb, vb, batch, seq, blk, nq):
    nblk = seq // (blk * nq)
    return pl.pallas_call(
        functools.partial(_sb_prompt_kernel, blk=blk, nq=nq),
        grid=(batch, nblk),
        in_specs=[
            pl.BlockSpec((nq * blk, SB_W), lambda b, i: (b * nblk + i, 0)),
            pl.BlockSpec((seq, SB_W), lambda b, i: (b, 0)),
            pl.BlockSpec((seq, SB_W), lambda b, i: (b, 0)),
        ],
        out_specs=pl.BlockSpec((nq * blk, SB_W), lambda b, i: (b * nblk + i, 0)),
        out_shape=jax.ShapeDtypeStruct((batch * seq, SB_W), BF),
        scratch_shapes=[pltpu.VMEM((nq * blk, SB_W), F32)],
        compiler_params=_cparams(("parallel", "arbitrary")),
        name="sb_prompt",
    )(q, kb, vb)


def _sb_sample_kernel(q_ref, kn_ref, vn_ref, kc_ref, vc_ref, o_ref, *, seq, past, blk):
    pair_w = 2 * SB_DIM
    n_pairs = SB_HEADS // 2
    pairs = [slice(p * pair_w, (p + 1) * pair_w) for p in range(n_pairs)]

    def split(x, axis):
        low = lax.broadcasted_iota(jnp.int32, x.shape, axis) < SB_DIM
        zero = jnp.zeros(x.shape, x.dtype)
        return jnp.where(low, x, zero), jnp.where(low, zero, x)

    q_heads = []
    for ps in pairs:
        q_heads.extend(split(q_ref[:, ps], 1))
    r = lax.broadcasted_iota(jnp.int32, (SB_HEADS * seq, seq), 0) % seq
    causal = lax.broadcasted_iota(jnp.int32, (SB_HEADS * seq, seq), 1) < r

    def visit(k_pairs, v_pairs, transposed, carry, acc, upper, mask):
        qk = _dot if transposed else _dot_nt
        pv = _dot_nt if transposed else _dot
        z = jnp.concatenate([qk(q_heads[h], k_pairs[h // 2]) for h in range(SB_HEADS)], axis=0)
        w, carry = _sb_weights(z, carry, upper, mask)
        out = []
        for p in range(n_pairs):
            v_lo, v_hi = split(v_pairs[p], 0 if transposed else 1)
            w_lo = w[(2 * p) * seq:(2 * p + 1) * seq]
            w_hi = w[(2 * p + 1) * seq:(2 * p + 2) * seq]
            out.append(acc[p] + pv(w_lo, v_lo) + pv(w_hi, v_hi))
        return carry, out

    carry = jnp.zeros((SB_HEADS * seq, 1), F32)
    acc = [jnp.zeros((seq, pair_w), F32) for _ in pairs]
    carry, acc = visit([kn_ref[:, ps] for ps in pairs], [vn_ref[:, ps] for ps in pairs], False,
                       carry, acc, _upper_ones(seq), causal)
    upper_past = _upper_ones(blk)
    for j in reversed(range(past // blk)):
        cols = slice(j * blk, (j + 1) * blk)
        carry, acc = visit([kc_ref[0, 0, ps, cols].astype(BF) for ps in pairs],
                           [vc_ref[0, 0, ps, cols].astype(BF) for ps in pairs], True,
                           carry, acc, upper_past, None)
    for p, ps in enumerate(pairs):
        o_ref[:, ps] = acc[p].astype(BF)


def _sb_sample(q, kb, vb, cache_k, cache_v, layer, batch, seq, blk):
    past = cache_k.shape[3]
    new = pl.BlockSpec((seq, SB_W), lambda b: (b, 0))
    cache = pl.BlockSpec((1, 1, SB_W, past), lambda b: (layer, b, 0, 0))
    return pl.pallas_call(
        functools.partial(_sb_sample_kernel, seq=seq, past=past, blk=blk),
        grid=(batch,),
        in_specs=[new, new, new, cache, cache],
        out_specs=new,
        out_shape=jax.ShapeDtypeStruct((batch * seq, SB_W), BF),
        compiler_params=_cparams(("parallel",)),
        name="sb_sample",
    )(q, kb, vb, cache_k, cache_v)


def _pool_kernel(u_ref, hist_ref, w_ref, scale_ref, o_ref, buf_ref, *, tm, pos0):
    i = pl.program_id(1)

    @pl.when(i == 0)
    def _():
        buf_ref[0:HIST_PAD, :] = hist_ref[0]

    @pl.when(i > 0)
    def _():
        buf_ref[0:HIST_PAD, :] = buf_ref[tm:tm + HIST_PAD, :]

    u = u_ref[...]
    buf_ref[HIST_PAD:HIST_PAD + tm, :] = u
    pooled = _pool_rows(buf_ref[0:tm + HIST_PAD, :], u, pos0 + i * tm)
    o_ref[...] = (_dot(pooled.astype(BF), w_ref[...]) * scale_ref[...]).astype(BF)


def _pool_rows(full, u, first_pos):
    tm = u.shape[0]
    low = lax.broadcasted_iota(jnp.int32, (tm, LANE), 1) < POOL_GDIM
    pos1 = first_pos + lax.broadcasted_iota(jnp.int32, (tm, LANE), 0) + 1
    pooled = []
    for tile in range(POOL_W // LANE):
        cols = slice(tile * LANE, (tile + 1) * LANE)
        w_lo, w_hi = POOL_WINDOWS[2 * tile], POOL_WINDOWS[2 * tile + 1]
        s = full[:, cols]
        sums, w = {}, 1
        while w < w_hi:
            s = s + pltpu.roll(s, w, 0)
            w *= 2
            sums[w] = s
        win_sum = jnp.where(low, sums[w_lo][HIST_PAD:], sums[w_hi][HIST_PAD:])
        cnt = jnp.minimum(pos1, jnp.where(low, w_lo, w_hi)).astype(F32)
        pooled.append(win_sum / cnt - u[:, cols])
    return jnp.concatenate(pooled, axis=1)


def _pool(u, hist, w_bd, scale, batch, seq, tm, pos0):
    nt = seq // tm
    return pl.pallas_call(
        functools.partial(_pool_kernel, tm=tm, pos0=pos0),
        grid=(batch, nt),
        in_specs=[
            pl.BlockSpec((tm, POOL_W), lambda b, i: (b * nt + i, 0)),
            pl.BlockSpec((1, HIST_PAD, POOL_W), lambda b, i: (b, 0, 0)),
            _const_spec((POOL_W, POOL_W)),
            _const_spec((1, POOL_W)),
        ],
        out_specs=pl.BlockSpec((tm, POOL_W), lambda b, i: (b * nt + i, 0)),
        out_shape=jax.ShapeDtypeStruct((batch * seq, POOL_W), BF),
        scratch_shapes=[pltpu.VMEM((tm + HIST_PAD, POOL_W), F32)],
        compiler_params=_cparams(("parallel", "arbitrary")),
        name="pool_mix",
    )(u, hist, w_bd, scale)


def _gla_levels(rows):
    out, s = [], 1
    while s < rows:
        out.append(s)
        s *= 2
    return out


def _gla_exponent_matrix(rows):
    t = np.arange(rows)[:, None]
    j = np.arange(rows)[None, :]
    blocks = [j <= t]
    for s in _gla_levels(rows)[1:]:
        start = (t // s) * s
        later = (t // s) % 2 == 1
        blocks.append(np.where(later, (j >= start) & (j <= t), (j > t) & (j <= start + s - 1)))
    return np.concatenate(blocks, axis=0).astype(np.float32)


def _gla_later_mask(rows):
    t = np.arange(rows)[:, None]
    blocks = [np.broadcast_to((t // s) % 2 == 1, (rows, GLA_WP)) for s in _gla_levels(rows)]
    return np.concatenate(blocks, axis=0).astype(np.float32)


def _gla_kernel(qkvg_ref, la_ref, em_ref, lm_ref, gn_ref, s0_ref, o_ref, st_ref, *, rows, sub):
    @pl.when(pl.program_id(1) == 0)
    def _():
        st_ref[...] = s0_ref[...]

    for j in range(sub):
        _gla_tile(qkvg_ref, la_ref, em_ref, lm_ref, gn_ref, o_ref, st_ref,
                  slice(j * rows, (j + 1) * rows), rows)


def _gla_tile(qkvg_ref, la_ref, em_ref, lm_ref, gn_ref, o_ref, st_ref, rs, rows):
    expo = _dot(em_ref[...], la_ref[rs, :])

    qb = qkvg_ref[rs, 0:GLA_WP]
    kb = qkvg_ref[rs, GLA_WP:2 * GLA_WP]
    vb = qkvg_ref[rs, 2 * GLA_WP:3 * GLA_WP]
    gate = qkvg_ref[rs, 3 * GLA_WP:4 * GLA_WP].astype(F32)

    b = expo[0:rows]
    b_last = b[rows - 1:rows]
    q_dec = qb * jnp.exp2(b).astype(BF)
    k_dec = kb * jnp.exp2(b_last - b).astype(BF)

    row = lax.broadcasted_iota(jnp.int32, (rows, GLA_WP), 0)
    r = lax.broadcasted_iota(jnp.int32, (rows, rows), 0)
    c = lax.broadcasted_iota(jnp.int32, (rows, rows), 1)
    heads = [slice(h * GLA_DP, (h + 1) * GLA_DP) for h in range(GLA_HEADS)]

    att = [jnp.where(r == c, _dot_nt(qb[:, hs], kb[:, hs]), 0.0) for hs in heads]
    for n, s in enumerate(_gla_levels(rows)):
        shift = s.bit_length() - 1
        rb = r >> shift
        pair = ((rb & 1) == 1) & ((c >> shift) == rb - 1)
        if s == 1:
            e = jnp.where((row & 1) == 1, la_ref[rs, :].astype(F32), 0.0)
        else:
            e = expo[n * rows:(n + 1) * rows]
        later = lm_ref[n * rows:(n + 1) * rows] > 0
        x_lvl = jnp.where(later, qb, kb) * jnp.exp2(e).astype(BF)
        for h, hs in enumerate(heads):
            att[h] = jnp.where(pair, _dot_nt(x_lvl[:, hs], x_lvl[:, hs]), att[h])

    for h, hs in enumerate(heads):
        st = st_ref[0, h]
        o = (_dot(att[h].astype(BF), vb[:, hs]) + _dot_nt(q_dec[:, hs], st.astype(BF))) * (GLA_D ** -0.5)
        ms = jnp.sum(o * o, axis=-1, keepdims=True) * (1.0 / GLA_D)
        y = o * lax.rsqrt(ms + EPS) * gn_ref[:, hs]
        gh = gate[:, hs]
        o_ref[rs, hs] = (y * (gh * _sigmoid(gh))).astype(BF)
        st_ref[0, h] = st * jnp.exp2(b_last[:, hs]) + _dot_tn(vb[:, hs], k_dec[:, hs])


def _gla(qkvg, log_a, gn, s0t, batch, seq, rows, sub):
    nt = seq // (rows * sub)
    em = jnp.asarray(_gla_exponent_matrix(rows), BF)
    lm = jnp.asarray(_gla_later_mask(rows), BF)
    state = pl.BlockSpec((1, GLA_HEADS, GLA_DP, GLA_DP), lambda b, i: (b, 0, 0, 0))
    return pl.pallas_call(
        functools.partial(_gla_kernel, rows=rows, sub=sub),
        grid=(batch, nt),
        in_specs=[
            pl.BlockSpec((rows * sub, 4 * GLA_WP), lambda b, i: (b * nt + i, 0)),
            pl.BlockSpec((rows * sub, GLA_WP), lambda b, i: (b * nt + i, 0)),
            _const_spec(em.shape),
            _const_spec(lm.shape),
            _const_spec((1, GLA_WP)),
            state,
        ],
        out_specs=[pl.BlockSpec((rows * sub, GLA_WP), lambda b, i: (b * nt + i, 0)), state],
        out_shape=[jax.ShapeDtypeStruct((batch * seq, GLA_WP), BF),
                   jax.ShapeDtypeStruct((batch, GLA_HEADS, GLA_DP, GLA_DP), F32)],
        compiler_params=_cparams(("parallel", "arbitrary")),
        name="gla",
    )(qkvg, log_a, em, lm, gn, s0t)


def _merge_kernel(x_ref, oa_ref, ob_ref, oc_ref, win_ref, wa_ref, wb_ref, wc_ref, wo_ref,
                  gpre_ref, gpost_ref, y_ref, *, d_model, gate_row0):
    _merge_rows(x_ref, (oa_ref[...], ob_ref[...], oc_ref[...]), win_ref, (wa_ref, wb_ref, wc_ref),
                wo_ref, gpre_ref, gpost_ref, y_ref, d_model, gate_row0)


def _merge_rows(x_ref, branches, win_ref, w_refs, wo_ref, gpre_ref, gpost_ref, y_ref, d_model, gate_row0):
    x = x_ref[...]
    h = _rms(x, gpre_ref[...]).astype(BF)
    merged = None
    for j, (o, w_ref) in enumerate(zip(branches, w_refs)):
        rows = slice(gate_row0 + j * d_model, gate_row0 + (j + 1) * d_model)
        gate = _sigmoid(_dot_nt(h, win_ref[rows, :]))
        term = gate * _dot(o, w_ref[...])
        merged = term if merged is None else merged + term
    mix = _dot(merged.astype(BF), wo_ref[...])
    y_ref[...] = x + _rms(mix, gpost_ref[...])


def _merge_pool_kernel(x_ref, oa_ref, u_ref, uprev_ref, hist_ref, wp_ref, sc_ref, oc_ref, win_ref,
                       wa_ref, wb_ref, wc_ref, wo_ref, gpre_ref, gpost_ref, y_ref,
                       *, d_model, gate_row0, tiles_per_seq, pos0):
    j = pl.program_id(0) % tiles_per_seq
    u = u_ref[...]
    prev = jnp.where(j == 0, hist_ref[0], uprev_ref[...])
    pooled = _pool_rows(jnp.concatenate([prev, u], axis=0), u, pos0 + j * u.shape[0])
    ob = (_dot(pooled.astype(BF), wp_ref[...]) * sc_ref[...]).astype(BF)
    _merge_rows(x_ref, (oa_ref[...], ob, oc_ref[...]), win_ref, (wa_ref, wb_ref, wc_ref),
                wo_ref, gpre_ref, gpost_ref, y_ref, d_model, gate_row0)


def _merge_pool(x, oa, u, hist, w_pool, scale, oc, w_in_t, wa, wb, wc, wo, gpre, gpost, tm, layer,
                seq, pos0):
    n, d = x.shape
    nt = seq // tm
    per = tm // HIST_PAD
    row = lambda width: pl.BlockSpec((tm, width), lambda i: (i, 0))
    return pl.pallas_call(
        functools.partial(_merge_pool_kernel, d_model=d, gate_row0=w_in_t.shape[1] - 3 * d,
                          tiles_per_seq=nt, pos0=pos0),
        grid=(n // tm,),
        in_specs=[row(d), row(SB_W), row(POOL_W),
                  pl.BlockSpec((HIST_PAD, POOL_W), lambda i: (jnp.maximum(i * per - 1, 0), 0)),
                  pl.BlockSpec((1, HIST_PAD, POOL_W), lambda i: (i // nt, 0, 0)),
                  _const_spec((POOL_W, POOL_W)), _const_spec((1, POOL_W)), row(GLA_WP)]
                 + [_layer_spec(w, layer) for w in (w_in_t, wa, wb, wc, wo)]
                 + [_const_spec((1, d)), _const_spec((1, d))],
        out_specs=row(d),
        out_shape=jax.ShapeDtypeStruct((n, d), F32),
        compiler_params=_cparams(("parallel",)),
        name="merge_pool_out",
    )(x, oa, u, u, hist, w_pool, scale, oc, w_in_t, wa, wb, wc, wo, gpre, gpost)


def _merge(x, oa, ob, oc, w_in_t, wa, wb, wc, wo, gpre, gpost, tm, layer):
    n, d = x.shape
    row = lambda width: pl.BlockSpec((tm, width), lambda i: (i, 0))
    return pl.pallas_call(
        functools.partial(_merge_kernel, d_model=d, gate_row0=w_in_t.shape[1] - 3 * d),
        grid=(n // tm,),
        in_specs=[row(d), row(SB_W), row(POOL_W), row(GLA_WP)]
                 + [_layer_spec(w, layer) for w in (w_in_t, wa, wb, wc, wo)]
                 + [_const_spec((1, d)), _const_spec((1, d))],
        out_specs=row(d),
        out_shape=jax.ShapeDtypeStruct((n, d), F32),
        compiler_params=_cparams(("parallel",)),
        name="merge_out",
    )(x, oa, ob, oc, w_in_t, wa, wb, wc, wo, gpre, gpost)


def _xattn_kernel(x_ref, mk_ref, mv_ref, wq_ref, wo_ref, gpre_ref, gpost_ref, y_ref, *, d_model):
    x = x_ref[...]
    q = _dot(_rms(x, gpre_ref[...]).astype(BF), wq_ref[...]).astype(BF)
    hd = d_model // X_HEADS
    heads = [slice(h * hd, (h + 1) * hd) for h in range(X_HEADS)]
    s, o = {}, {}
    for i in range(X_HEADS + 1):
        if i < X_HEADS:
            s[i] = _dot_nt(q[:, heads[i]], mk_ref[0, :, heads[i]].astype(BF)) * (hd ** -0.5)
        j = i - 1
        if 0 <= j < X_HEADS:
            p = jnp.exp(s[j] - jnp.max(s[j], axis=-1, keepdims=True))
            p = p / jnp.sum(p, axis=-1, keepdims=True)
            o[j] = _dot(p.astype(BF), mv_ref[0, :, heads[j]].astype(BF)).astype(BF)
    attn = jnp.concatenate([o[h] for h in range(X_HEADS)], axis=-1)
    y_ref[...] = x + _rms(_dot(attn, wo_ref[...]), gpost_ref[...])


def _xattn_short_kernel(x_ref, mk_ref, mv_ref, wq_ref, wo_ref, gpre_ref, gpost_ref, y_ref,
                        q_scr, o_scr, *, d_model, seq):
    b = pl.program_id(0)

    @pl.when(b == 0)
    def _():
        q_scr[...] = _dot(_rms(x_ref[...], gpre_ref[...]).astype(BF), wq_ref[...]).astype(BF)

    rows = pl.ds(pl.multiple_of(b * seq, seq), seq)
    hd = d_model // X_HEADS
    for h in range(X_HEADS):
        hs = slice(h * hd, (h + 1) * hd)
        s = _dot_nt(q_scr[rows, hs], mk_ref[0, :, hs].astype(BF)) * (hd ** -0.5)
        p = jnp.exp(s - jnp.max(s, axis=-1, keepdims=True))
        p = p / jnp.sum(p, axis=-1, keepdims=True)
        o_scr[rows, hs] = _dot(p.astype(BF), mv_ref[0, :, hs].astype(BF)).astype(BF)

    @pl.when(b == pl.num_programs(0) - 1)
    def _():
        y_ref[...] = x_ref[...] + _rms(_dot(o_scr[...], wo_ref[...]), gpost_ref[...])


def _xattn_short(x, mk, mv, mem_index, wq, wo, gpre, gpost, batch, seq, layer):
    n, d = x.shape
    mem_len = mk.shape[-2]
    mem = pl.BlockSpec((1, mem_len, d), lambda b: (mem_index(b), 0, 0))
    return pl.pallas_call(
        functools.partial(_xattn_short_kernel, d_model=d, seq=seq),
        grid=(batch,),
        in_specs=[_const_spec((n, d)), mem, mem, _layer_spec(wq, layer), _layer_spec(wo, layer),
                  _const_spec((1, d)), _const_spec((1, d))],
        out_specs=pl.BlockSpec((n, d), lambda b: (0, 0)),
        out_shape=jax.ShapeDtypeStruct((n, d), F32),
        scratch_shapes=[pltpu.VMEM((n, d), BF), pltpu.VMEM((n, d), BF)],
        compiler_params=_cparams(("arbitrary",)),
        name="cross_attn_short",
    )(x, mk, mv, wq, wo, gpre, gpost)


def _xattn(x, mk, mv, mem_index, wq, wo, gpre, gpost, batch, seq, tm, layer):
    n, d = x.shape
    if seq < XATTN_SHORT_SEQ:
        return _xattn_short(x, mk, mv, mem_index, wq, wo, gpre, gpost, batch, seq, layer)
    nt = seq // tm
    mem_len = mk.shape[-2]
    row = pl.BlockSpec((tm, d), lambda b, i: (b * nt + i, 0))
    mem = pl.BlockSpec((1, mem_len, d), lambda b, i: (mem_index(b), 0, 0))
    return pl.pallas_call(
        functools.partial(_xattn_kernel, d_model=d),
        grid=(batch, nt),
        in_specs=[row, mem, mem, _layer_spec(wq, layer), _layer_spec(wo, layer),
                  _const_spec((1, d)), _const_spec((1, d))],
        out_specs=row,
        out_shape=jax.ShapeDtypeStruct((n, d), F32),
        compiler_params=_cparams(("parallel", "parallel")),
        name="cross_attn",
    )(x, mk, mv, wq, wo, gpre, gpost)


def _ffn_kernel(x_ref, wgu_ref, wo_ref, gpre_ref, gpost_ref, y_ref, *, chunk):
    x = x_ref[...]
    h = _rms(x, gpre_ref[...]).astype(BF)
    ff = wo_ref.shape[0]
    acc = jnp.zeros(x.shape, F32)
    for j in range(ff // chunk):
        cs = slice(j * chunk, (j + 1) * chunk)
        g = _dot(h, wgu_ref[:, cs])
        u = _dot(h, wgu_ref[:, ff + j * chunk:ff + (j + 1) * chunk])
        act = (g * _sigmoid(g) * u).astype(BF)
        acc = acc + _dot(act, wo_ref[cs, :])
    y_ref[...] = x + _rms(acc, gpost_ref[...])


def _ffn(x, wgu, wo, gpre, gpost, tm, chunk, layer):
    n, d = x.shape
    row = pl.BlockSpec((tm, d), lambda i: (i, 0))
    return pl.pallas_call(
        functools.partial(_ffn_kernel, chunk=chunk),
        grid=(n // tm,),
        in_specs=[row, _layer_spec(wgu, layer), _layer_spec(wo, layer),
                  _const_spec((1, d)), _const_spec((1, d))],
        out_specs=row,
        out_shape=jax.ShapeDtypeStruct((n, d), F32),
        compiler_params=_cparams(("parallel",)),
        name="ffn",
    )(x, wgu, wo, gpre, gpost)


def _memkv_kernel(m_ref, g_ref, wk_ref, wv_ref, k_ref, v_ref):
    m = _rms(m_ref[...], g_ref[0]).astype(BF)
    k_ref[0] = _dot(m, wk_ref[0].astype(BF))
    v_ref[0] = _dot(m, wv_ref[0].astype(BF))


def _memkv(mem, g, wk, wv, tm):
    depth, d, _ = wk.shape
    n = mem.shape[0]
    w = pl.BlockSpec((1, d, d), lambda l, i: (l, 0, 0))
    out = pl.BlockSpec((1, tm, d), lambda l, i: (l, i, 0))
    return pl.pallas_call(
        _memkv_kernel,
        grid=(depth, n // tm),
        in_specs=[pl.BlockSpec((tm, d), lambda l, i: (i, 0)),
                  pl.BlockSpec((1, 1, d), lambda l, i: (l, 0, 0)), w, w],
        out_specs=[out, out],
        out_shape=[jax.ShapeDtypeStruct((depth, n, d), F32)] * 2,
        compiler_params=_cparams(("parallel", "parallel")),
        name="mem_kv",
    )(mem, g, wk, wv)


def _pad_heads(w, axis):
    shape = w.shape
    w = w.reshape(shape[:axis] + (GLA_HEADS, GLA_D) + shape[axis + 1:])
    pad = [(0, 0)] * w.ndim
    pad[axis + 1] = (0, GLA_DP - GLA_D)
    w = jnp.pad(w, pad)
    return w.reshape(shape[:axis] + (GLA_WP,) + shape[axis + 1:])


def _cast_stacks(p):
    w_in_t = jnp.swapaxes(p['w_in'], 1, 2).astype(BF)
    o = 3 * SB_W + POOL_W
    gla = [_pad_heads(w_in_t[:, o + j * GLA_W:o + (j + 1) * GLA_W], 1) for j in range(4)]
    o += 4 * GLA_W
    lr = jnp.pad(w_in_t[:, o:o + GLA_RANK], ((0, 0), (0, LANE - GLA_RANK), (0, 0)))
    return dict(
        w_in_t=w_in_t,
        w_in=jnp.concatenate([w_in_t[:, :3 * SB_W + POOL_W]] + gla + [lr], axis=1),
        w_a=p['w_branch_a'].astype(BF),
        w_b=p['w_branch_b'].astype(BF),
        w_c=_pad_heads(p['w_branch_c'], 1).astype(BF),
        w_mix=p['w_mix_out'].astype(BF),
        w_xq=p['w_xq'].astype(BF),
        w_xo=p['w_xo'].astype(BF),
        w_ffn_in=p['w_ffn_in'].astype(BF),
        w_ffn_out=p['w_ffn_out'].astype(BF),
    )


def _pack_layer(p, stacks, l):
    eye = jnp.eye(len(POOL_WINDOWS), dtype=F32)
    w_pool_bd = (eye[:, None, :, None] * p['w_pool'][l][:, :, None, :]).reshape(POOL_W, POOL_W)
    row = lambda v: v.reshape(1, -1)
    return dict(
        stacks, layer=l,
        w_a2=jnp.pad(_pad_heads(p['w_gla_a2'][l], 1), ((0, LANE - GLA_RANK), (0, 0))).astype(BF),
        b_a=row(_pad_heads(p['b_gla_a'][l], 0)),
        gla_norm=row(_pad_heads(p['gla_norm'][l], 0)),
        w_pool=w_pool_bd.astype(BF),
        pool_scale=row(p['pool_scale'][l]),
        n_mix_pre=row(p['norm_mix_pre'][l]), n_mix_post=row(p['norm_mix_post'][l]),
        n_x_pre=row(p['norm_x_pre'][l]), n_x_post=row(p['norm_x_post'][l]),
        n_ffn_pre=row(p['norm_ffn_pre'][l]), n_ffn_post=row(p['norm_ffn_post'][l]),
    )


def _pad_state(s):
    s = jnp.swapaxes(s, -1, -2)
    return jnp.pad(s, ((0, 0), (0, 0), (0, GLA_DP - GLA_D), (0, GLA_DP - GLA_D)))


def _unpad_state(st):
    return jnp.swapaxes(st[:, :, :GLA_D, :GLA_D], -1, -2)


def _tiles(batch, seq):
    n = batch * seq
    return dict(
        proj=min(512, n),
        sb=min(256, seq),
        pool=min(1024, seq),
        gla=min(128, seq),
        merge=min(1024, n),
        xattn=min(1024, seq),
        ffn=min(1024, n),
    )


def _layer(x, w, batch, seq, sb_attend, pool_hist, pos0, s0t, mk, mv, mem_index, kv_stack=None):
    t = _tiles(batch, seq)
    q, k, v, kb, vb, u, qkvg, log_a = _proj(x, w['n_mix_pre'], w['w_in'], w['layer'], w['w_a2'],
                                            w['b_a'], t['proj'], kv_stack)
    oa = sb_attend(q, kb, vb)
    oc, st = _gla(qkvg, log_a, w['gla_norm'], s0t, batch, seq, t['gla'],
                  4 if seq % (4 * t['gla']) == 0 else 1)
    if seq % t['merge'] == 0:
        x = _merge_pool(x, oa, u, pool_hist, w['w_pool'], w['pool_scale'], oc, w['w_in_t'], w['w_a'],
                        w['w_b'], w['w_c'], w['w_mix'], w['n_mix_pre'], w['n_mix_post'], t['merge'],
                        w['layer'], seq, pos0)
    else:
        ob = _pool(u, pool_hist, w['w_pool'], w['pool_scale'], batch, seq, t['pool'], pos0)
        x = _merge(x, oa, ob, oc, w['w_in_t'], w['w_a'], w['w_b'], w['w_c'], w['w_mix'],
                   w['n_mix_pre'], w['n_mix_post'], t['merge'], w['layer'])
    x = _xattn(x, mk, mv, mem_index, w['w_xq'], w['w_xo'], w['n_x_pre'], w['n_x_post'],
               batch, seq, t['xattn'], w['layer'])
    x = _ffn(x, w['w_ffn_in'], w['w_ffn_out'], w['n_ffn_pre'], w['n_ffn_post'], t['ffn'], 256,
             w['layer'])
    return x, k, v, u, st


def kernel(x_prompt, x_sample, mem_prompt, cache_sb_k, cache_sb_v, state_pool, state_gla, cache_mem_k, cache_mem_v, w_in, w_gla_a2, b_gla_a, gla_norm, w_pool, pool_scale, w_branch_a, w_branch_b, w_branch_c, w_mix_out, mem_norm, w_xq, w_xk, w_xv, w_xo, w_ffn_in, w_ffn_out, norm_mix_pre, norm_mix_post, norm_x_pre, norm_x_post, norm_ffn_pre, norm_ffn_post):
    p = dict(w_in=w_in, w_gla_a2=w_gla_a2, b_gla_a=b_gla_a, gla_norm=gla_norm, w_pool=w_pool,
             pool_scale=pool_scale, w_branch_a=w_branch_a, w_branch_b=w_branch_b,
             w_branch_c=w_branch_c, w_mix_out=w_mix_out, w_xq=w_xq, w_xo=w_xo,
             w_ffn_in=w_ffn_in, w_ffn_out=w_ffn_out, norm_mix_pre=norm_mix_pre,
             norm_mix_post=norm_mix_post, norm_x_pre=norm_x_pre, norm_x_post=norm_x_post,
             norm_ffn_pre=norm_ffn_pre, norm_ffn_post=norm_ffn_post)
    depth = w_in.shape[0]
    bp, tp, d = x_prompt.shape
    bs, ts, _ = x_sample.shape
    mem_len = mem_prompt.shape[1]
    past = cache_sb_k.shape[2]

    mk_all, mv_all = _memkv(mem_prompt.reshape(bp * mem_len, d), mem_norm.reshape(depth, 1, d),
                            w_xk, w_xv, min(512, bp * mem_len))
    mk_p = mk_all.reshape(depth * bp, mem_len, d)
    mv_p = mv_all.reshape(depth * bp, mem_len, d)
    mk_s = cache_mem_k.reshape(depth * bs, mem_len, d)
    mv_s = cache_mem_v.reshape(depth * bs, mem_len, d)
    ck = jnp.transpose(cache_sb_k, (0, 1, 3, 4, 2)).reshape(depth, bs, SB_W, past)
    cv = jnp.transpose(cache_sb_v, (0, 1, 3, 4, 2)).reshape(depth, bs, SB_W, past)

    yp = x_prompt.reshape(bp * tp, d)
    ys = x_sample.reshape(bs * ts, d)
    hist_zero = jnp.zeros((bp, HIST_PAD, POOL_W), F32)
    state_zero = jnp.zeros((bp, GLA_HEADS, GLA_DP, GLA_DP), F32)
    tiles_p = _tiles(bp, tp)
    kv_t = None
    poolp, glap, kss, vss, pools, glas = ([] for _ in range(6))
    stacks = _cast_stacks(p)
    for l in range(depth):
        w = _pack_layer(p, stacks, l)
        yp, k, v, u, st = _layer(
            yp, w, bp, tp,
            functools.partial(_sb_prompt, batch=bp, seq=tp, blk=tiles_p['sb'],
                              nq=4 if tp % (4 * tiles_p['sb']) == 0 else 1),
            hist_zero, 0, state_zero, mk_p, mv_p, lambda b, l=l: l * bp + b,
            kv_stack=(l, depth, bp, tp, kv_t))
        kv_t = (k, v)
        poolp.append(u.reshape(bp, tp, POOL_W)[:, tp - POOL_HIST:])
        glap.append(_unpad_state(st))

        hist = jnp.pad(state_pool[l], ((0, 0), (HIST_PAD - POOL_HIST, 0), (0, 0)))
        ys, k, v, u, st = _layer(
            ys, w, bs, ts,
            functools.partial(_sb_sample, cache_k=ck, cache_v=cv, layer=l, batch=bs, seq=ts,
                              blk=min(256, past)),
            hist, past, _pad_state(state_gla[l]), mk_s, mv_s, lambda b, l=l: l * bs + b)
        kss.append(k.reshape(bs, ts, SB_HEADS, SB_DIM))
        vss.append(v.reshape(bs, ts, SB_HEADS, SB_DIM))
        full = jnp.concatenate([state_pool[l], u.reshape(bs, ts, POOL_W)], axis=1)
        pools.append(full[:, -POOL_HIST:])
        glas.append(_unpad_state(st))

    kp, vp = (jnp.transpose(a.reshape(depth, bp, SB_HEADS, SB_DIM, tp), (0, 1, 4, 2, 3)) for a in kv_t)
    return (yp.reshape(bp, tp, d), ys.reshape(bs, ts, d),
            kp, vp, jnp.stack(poolp), jnp.stack(glap),
            mk_all.reshape(depth, bp, mem_len, X_HEADS, d // X_HEADS),
            mv_all.reshape(depth, bp, mem_len, X_HEADS, d // X_HEADS),
            jnp.stack(kss), jnp.stack(vss), jnp.stack(pools), jnp.stack(glas))
```
